```python
import jax, jax.numpy as jnp
from jax import lax
import numpy as np

D_MODEL = 2048
BATCH = 4
SEQ = 4096
DEPTH = 2

GRID_W = 64
CTX_LEN = 256
EPS = 1e-6
N_BRANCH = 4
BRANCH_W = D_MODEL // 2
ATT_HEAD_DIM = 128
ATT_HEADS = BRANCH_W // ATT_HEAD_DIM
ATT_KV_HEADS = ATT_HEADS // 4
ATT_GROUP = ATT_HEADS // ATT_KV_HEADS
ATT_SCALE = ATT_HEAD_DIM ** -0.5
ROPE_THETA = 10000.0
Q_BLOCK = 128
MLSTM_HEADS = 4
MLSTM_HEAD_DIM = BRANCH_W // MLSTM_HEADS
MLSTM_CHUNK = 64
M_INIT = -1e30
CONV_WIDTH = 3
POOL_WINDOWS = (2, 4, 8, 16)
POOL_GROUP = BRANCH_W // len(POOL_WINDOWS)

SPLITS = (
    ATT_HEADS * ATT_HEAD_DIM, ATT_KV_HEADS * ATT_HEAD_DIM, ATT_KV_HEADS * ATT_HEAD_DIM, BRANCH_W,
    BRANCH_W, BRANCH_W, BRANCH_W, BRANCH_W, BRANCH_W, 4 * MLSTM_HEADS,
    BRANCH_W, BRANCH_W, BRANCH_W, BRANCH_W,
    BRANCH_W, BRANCH_W,
    N_BRANCH * D_MODEL,
)
N_IN = sum(SPLITS)

kernel_name = "hybrid_parallel_gated_dit_block"


def split_cols(y):
    idx = np.cumsum(SPLITS)[:-1].tolist()
    return jnp.split(y, idx, axis=-1)


def rms_norm(x, gain):
    xf = x.astype(jnp.float32)
    y = xf * lax.rsqrt(jnp.mean(xf * xf, axis=-1, keepdims=True) + EPS)
    return (y * gain.astype(jnp.float32)).astype(x.dtype)


def heads(a, n, d):
    return a.reshape(a.shape[:2] + (n, d))


def axial_rope(x):
    T = x.shape[1]
    n_rows = T // GRID_W
    row = jnp.broadcast_to(jnp.arange(n_rows, dtype=jnp.float32)[:, None], (n_rows, GRID_W)).reshape(-1)
    col = jnp.broadcast_to(jnp.arange(GRID_W, dtype=jnp.float32)[None, :], (n_rows, GRID_W)).reshape(-1)
    half = ATT_HEAD_DIM // 2
    quarter = half // 2
    freq = ROPE_THETA ** (-jnp.arange(quarter, dtype=jnp.float32) / quarter)

    def rot(xa, pos):
        ang = pos[:, None] * freq[None, :]
        cos = jnp.cos(ang)[None, :, None, :]
        sin = jnp.sin(ang)[None, :, None, :]
        x1, x2 = xa[..., :quarter], xa[..., quarter:]
        return jnp.concatenate([x1 * cos - x2 * sin, x2 * cos + x1 * sin], axis=-1)

    xf = x.astype(jnp.float32)
    out = jnp.concatenate([rot(xf[..., :half], row), rot(xf[..., half:], col)], axis=-1)
    return out.astype(x.dtype)


def gqa_attend(q, k, v):
    s = jnp.einsum('bqhgd,bkhd->bhgqk', q, k).astype(jnp.float32) * ATT_SCALE
    p = jax.nn.softmax(s, axis=-1).astype(v.dtype)
    return jnp.einsum('bhgqk,bkhd->bqhgd', p, v)


def latent_attention(q, k, v, k_ctx, v_ctx):
    B, T = q.shape[:2]
    k_all = jnp.concatenate([k, k_ctx], axis=1)
    v_all = jnp.concatenate([v, v_ctx], axis=1)
    nb = T // Q_BLOCK
    qb = q.reshape(B, nb, Q_BLOCK, ATT_KV_HEADS, ATT_GROUP, ATT_HEAD_DIM).swapaxes(0, 1)
    ob = lax.map(lambda qblk: gqa_attend(qblk, k_all, v_all), qb)
    return ob.swapaxes(0, 1).reshape(B, T, ATT_HEADS * ATT_HEAD_DIM)


def context_attention(q, k, v):
    B, L = q.shape[:2]
    o = gqa_attend(q.reshape(B, L, ATT_KV_HEADS, ATT_GROUP, ATT_HEAD_DIM), k, v)
    return o.reshape(B, L, ATT_HEADS * ATT_HEAD_DIM)


def attn_qk(q, k, q_gain, k_gain):
    q = rms_norm(heads(q, ATT_HEADS, ATT_HEAD_DIM), q_gain)
    k = rms_norm(heads(k, ATT_KV_HEADS, ATT_HEAD_DIM), k_gain)
    return q, k


def mlstm_scan(q, k, v, log_i, log_f, state):
    B, H, T, _ = q.shape
    L = MLSTM_CHUNK
    nc = T // L

    def to_chunks(a):
        return jnp.moveaxis(a.reshape(a.shape[:2] + (nc, L) + a.shape[3:]), 2, 0)

    xs = tuple(to_chunks(a) for a in (q, k, v, log_i, log_f))
    lower = jnp.tril(jnp.ones((L, L), dtype=bool))

    def step(carry, inp):
        C, n, m = carry
        qc, kc, vc, ic, fc = inp
        b = jnp.cumsum(fc, axis=-1)
        log_d = jnp.where(lower, b[..., :, None] - b[..., None, :] + ic[..., None, :], -jnp.inf)
        log_inter = b + m[..., None]
        m_t = jnp.maximum(log_inter, jnp.max(log_d, axis=-1))
        d = jnp.exp(log_d - m_t[..., None])
        inter = jnp.exp(log_inter - m_t)
        s = jnp.einsum('bhtd,bhsd->bhts', qc, kc) * d
        num = inter[..., None] * jnp.einsum('bhtd,bhde->bhte', qc, C) + jnp.einsum('bhts,bhse->bhte', s, vc)
        den = inter * jnp.einsum('bhtd,bhd->bht', qc, n) + jnp.sum(s, axis=-1)
        h = num / jnp.maximum(jnp.abs(den), jnp.exp(-m_t))[..., None]
        b_last = b[..., -1]
        log_w = b_last[..., None] - b + ic
        m_new = jnp.maximum(b_last + m, jnp.max(log_w, axis=-1))
        w = jnp.exp(log_w - m_new[..., None])
        decay = jnp.exp(b_last + m - m_new)
        C_new = decay[..., None, None] * C + jnp.einsum('bhs,bhsd,bhse->bhde', w, kc, vc)
        n_new = decay[..., None] * n + jnp.einsum('bhs,bhsd->bhd', w, kc)
        return (C_new, n_new, m_new), h

    state, hs = lax.scan(step, state, xs)
    h = jnp.moveaxis(hs, 0, 2).reshape(B, H, T, -1)
    return h, state


def mlstm_inputs(q, k, v, g, gate_bias):
    B, T = q.shape[:2]
    def hd(a):
        return heads(a, MLSTM_HEADS, MLSTM_HEAD_DIM).transpose(0, 2, 1, 3).astype(jnp.float32)
    qh, kh, vh = hd(q), hd(k) * (MLSTM_HEAD_DIM ** -0.5), hd(v)
    gates = (g + gate_bias.reshape(-1)).astype(jnp.float32).reshape(B, T, 4, MLSTM_HEADS).transpose(2, 0, 3, 1)
    return qh, kh, vh, gates


def flip_t(a):
    return jnp.flip(a, axis=2)


def mlstm_bidir(qh, kh, vh, gates, state_f, state_b):
    h_f, st_f = mlstm_scan(qh, kh, vh, gates[0], jax.nn.log_sigmoid(gates[1]), state_f)
    h_b, st_b = mlstm_scan(flip_t(qh), flip_t(kh), flip_t(vh), flip_t(gates[2]),
                           flip_t(jax.nn.log_sigmoid(gates[3])), state_b)
    return h_f + flip_t(h_b), st_f, st_b


def mlstm_empty_state(B):
    return (jnp.zeros((B, MLSTM_HEADS, MLSTM_HEAD_DIM, MLSTM_HEAD_DIM), jnp.float32),
            jnp.zeros((B, MLSTM_HEADS, MLSTM_HEAD_DIM), jnp.float32),
            jnp.full((B, MLSTM_HEADS), M_INIT, jnp.float32))


def mlstm_out(h, o, gain):
    B, _, T, _ = h.shape
    hn = rms_norm(h.transpose(0, 2, 1, 3), jnp.ones((MLSTM_HEAD_DIM,), jnp.float32)).reshape(B, T, BRANCH_W)
    return (hn * gain.astype(jnp.float32) * jax.nn.sigmoid(o.astype(jnp.float32))).astype(o.dtype)


def short_conv(u, bg, cg, w):
    a = jnp.pad(cg * u, ((0, 0), (1, 1), (0, 0)))
    y = w[0] * a[:, :-2] + w[1] * a[:, 1:-1] + w[2] * a[:, 2:]
    return bg * y


def multiscale_pool(u, pool_w, pool_scale):
    B, T, _ = u.shape
    uf = u.astype(jnp.float32)
    csum = jnp.concatenate([jnp.zeros((B, 1, BRANCH_W), jnp.float32), jnp.cumsum(uf, axis=1)], axis=1)
    t = jnp.arange(T)
    outs = []
    for g, w in enumerate(POOL_WINDOWS):
        lo = jnp.clip(t - w // 2, 0, T)
        hi = jnp.clip(t + w - w // 2, 0, T)
        sl = slice(g * POOL_GROUP, (g + 1) * POOL_GROUP)
        cs = csum[..., sl]
        mean = (cs[:, hi] - cs[:, lo]) / (hi - lo).astype(jnp.float32)[None, :, None]
        outs.append((mean - uf[..., sl]).astype(u.dtype) @ pool_w[g])
    return jnp.concatenate(outs, axis=-1) * pool_scale


def merge_branches(branches, gate_cols, w_branch, w_out):
    gates = jnp.split(jax.nn.sigmoid(gate_cols.astype(jnp.float32)).astype(gate_cols.dtype), N_BRANCH, axis=-1)
    acc = gates[0] * (branches[0] @ w_branch[0])
    for i in range(1, N_BRANCH):
        acc = acc + gates[i] * (branches[i] @ w_branch[i])
    return acc @ w_out


def hybrid_layer(x, xc, c, c_ctx, norm_gain, w_mod, b_mod, w_in, q_gain, k_gain, gate_bias, m_gain,
                 conv_w, pool_w, pool_scale, w_branch, w_out, update_ctx):
    B = x.shape[0]
    mod_l = jax.nn.silu(c) @ w_mod + b_mod
    mod_c = jax.nn.silu(c_ctx) @ w_mod + b_mod
    sh_l, sc_l, gt_l = jnp.split(mod_l[:, None, :], 3, axis=-1)
    sh_c, sc_c, gt_c = jnp.split(mod_c, 3, axis=-1)
    h_l = rms_norm(x, norm_gain) * (1 + sc_l) + sh_l
    h_c = rms_norm(xc, norm_gain) * (1 + sc_c) + sh_c
    (aq_l, ak_l, av_l, az_l, mq_l, mk_l, mv_l, mo_l, mz_l, mg_l,
     cu_l, cb_l, cc_l, cz_l, pu_l, pz_l, gm_l) = split_cols(h_l @ w_in)
    (aq_c, ak_c, av_c, az_c, mq_c, mk_c, mv_c, mo_c, mz_c, mg_c,
     cu_c, cb_c, cc_c, cz_c, pu_c, pz_c, gm_c) = split_cols(h_c @ w_in)

    q_l, k_l = attn_qk(aq_l, ak_l, q_gain, k_gain)
    q_l, k_l = axial_rope(q_l), axial_rope(k_l)
    q_c, k_c = attn_qk(aq_c, ak_c, q_gain, k_gain)
    v_l = heads(av_l, ATT_KV_HEADS, ATT_HEAD_DIM)
    v_c = heads(av_c, ATT_KV_HEADS, ATT_HEAD_DIM)
    ya_l = latent_attention(q_l, k_l, v_l, k_c, v_c)

    mq, mk, mv, mg = mlstm_inputs(mq_c, mk_c, mv_c, mg_c, gate_bias)
    hm_c, st_f, st_b = mlstm_bidir(mq, mk, mv, mg, mlstm_empty_state(B), mlstm_empty_state(B))
    mq, mk, mv, mg = mlstm_inputs(mq_l, mk_l, mv_l, mg_l, gate_bias)
    hm_l, _, _ = mlstm_bidir(mq, mk, mv, mg, st_f, st_b)
    yb_l = mlstm_out(hm_l, mo_l, m_gain)

    yc_l = short_conv(cu_l, cb_l, cc_l, conv_w)
    yd_l = multiscale_pool(pu_l, pool_w, pool_scale)

    out_l = merge_branches([ya_l * jax.nn.silu(az_l), yb_l * jax.nn.silu(mz_l),
                            yc_l * jax.nn.silu(cz_l), yd_l * jax.nn.silu(pz_l)], gm_l, w_branch, w_out)
    x = x + gt_l * out_l

    if update_ctx:
        ya_c = context_attention(q_c, k_c, v_c)
        yb_c = mlstm_out(hm_c, mo_c, m_gain)
        yc_c = short_conv(cu_c, cb_c, cc_c, conv_w)
        yd_c = multiscale_pool(pu_c, pool_w, pool_scale)
        out_c = merge_branches([ya_c * jax.nn.silu(az_c), yb_c * jax.nn.silu(mz_c),
                                yc_c * jax.nn.silu(cz_c), yd_c * jax.nn.silu(pz_c)], gm_c, w_branch, w_out)
        xc = xc + gt_c * out_c
    return x, xc


def setup_inputs(seed: int = 0) -> dict:
    key = jax.random.key(seed)
    ks = jax.random.split(key, 20)
    f32 = jnp.float32

    def nrm(k, shape, scale):
        return jax.random.normal(k, shape, f32) * scale

    fbias = jnp.array([0.0, 1.0, 0.0, 1.0], f32)[:, None] * jnp.linspace(3.0, 6.0, MLSTM_HEADS, dtype=f32)[None, :]
    return {
        "x": nrm(ks[0], (BATCH, SEQ, D_MODEL), 1.0),
        "c": nrm(ks[1], (BATCH, D_MODEL), 1.0),
        "ctx": nrm(ks[2], (BATCH, CTX_LEN, D_MODEL), 1.0),
        "c_ctx": nrm(ks[3], (D_MODEL,), 1.0),
        "norm_gain": 1.0 + nrm(ks[4], (DEPTH, D_MODEL), 0.05),
        "w_mod": nrm(ks[5], (DEPTH, D_MODEL, 3 * D_MODEL), D_MODEL ** -0.5),
        "b_mod": nrm(ks[6], (DEPTH, 3 * D_MODEL), 0.02),
        "w_in": nrm(ks[7], (DEPTH, D_MODEL, N_IN), D_MODEL ** -0.5),
        "q_norm_gain": 1.0 + nrm(ks[8], (DEPTH, ATT_HEAD_DIM), 0.05),
        "k_norm_gain": 1.0 + nrm(ks[9], (DEPTH, ATT_HEAD_DIM), 0.05),
        "mlstm_gate_bias": fbias[None] + nrm(ks[10], (DEPTH, 4, MLSTM_HEADS), 0.1),
        "mlstm_norm_gain": 1.0 + nrm(ks[11], (DEPTH, BRANCH_W), 0.05),
        "conv_w": nrm(ks[12], (DEPTH, CONV_WIDTH, BRANCH_W), CONV_WIDTH ** -0.5),
        "pool_w": nrm(ks[13], (DEPTH, len(POOL_WINDOWS), POOL_GROUP, POOL_GROUP), POOL_GROUP ** -0.5),
        "pool_scale": 1.0 + nrm(ks[14], (DEPTH, BRANCH_W), 0.1),
        "w_branch": nrm(ks[15], (DEPTH, N_BRANCH, BRANCH_W, D_MODEL), BRANCH_W ** -0.5),
        "w_out": nrm(ks[16], (DEPTH, D_MODEL, D_MODEL), D_MODEL ** -0.5),
        "final_norm_gain": 1.0 + nrm(ks[17], (D_MODEL,), 0.05),
    }


def reference(x, c, ctx, c_ctx, norm_gain, w_mod, b_mod, w_in, q_norm_gain, k_norm_gain, mlstm_gate_bias,
              mlstm_norm_gain, conv_w, pool_w, pool_scale, w_branch, w_out, final_norm_gain):
    for l in range(DEPTH):
        x, ctx = hybrid_layer(x, ctx, c, c_ctx, norm_gain[l], w_mod[l], b_mod[l], w_in[l], q_norm_gain[l],
                              k_norm_gain[l], mlstm_gate_bias[l], mlstm_norm_gain[l], conv_w[l], pool_w[l],
                              pool_scale[l], w_branch[l], w_out[l], update_ctx=(l < DEPTH - 1))
    return rms_norm(x, final_norm_gain)
```

```python
import functools

import numpy as np
import jax
import jax.numpy as jnp
from jax import lax
from jax.experimental import pallas as pl
from jax.experimental.pallas import tpu as pltpu

F32 = jnp.float32
BF16 = jnp.bfloat16

D_MODEL = 2048
BRANCH_W = 1024
GRID_W = 64
CTX_LEN = 256
EPS = 1e-6
ATT_HEAD_DIM = 128
ATT_GROUP = 4
ATT_KV_HEADS = 2
ATT_SCALE = ATT_HEAD_DIM ** -0.5
ROPE_THETA = 10000.0
MLSTM_HEADS = 4
MLSTM_HEAD_DIM = 256
MLSTM_K_SCALE = MLSTM_HEAD_DIM ** -0.5
M_INIT = -1e30
POOL_WINDOWS = (2, 4, 8, 16)
POOL_GROUP = 256
N_BRANCH = 4

ROW_TILE = 256
HALO = 16
ATT_Q_TILE = 128
V7X_VMEM_LIMIT = 56 * 1024 * 1024

COL_GATE = 0
COL_AQ, COL_AZ = 8, 9
COL_MQ, COL_MK, COL_MV, COL_MO, COL_MZ = 10, 11, 12, 13, 14
COL_CU, COL_CB, COL_CC, COL_CZ = 15, 16, 17, 18
COL_PU, COL_PZ = 19, 20
N_MAIN = 21 * 1024
N_TAIL = 640


def _cparams(sem, vmem=V7X_VMEM_LIMIT):
    return pltpu.CompilerParams(dimension_semantics=sem, vmem_limit_bytes=vmem)


def _silu(x):
    return x * jax.nn.sigmoid(x)


def _mod_kernel(c_ref, w_ref, b_ref, o_ref):
    a = _silu(c_ref[...]).astype(BF16)
    o_ref[...] = jnp.dot(a, w_ref[...].astype(BF16), preferred_element_type=F32) + b_ref[...]


def _modulation(cc, w_mod, b_mod):
    depth, d, n = w_mod.shape
    tn = 768
    return pl.pallas_call(
        _mod_kernel,
        out_shape=jax.ShapeDtypeStruct((depth, 8, n), F32),
        grid=(depth, n // tn),
        in_specs=[pl.BlockSpec((8, d), lambda l, j: (0, 0)),
                  pl.BlockSpec((None, d, tn), lambda l, j: (l, 0, j)),
                  pl.BlockSpec((None, 1, tn), lambda l, j: (l, 0, j))],
        out_specs=pl.BlockSpec((None, 8, tn), lambda l, j: (l, 0, j)),
        compiler_params=_cparams(("parallel", "parallel")),
        name="modulation",
    )(cc, w_mod, b_mod.reshape(depth, 1, n))


def _normmod_kernel(x_ref, g_ref, sc_ref, sh_ref, o_ref):
    x = x_ref[...]
    y = x * lax.rsqrt(jnp.mean(x * x, axis=-1, keepdims=True) + EPS) * g_ref[...]
    o_ref[...] = (y * (1.0 + sc_ref[0]) + sh_ref[0]).astype(BF16)


def _normmod(xs, gain, sc_t, sh_t):
    m, d = xs.shape
    nt = m // ROW_TILE
    return pl.pallas_call(
        _normmod_kernel,
        out_shape=jax.ShapeDtypeStruct((m, d), BF16),
        grid=(nt,),
        in_specs=[pl.BlockSpec((ROW_TILE, d), lambda i: (i, 0)),
                  pl.BlockSpec((1, d), lambda i: (0, 0)),
                  pl.BlockSpec((1, 1, d), lambda i: (i, 0, 0)),
                  pl.BlockSpec((1, 1, d), lambda i: (i, 0, 0))],
        out_specs=pl.BlockSpec((ROW_TILE, d), lambda i: (i, 0)),
        compiler_params=_cparams(("parallel",)),
        name="normmod",
    )(xs, gain.reshape(1, d), sc_t, sh_t)


def _matmul_kernel(a_ref, w_ref, o_ref):
    o_ref[...] = jnp.dot(a_ref[...], w_ref[...], preferred_element_type=F32).astype(o_ref.dtype)


def _pick_tile(n, candidates):
    for t in candidates:
        if n % t == 0:
            return t
    raise ValueError(f"no tile for {n}")


def _matmul(a, w, out_dtype, tn):
    m, k = a.shape
    n = w.shape[1]
    tm = _pick_tile(m, (1024, 512, 256))
    return pl.pallas_call(
        _matmul_kernel,
        out_shape=jax.ShapeDtypeStruct((m, n), out_dtype),
        grid=(m // tm, n // tn),
        in_specs=[pl.BlockSpec((tm, k), lambda i, j: (i, 0)),
                  pl.BlockSpec((k, tn), lambda i, j: (0, j))],
        out_specs=pl.BlockSpec((tm, tn), lambda i, j: (i, j)),
        compiler_params=_cparams(("parallel", "parallel")),
        name="inproj",
    )(a, w)


def _swap32(x):
    lane = lax.broadcasted_iota(jnp.int32, x.shape, 1)
    return jnp.where((lane % 64) < 32, pltpu.roll(x, 96, 1), pltpu.roll(x, 32, 1))


def _norm_rope(x, gain, cos, sin, is_lat):
    x = x * lax.rsqrt(jnp.mean(x * x, axis=-1, keepdims=True) + EPS) * gain
    xr = x * cos + _swap32(x) * sin
    return jnp.where(is_lat, xr, x)


def _kvprep_kernel(t_ref, cos_ref, sin_ref, kg_ref, k_out, v_out, *, tiles_per_batch):
    is_lat = (pl.program_id(0) % tiles_per_batch) < tiles_per_batch - 1
    cos, sin, kg = cos_ref[...], sin_ref[...], kg_ref[...]
    for h in range(ATT_KV_HEADS):
        sl = slice(h * ATT_HEAD_DIM, (h + 1) * ATT_HEAD_DIM)
        k_out[:, sl] = _norm_rope(t_ref[:, sl], kg, cos, sin, is_lat).astype(BF16)
    kvw = ATT_KV_HEADS * ATT_HEAD_DIM
    v_out[...] = t_ref[:, kvw:2 * kvw].astype(BF16)


def _kvprep(tail, cos_t, sin_t, k_gain, tiles_per_batch):
    m = tail.shape[0]
    nt = m // ROW_TILE
    lat_tiles = tiles_per_batch - 1
    kvw = ATT_KV_HEADS * ATT_HEAD_DIM
    rope_spec = pl.BlockSpec((ROW_TILE, ATT_HEAD_DIM),
                             lambda i: (jnp.minimum(i % tiles_per_batch, lat_tiles - 1), 0))
    return pl.pallas_call(
        functools.partial(_kvprep_kernel, tiles_per_batch=tiles_per_batch),
        out_shape=(jax.ShapeDtypeStruct((m, kvw), BF16), jax.ShapeDtypeStruct((m, kvw), BF16)),
        grid=(nt,),
        in_specs=[pl.BlockSpec((ROW_TILE, 2 * kvw), lambda i: (i, 0)),
                  rope_spec, rope_spec,
                  pl.BlockSpec((1, ATT_HEAD_DIM), lambda i: (0, 0))],
        out_specs=(pl.BlockSpec((ROW_TILE, kvw), lambda i: (i, 0)),
                   pl.BlockSpec((ROW_TILE, kvw), lambda i: (i, 0))),
        compiler_params=_cparams(("parallel",)),
        name="kvprep",
    )(tail, cos_t, sin_t, k_gain.reshape(1, ATT_HEAD_DIM))


def _attn_kernel(q_ref, z_ref, k_ref, v_ref, cos_ref, sin_ref, qg_ref, o_ref, *, n_lat_q, t_lat):
    is_lat = pl.program_id(2) < n_lat_q
    cos, sin = cos_ref[...], sin_ref[...]
    qg = qg_ref[...] * ATT_SCALE
    qs = []
    for g in range(ATT_GROUP):
        sl = slice(g * ATT_HEAD_DIM, (g + 1) * ATT_HEAD_DIM)
        qs.append(_norm_rope(q_ref[:, sl].astype(F32), qg, cos, sin, is_lat).astype(BF16))
    q = jnp.concatenate(qs, axis=0)

    def attend(k, v):
        s = lax.dot_general(q, k, (((1,), (1,)), ((), ())), preferred_element_type=F32)
        p = jnp.exp(s - jnp.max(s, axis=-1, keepdims=True))
        l = jnp.sum(p, axis=-1, keepdims=True)
        o = jnp.dot(p.astype(BF16), v, preferred_element_type=F32) / l
        for g in range(ATT_GROUP):
            sl = slice(g * ATT_HEAD_DIM, (g + 1) * ATT_HEAD_DIM)
            og = o[g * ATT_Q_TILE:(g + 1) * ATT_Q_TILE]
            o_ref[:, sl] = (og * _silu(z_ref[:, sl].astype(F32))).astype(BF16)

    @pl.when(is_lat)
    def _():
        attend(k_ref[...], v_ref[...])

    @pl.when(jnp.logical_not(is_lat))
    def _():
        attend(k_ref[t_lat:, :], v_ref[t_lat:, :])


def _attention(ymain, kb, vb, cos_t, sin_t, q_gain, batch, s_len):
    m = ymain.shape[0]
    t_lat = s_len - CTX_LEN
    nq = s_len // ATT_Q_TILE
    n_lat_q = t_lat // ATT_Q_TILE
    gw = ATT_GROUP * ATT_HEAD_DIM
    rope_spec = pl.BlockSpec((ATT_Q_TILE, ATT_HEAD_DIM),
                             lambda b, h, i: (jnp.minimum(i, n_lat_q - 1), 0))
    return pl.pallas_call(
        functools.partial(_attn_kernel, n_lat_q=n_lat_q, t_lat=t_lat),
        out_shape=jax.ShapeDtypeStruct((m, BRANCH_W), BF16),
        grid=(batch, ATT_KV_HEADS, nq),
        in_specs=[pl.BlockSpec((ATT_Q_TILE, gw), lambda b, h, i: (b * nq + i, COL_AQ * 2 + h)),
                  pl.BlockSpec((ATT_Q_TILE, gw), lambda b, h, i: (b * nq + i, COL_AZ * 2 + h)),
                  pl.BlockSpec((s_len, ATT_HEAD_DIM), lambda b, h, i: (b, h)),
                  pl.BlockSpec((s_len, ATT_HEAD_DIM), lambda b, h, i: (b, h)),
                  rope_spec, rope_spec,
                  pl.BlockSpec((1, ATT_HEAD_DIM), lambda b, h, i: (0, 0))],
        out_specs=pl.BlockSpec((ATT_Q_TILE, gw), lambda b, h, i: (b * nq + i, h)),
        compiler_params=_cparams(("parallel", "parallel", "arbitrary")),
        name="attention",
    )(ymain, ymain, kb, vb, cos_t, sin_t, q_gain.reshape(1, ATT_HEAD_DIM))


def _mlstm_kernel(*refs, reverse, final):
    if final:
        (q_ref, k_ref, v_ref, g_ref, bias_ref, o_ref, z_ref, gain_ref, hprev_ref,
         out_ref, c_s, n_s, m_s) = refs
    else:
        q_ref, k_ref, v_ref, g_ref, bias_ref, out_ref, c_s, n_s, m_s = refs
    L = ROW_TILE
    hd = MLSTM_HEAD_DIM

    @pl.when(pl.program_id(1) == 0)
    def _():
        c_s[...] = jnp.zeros(c_s.shape, F32)
        n_s[...] = jnp.zeros(n_s.shape, F32)
        m_s[...] = jnp.full(m_s.shape, M_INIT, F32)

    gates = g_ref[...] + bias_ref[...]
    log_f = jnp.minimum(gates, 0.0) - jnp.log1p(jnp.exp(-jnp.abs(gates)))
    row = lax.broadcasted_iota(jnp.int32, (L, L), 0)
    col = lax.broadcasted_iota(jnp.int32, (L, L), 1)
    tri = (col >= row) if reverse else (col <= row)
    bcum = jnp.dot(tri.astype(F32), log_f, precision=lax.Precision.HIGHEST, preferred_element_type=F32)
    bcum_t = bcum.T
    gates_t = gates.T
    last = 0 if reverse else L - 1
    for h in range(MLSTM_HEADS):
        ic = (2 if reverse else 0) * MLSTM_HEADS + h
        fc = (3 if reverse else 1) * MLSTM_HEADS + h
        hs = slice(h * hd, (h + 1) * hd)
        b_col, b_row = bcum[:, fc:fc + 1], bcum_t[fc:fc + 1, :]
        i_col, i_row = gates[:, ic:ic + 1], gates_t[ic:ic + 1, :]
        m_prev = m_s[h, 0:1, 0:1]
        log_d = jnp.where(tri, b_col - b_row + i_row, -jnp.inf)
        log_inter = b_col + m_prev
        m_t = jnp.maximum(log_inter, jnp.max(log_d, axis=1, keepdims=True))
        dmat = jnp.exp(log_d - m_t)
        inter = jnp.exp(log_inter - m_t)
        q = q_ref[:, hs]
        k = k_ref[:, hs].astype(F32) * MLSTM_K_SCALE
        v = v_ref[:, hs]
        qk = lax.dot_general(q, k.astype(BF16), (((1,), (1,)), ((), ())), preferred_element_type=F32)
        sm = qk * dmat
        c_mat = c_s[h]
        n_vec = n_s[h, 0:1, :]
        num = inter * jnp.dot(q, c_mat.astype(BF16), preferred_element_type=F32) \
            + jnp.dot(sm.astype(BF16), v, preferred_element_type=F32)
        den = inter * jnp.sum(q.astype(F32) * n_vec, axis=1, keepdims=True) + jnp.sum(sm, axis=1, keepdims=True)
        hh = num / jnp.maximum(jnp.abs(den), jnp.exp(-m_t))

        b_last = bcum[last:last + 1, fc:fc + 1]
        log_w = b_last - b_col + i_col
        m_new = jnp.maximum(b_last + m_prev, jnp.max(log_w, axis=0, keepdims=True))
        w = jnp.exp(log_w - m_new)
        decay = jnp.exp(b_last + m_prev - m_new)
        kw = k * w
        c_s[h] = decay * c_mat + lax.dot_general(kw.astype(BF16), v, (((0,), (0,)), ((), ())),
                                                 preferred_element_type=F32)
        n_s[h, 0:1, :] = decay * n_vec + jnp.sum(kw, axis=0, keepdims=True)
        m_s[h] = jnp.broadcast_to(m_new, m_s.shape[1:])

        if final:
            ht = hh + hprev_ref[:, hs]
            hn = ht * lax.rsqrt(jnp.mean(ht * ht, axis=-1, keepdims=True) + EPS)
            y = hn * gain_ref[:, hs] * jax.nn.sigmoid(o_ref[:, hs].astype(F32)) * _silu(z_ref[:, hs].astype(F32))
            out_ref[:, hs] = y.astype(BF16)
        else:
            out_ref[:, hs] = hh


def _mlstm_pass(ymain, tail, bias, gain, hprev, batch, tiles_per_batch, reverse):
    m = ymain.shape[0]
    lat = tiles_per_batch - 1
    final = hprev is not None

    def rows(b, c):
        r = jnp.where(c == 0, lat, (lat - c) if reverse else (c - 1))
        return b * tiles_per_batch + r

    def col(cidx):
        return pl.BlockSpec((ROW_TILE, BRANCH_W), lambda b, c: (rows(b, c), cidx))

    gate_spec = pl.BlockSpec((ROW_TILE, 128), lambda b, c: (rows(b, c), (N_TAIL - 128) // 128))
    in_specs = [col(COL_MQ), col(COL_MK), col(COL_MV), gate_spec, pl.BlockSpec((1, 128), lambda b, c: (0, 0))]
    args = [ymain, ymain, ymain, tail, bias]
    if final:
        in_specs += [col(COL_MO), col(COL_MZ), pl.BlockSpec((1, BRANCH_W), lambda b, c: (0, 0)),
                     pl.BlockSpec((ROW_TILE, BRANCH_W), lambda b, c: (rows(b, c), 0))]
        args += [ymain, ymain, gain.reshape(1, BRANCH_W), hprev]
    return pl.pallas_call(
        functools.partial(_mlstm_kernel, reverse=reverse, final=final),
        out_shape=jax.ShapeDtypeStruct((m, BRANCH_W), BF16 if final else F32),
        grid=(batch, tiles_per_batch),
        in_specs=in_specs,
        out_specs=pl.BlockSpec((ROW_TILE, BRANCH_W), lambda b, c: (rows(b, c), 0)),
        scratch_shapes=[pltpu.VMEM((MLSTM_HEADS, MLSTM_HEAD_DIM, MLSTM_HEAD_DIM), F32),
                        pltpu.VMEM((MLSTM_HEADS, 8, MLSTM_HEAD_DIM), F32),
                        pltpu.VMEM((MLSTM_HEADS, 8, 128), F32)],
        compiler_params=_cparams(("parallel", "arbitrary")),
        name="mlstm_bwd" if reverse else "mlstm_fwd",
    )(*args)


def _local_kernel(cu_ref, cb_ref, cc_ref, cz_ref, pu_ref, pz_ref,
                  cu_p, cc_p, pu_p, cu_n, cc_n, pu_n,
                  cw_ref, pw_ref, ps_ref, yc_ref, yd_ref, *, tiles_per_batch):
    r = pl.program_id(0) % tiles_per_batch
    lat = tiles_per_batch - 1
    has_prev = jnp.logical_and(r != 0, r != lat)
    has_next = jnp.logical_and(r != lat - 1, r != lat)
    seg_len = jnp.where(r == lat, CTX_LEN, lat * ROW_TILE)
    t0 = jnp.where(r == lat, 0, r * ROW_TILE)
    rowi = lax.broadcasted_iota(jnp.int32, (ROW_TILE, 1), 0)

    a = cc_ref[...].astype(F32) * cu_ref[...].astype(F32)
    a_prev = jnp.where(has_prev, cc_p[HALO - 1:HALO, :].astype(F32) * cu_p[HALO - 1:HALO, :].astype(F32), 0.0)
    a_next = jnp.where(has_next, cc_n[0:1, :].astype(F32) * cu_n[0:1, :].astype(F32), 0.0)
    a_m1 = jnp.where(rowi == 0, a_prev, pltpu.roll(a, 1, 0))
    a_p1 = jnp.where(rowi == ROW_TILE - 1, a_next, pltpu.roll(a, ROW_TILE - 1, 0))
    y = cw_ref[0:1, :] * a_m1 + cw_ref[1:2, :] * a + cw_ref[2:3, :] * a_p1
    yc_ref[...] = (cb_ref[...].astype(F32) * y * _silu(cz_ref[...].astype(F32))).astype(BF16)

    u = pu_ref[...].astype(F32)
    ext = jnp.concatenate([jnp.where(has_prev, pu_p[...].astype(F32), 0.0), u,
                           jnp.where(has_next, pu_n[...].astype(F32), 0.0)], axis=0)
    n_ext = ROW_TILE + 2 * HALO
    t = t0 + rowi
    for g, w in enumerate(POOL_WINDOWS):
        gs = slice(g * POOL_GROUP, (g + 1) * POOL_GROUP)
        xg = ext[:, gs]
        acc = None
        for dlt in range(-(w // 2), w - w // 2):
            sh = xg if dlt == 0 else pltpu.roll(xg, (-dlt) % n_ext, 0)
            part = sh[HALO:HALO + ROW_TILE]
            acc = part if acc is None else acc + part
        cnt = (jnp.minimum(t + (w - w // 2), seg_len) - jnp.maximum(t - w // 2, 0)).astype(F32)
        dev = acc / cnt - u[:, gs]
        pg = jnp.dot(dev.astype(BF16), pw_ref[g], preferred_element_type=F32) * ps_ref[:, gs]
        yd_ref[:, gs] = (pg * _silu(pz_ref[:, gs].astype(F32))).astype(BF16)


def _local_mixers(ymain, conv_w, pool_w, pool_scale, tiles_per_batch):
    m = ymain.shape[0]
    nt = m // ROW_TILE
    hb = ROW_TILE // HALO
    n_halo = m // HALO

    def col(cidx):
        return pl.BlockSpec((ROW_TILE, BRANCH_W), lambda i: (i, cidx))

    def prev(cidx):
        return pl.BlockSpec((HALO, BRANCH_W), lambda i: (jnp.maximum(i * hb - 1, 0), cidx))

    def nxt(cidx):
        return pl.BlockSpec((HALO, BRANCH_W), lambda i: (jnp.minimum((i + 1) * hb, n_halo - 1), cidx))

    out_sd = jax.ShapeDtypeStruct((m, BRANCH_W), BF16)
    return pl.pallas_call(
        functools.partial(_local_kernel, tiles_per_batch=tiles_per_batch),
        out_shape=(out_sd, out_sd),
        grid=(nt,),
        in_specs=[col(COL_CU), col(COL_CB), col(COL_CC), col(COL_CZ), col(COL_PU), col(COL_PZ),
                  prev(COL_CU), prev(COL_CC), prev(COL_PU), nxt(COL_CU), nxt(COL_CC), nxt(COL_PU),
                  pl.BlockSpec((3, BRANCH_W), lambda i: (0, 0)),
                  pl.BlockSpec((len(POOL_WINDOWS), POOL_GROUP, POOL_GROUP), lambda i: (0, 0, 0)),
                  pl.BlockSpec((1, BRANCH_W), lambda i: (0, 0))],
        out_specs=(pl.BlockSpec((ROW_TILE, BRANCH_W), lambda i: (i, 0)),
                   pl.BlockSpec((ROW_TILE, BRANCH_W), lambda i: (i, 0))),
        compiler_params=_cparams(("parallel",)),
        name="local_mixers",
    )(*([ymain] * 12), conv_w, pool_w.astype(BF16), pool_scale.reshape(1, BRANCH_W))


def _merge_kernel(b0, b1, b2, b3, g_ref, w_ref, o_ref, acc_ref):
    i = pl.program_id(1)
    gate = jax.nn.sigmoid(g_ref[...].astype(F32))
    for j, br in enumerate((b0, b1, b2, b3)):
        @pl.when(i == j)
        def _(br=br, j=j):
            term = gate * jnp.dot(br[...], w_ref[...], preferred_element_type=F32)
            if j == 0:
                acc_ref[...] = term
            elif j < N_BRANCH - 1:
                acc_ref[...] += term
            else:
                o_ref[...] = (acc_ref[...] + term).astype(BF16)


def _merge(branches, ymain, w_branch):
    m = ymain.shape[0]
    tm = _pick_tile(m, (512, 256))
    br_spec = pl.BlockSpec((tm, BRANCH_W), lambda r, i: (r, 0))
    return pl.pallas_call(
        _merge_kernel,
        out_shape=jax.ShapeDtypeStruct((m, D_MODEL), BF16),
        grid=(m // tm, N_BRANCH),
        in_specs=[br_spec, br_spec, br_spec, br_spec,
                  pl.BlockSpec((tm, D_MODEL), lambda r, i: (r, i)),
                  pl.BlockSpec((None, BRANCH_W, D_MODEL), lambda r, i: (i, 0, 0))],
        out_specs=pl.BlockSpec((tm, D_MODEL), lambda r, i: (r, 0)),
        scratch_shapes=[pltpu.VMEM((tm, D_MODEL), F32)],
        compiler_params=_cparams(("parallel", "arbitrary")),
        name="merge",
    )(*branches, ymain, w_branch)


def _outproj_kernel(*refs, final):
    if final:
        a_ref, w_ref, x_ref, gt_ref, fg_ref, o_ref = refs
    else:
        a_ref, w_ref, x_ref, gt_ref, o_ref = refs
    y = x_ref[...] + gt_ref[0] * jnp.dot(a_ref[...], w_ref[...], preferred_element_type=F32)
    if final:
        y = y * lax.rsqrt(jnp.mean(y * y, axis=-1, keepdims=True) + EPS) * fg_ref[...]
    o_ref[...] = y


def _outproj(acc, w_out, xs, gt_t, final_gain, batch, tiles_per_batch):
    m, d = xs.shape
    final = final_gain is not None
    lat = tiles_per_batch - 1
    n_r = lat if final else tiles_per_batch

    def src(b, r):
        return b * tiles_per_batch + r

    in_specs = [pl.BlockSpec((ROW_TILE, d), lambda b, r: (src(b, r), 0)),
                pl.BlockSpec((d, d), lambda b, r: (0, 0)),
                pl.BlockSpec((ROW_TILE, d), lambda b, r: (src(b, r), 0)),
                pl.BlockSpec((1, 1, d), lambda b, r: (src(b, r), 0, 0))]
    args = [acc, w_out, xs, gt_t]
    if final:
        in_specs.append(pl.BlockSpec((1, d), lambda b, r: (0, 0)))
        args.append(final_gain.reshape(1, d))
    return pl.pallas_call(
        functools.partial(_outproj_kernel, final=final),
        out_shape=jax.ShapeDtypeStruct((batch * n_r * ROW_TILE, d), F32),
        grid=(batch, n_r),
        in_specs=in_specs,
        out_specs=pl.BlockSpec((ROW_TILE, d), lambda b, r: (b * n_r + r, 0)),
        compiler_params=_cparams(("parallel", "parallel")),
        name="outproj_final" if final else "outproj",
    )(*args)


def _rope_tables(t_lat):
    pos = np.arange(t_lat)
    quarter = ATT_HEAD_DIM // 4
    freq = ROPE_THETA ** (-jnp.arange(quarter, dtype=F32) / quarter)
    a_row = jnp.asarray(pos // GRID_W, F32)[:, None] * freq[None, :]
    a_col = jnp.asarray(pos % GRID_W, F32)[:, None] * freq[None, :]
    cos_t = jnp.concatenate([jnp.cos(a_row), jnp.cos(a_row), jnp.cos(a_col), jnp.cos(a_col)], axis=-1)
    sin_t = jnp.concatenate([-jnp.sin(a_row), jnp.sin(a_row), -jnp.sin(a_col), jnp.sin(a_col)], axis=-1)
    return cos_t, sin_t


def _reorder_w_in(w):
    main = jnp.concatenate([w[:, 13840:22032], w[:, 0:1024], w[:, 1536:2560], w[:, 2560:7680],
                            w[:, 7696:13840]], axis=1).astype(BF16)
    tail = jnp.concatenate([w[:, 1024:1536], w[:, 7680:7696],
                            jnp.zeros((w.shape[0], 112), w.dtype)], axis=1).astype(BF16)
    return main, tail


def kernel(x, c, ctx, c_ctx, norm_gain, w_mod, b_mod, w_in, q_norm_gain, k_norm_gain, mlstm_gate_bias,
           mlstm_norm_gain, conv_w, pool_w, pool_scale, w_branch, w_out, final_norm_gain):
    batch, t_lat, d = x.shape
    depth = w_in.shape[0]
    assert d == D_MODEL and ctx.shape[1] == CTX_LEN and t_lat % ROW_TILE == 0 and batch < 8
    s_len = t_lat + CTX_LEN
    tiles_per_batch = s_len // ROW_TILE
    m = batch * s_len

    xs = jnp.concatenate([x, ctx], axis=1).reshape(m, d)
    cc = jnp.zeros((8, d), F32).at[:batch].set(c).at[batch].set(c_ctx)
    mod = _modulation(cc, w_mod, b_mod)
    tile_row = np.array([b if r < tiles_per_batch - 1 else batch
                         for b in range(batch) for r in range(tiles_per_batch)], np.int32)
    cos_t, sin_t = _rope_tables(t_lat)

    out = None
    for l in range(depth):
        mod_t = mod[l][tile_row][:, None, :]
        sh_t, sc_t, gt_t = mod_t[..., :d], mod_t[..., d:2 * d], mod_t[..., 2 * d:]
        w_main, w_tail = _reorder_w_in(w_in[l])
        bias = jnp.zeros((1, 128), F32).at[0, :4 * MLSTM_HEADS].set(mlstm_gate_bias[l].reshape(-1))

        h = _normmod(xs, norm_gain[l], sc_t, sh_t)
        ymain = _matmul(h, w_main, BF16, 1024)
        tail = _matmul(h, w_tail, F32, N_TAIL)

        kb, vb = _kvprep(tail, cos_t, sin_t, k_norm_gain[l], tiles_per_batch)
        ya = _attention(ymain, kb, vb, cos_t, sin_t, q_norm_gain[l], batch, s_len)
        h_f = _mlstm_pass(ymain, tail, bias, None, None, batch, tiles_per_batch, reverse=False)
        yb = _mlstm_pass(ymain, tail, bias, mlstm_norm_gain[l], h_f, batch, tiles_per_batch, reverse=True)
        yc, yd = _local_mixers(ymain, conv_w[l], pool_w[l], pool_scale[l], tiles_per_batch)

        acc = _merge((ya, yb, yc, yd), ymain, w_branch[l].astype(BF16))
        last = l == depth - 1
        res = _outproj(acc, w_out[l].astype(BF16), xs, gt_t, final_norm_gain if last else None,
                       batch, tiles_per_batch)
        if last:
            out = res.reshape(batch, t_lat, d)
        else:
            xs = res
    return out
```

```python
import functools

import numpy as np
import jax
import jax.numpy as jnp
from jax import lax
from jax.experimental import pallas as pl
from jax.experimental.pallas import tpu as pltpu

F32 = jnp.float32
BF16 = jnp.bfloat16

D_MODEL = 2048
BRANCH_W = 1024
GRID_W = 64
CTX_LEN = 256
EPS = 1e-6
ATT_HEAD_DIM = 128
ATT_GROUP = 4
ATT_KV_HEADS = 2
ATT_SCALE = ATT_HEAD_DIM ** -0.5
LOG2_E = 1.4426950408889634
ROPE_THETA = 10000.0
MLSTM_HEADS = 4
MLSTM_HEAD_DIM = 256
MLSTM_K_SCALE = MLSTM_HEAD_DIM ** -0.5
M_INIT = -1e30
POOL_WINDOWS = (2, 4, 8, 16)
POOL_GROUP = 256
N_BRANCH = 4

ROW_TILE = 256
HALO = 16
ATT_Q_TILE = 128
ATT_KEY_CHUNK = 1024
VT_ROWS = ATT_HEAD_DIM + 16
V7X_VMEM_LIMIT = 56 * 1024 * 1024

COL_GATE = 0
COL_AQ, COL_AZ = 8, 9
COL_MQ, COL_MK, COL_MV, COL_MO, COL_MZ = 10, 11, 12, 13, 14
COL_CU, COL_CB, COL_CC, COL_CZ = 15, 16, 17, 18
COL_PU, COL_PZ = 19, 20
N_MAIN = 21 * 1024
N_TAIL = 640


def _cparams(sem, vmem=V7X_VMEM_LIMIT):
    return pltpu.CompilerParams(dimension_semantics=sem, vmem_limit_bytes=vmem)


def _silu(x):
    return x * jax.nn.sigmoid(x)


def _mod_kernel(c_ref, w_ref, b_ref, o_ref):
    a = _silu(c_ref[...]).astype(BF16)
    o_ref[...] = jnp.dot(a, w_ref[...].astype(BF16), preferred_element_type=F32) + b_ref[...]


def _modulation(cc, w_mod, b_mod):
    depth, d, n = w_mod.shape
    tn = 768
    return pl.pallas_call(
        _mod_kernel,
        out_shape=jax.ShapeDtypeStruct((depth, 8, n), F32),
        grid=(depth, n // tn),
        in_specs=[pl.BlockSpec((8, d), lambda l, j: (0, 0)),
                  pl.BlockSpec((None, d, tn), lambda l, j: (l, 0, j)),
                  pl.BlockSpec((None, 1, tn), lambda l, j: (l, 0, j))],
        out_specs=pl.BlockSpec((None, 8, tn), lambda l, j: (l, 0, j)),
        compiler_params=_cparams(("parallel", "parallel")),
        name="modulation",
    )(cc, w_mod, b_mod.reshape(depth, 1, n))


def _normmod_kernel(x_ref, g_ref, sc_ref, sh_ref, o_ref):
    x = x_ref[...]
    y = x * lax.rsqrt(jnp.mean(x * x, axis=-1, keepdims=True) + EPS) * g_ref[...]
    o_ref[...] = (y * (1.0 + sc_ref[0]) + sh_ref[0]).astype(BF16)


def _normmod(xs, gain, sc_t, sh_t):
    m, d = xs.shape
    nt = m // ROW_TILE
    return pl.pallas_call(
        _normmod_kernel,
        out_shape=jax.ShapeDtypeStruct((m, d), BF16),
        grid=(nt,),
        in_specs=[pl.BlockSpec((ROW_TILE, d), lambda i: (i, 0)),
                  pl.BlockSpec((1, d), lambda i: (0, 0)),
                  pl.BlockSpec((1, 1, d), lambda i: (i, 0, 0)),
                  pl.BlockSpec((1, 1, d), lambda i: (i, 0, 0))],
        out_specs=pl.BlockSpec((ROW_TILE, d), lambda i: (i, 0)),
        compiler_params=_cparams(("parallel",)),
        name="normmod",
    )(xs, gain.reshape(1, d), sc_t, sh_t)


def _matmul_kernel(a_ref, w_ref, o_ref):
    o_ref[...] = jnp.dot(a_ref[...], w_ref[...], preferred_element_type=F32).astype(o_ref.dtype)


def _pick_tile(n, candidates):
    for t in candidates:
        if n % t == 0:
            return t
    raise ValueError(f"no tile for {n}")


def _matmul(a, w, out_dtype, tn):
    m, k = a.shape
    n = w.shape[1]
    tm = _pick_tile(m, (1024, 512, 256))
    return pl.pallas_call(
        _matmul_kernel,
        out_shape=jax.ShapeDtypeStruct((m, n), out_dtype),
        grid=(m // tm, n // tn),
        in_specs=[pl.BlockSpec((tm, k), lambda i, j: (i, 0)),
                  pl.BlockSpec((k, tn), lambda i, j: (0, j))],
        out_specs=pl.BlockSpec((tm, tn), lambda i, j: (i, j)),
        compiler_params=_cparams(("parallel", "parallel")),
        name="inproj",
    )(a, w)


def _swap32(x):
    lane = lax.broadcasted_iota(jnp.int32, x.shape, 1)
    return jnp.where((lane % 64) < 32, pltpu.roll(x, 96, 1), pltpu.roll(x, 32, 1))


def _norm_rope(x, gain, cos, sin, is_lat):
    x = x * lax.rsqrt(jnp.mean(x * x, axis=-1, keepdims=True) + EPS) * gain
    xr = x * cos + _swap32(x) * sin
    return jnp.where(is_lat, xr, x)


def _qkvprep_kernel(q_ref, t_ref, cos_ref, sin_ref, qg_ref, kg_ref, q_out, k_out, vt_out, *, tiles_per_batch):
    is_lat = pl.program_id(1) < tiles_per_batch - 1
    cos, sin, kg = cos_ref[...], sin_ref[...], kg_ref[...]
    qg = qg_ref[...] * (ATT_SCALE * LOG2_E)
    hd = ATT_HEAD_DIM
    for h in range(ATT_KV_HEADS * ATT_GROUP):
        sl = slice(h * hd, (h + 1) * hd)
        q_out[:, sl] = _norm_rope(q_ref[:, sl].astype(F32), qg, cos, sin, is_lat).astype(BF16)
    for h in range(ATT_KV_HEADS):
        sl = slice(h * hd, (h + 1) * hd)
        k_out[:, sl] = _norm_rope(t_ref[:, sl], kg, cos, sin, is_lat).astype(BF16)
        vsl = slice((ATT_KV_HEADS + h) * hd, (ATT_KV_HEADS + h + 1) * hd)
        vt_out[h, 0:hd, :] = t_ref[:, vsl].T.astype(BF16)
        vt_out[h, hd:, :] = jnp.ones((VT_ROWS - hd, ROW_TILE), BF16)


def _qkvprep(ymain, tail, cos_t, sin_t, q_gain, k_gain, batch, tiles_per_batch):
    m = tail.shape[0]
    lat_tiles = tiles_per_batch - 1
    kvw = ATT_KV_HEADS * ATT_HEAD_DIM
    rope_spec = pl.BlockSpec((ROW_TILE, ATT_HEAD_DIM), lambda b, r: (jnp.minimum(r, lat_tiles - 1), 0))
    gain_spec = pl.BlockSpec((1, ATT_HEAD_DIM), lambda b, r: (0, 0))
    return pl.pallas_call(
        functools.partial(_qkvprep_kernel, tiles_per_batch=tiles_per_batch),
        out_shape=(jax.ShapeDtypeStruct((m, BRANCH_W), BF16),
                   jax.ShapeDtypeStruct((m, kvw), BF16),
                   jax.ShapeDtypeStruct((batch, ATT_KV_HEADS, VT_ROWS, tiles_per_batch * ROW_TILE), BF16)),
        grid=(batch, tiles_per_batch),
        in_specs=[pl.BlockSpec((ROW_TILE, BRANCH_W), lambda b, r: (b * tiles_per_batch + r, COL_AQ)),
                  pl.BlockSpec((ROW_TILE, 2 * kvw), lambda b, r: (b * tiles_per_batch + r, 0)),
                  rope_spec, rope_spec, gain_spec, gain_spec],
        out_specs=(pl.BlockSpec((ROW_TILE, BRANCH_W), lambda b, r: (b * tiles_per_batch + r, 0)),
                   pl.BlockSpec((ROW_TILE, kvw), lambda b, r: (b * tiles_per_batch + r, 0)),
                   pl.BlockSpec((None, ATT_KV_HEADS, VT_ROWS, ROW_TILE), lambda b, r: (b, 0, 0, r))),
        compiler_params=_cparams(("parallel", "parallel")),
        name="qkvprep",
    )(ymain, tail, cos_t, sin_t, q_gain.reshape(1, ATT_HEAD_DIM), k_gain.reshape(1, ATT_HEAD_DIM))


def _attn_kernel(q_ref, z_ref, k_ref, vt_ref, o_ref, *, n_lat_q, t_lat, kc):
    is_lat = pl.program_id(2) < n_lat_q
    hd = ATT_HEAD_DIM
    q = jnp.concatenate([q_ref[:, g * hd:(g + 1) * hd] for g in range(ATT_GROUP)], axis=0)

    def attend(chunks):
        def scores(c):
            off, n = chunks[c]
            return lax.dot_general(k_ref[off:off + n, :], q, (((1,), (1,)), ((), ())),
                                   preferred_element_type=F32)

        m, o = None, None
        s_next = scores(0)
        for c, (off, n) in enumerate(chunks):
            s = s_next
            if c + 1 < len(chunks):
                s_next = scores(c + 1)
            mc = jnp.max(s, axis=0, keepdims=True)
            m_new = mc if m is None else jnp.maximum(m, mc)
            p = jnp.exp2((s - m_new).astype(BF16))
            oc = jnp.dot(vt_ref[:, off:off + n], p, preferred_element_type=F32)
            o = oc if o is None else jnp.exp2(m - m_new) * o + oc
            m = m_new
        o = o[0:hd] / o[hd:hd + 1]
        for g in range(ATT_GROUP):
            sl = slice(g * ATT_HEAD_DIM, (g + 1) * ATT_HEAD_DIM)
            og = o[:, g * ATT_Q_TILE:(g + 1) * ATT_Q_TILE].T
            o_ref[:, sl] = (og * _silu(z_ref[:, sl].astype(F32))).astype(BF16)

    ctx_chunk = (t_lat, CTX_LEN)

    @pl.when(is_lat)
    def _():
        attend([(off, kc) for off in range(0, t_lat, kc)] + [ctx_chunk])

    @pl.when(jnp.logical_not(is_lat))
    def _():
        attend([ctx_chunk])


def _attention(qb, ymain, kb, vt, batch, s_len):
    m = ymain.shape[0]
    t_lat = s_len - CTX_LEN
    nq = s_len // ATT_Q_TILE
    n_lat_q = t_lat // ATT_Q_TILE
    gw = ATT_GROUP * ATT_HEAD_DIM
    return pl.pallas_call(
        functools.partial(_attn_kernel, n_lat_q=n_lat_q, t_lat=t_lat,
                          kc=_pick_tile(t_lat, (ATT_KEY_CHUNK, 512, 256))),
        out_shape=jax.ShapeDtypeStruct((m, BRANCH_W), BF16),
        grid=(batch, ATT_KV_HEADS, nq),
        in_specs=[pl.BlockSpec((ATT_Q_TILE, gw), lambda b, h, i: (b * nq + i, h)),
                  pl.BlockSpec((ATT_Q_TILE, gw), lambda b, h, i: (b * nq + i, COL_AZ * 2 + h)),
                  pl.BlockSpec((s_len, ATT_HEAD_DIM), lambda b, h, i: (b, h)),
                  pl.BlockSpec((None, None, VT_ROWS, s_len), lambda b, h, i: (b, h, 0, 0))],
        out_specs=pl.BlockSpec((ATT_Q_TILE, gw), lambda b, h, i: (b * nq + i, h)),
        compiler_params=_cparams(("parallel", "parallel", "arbitrary")),
        name="attention",
    )(qb, ymain, kb, vt)


def _mlstm_kernel(*refs, reverse, final):
    if final:
        (q_ref, k_ref, v_ref, g_ref, bias_ref, o_ref, z_ref, gain_ref, hprev_ref,
         out_ref, c_s, n_s, m_s) = refs
    else:
        q_ref, k_ref, v_ref, g_ref, bias_ref, out_ref, c_s, n_s, m_s = refs
    L = ROW_TILE
    hd = MLSTM_HEAD_DIM

    @pl.when(pl.program_id(1) == 0)
    def _():
        c_s[...] = jnp.zeros(c_s.shape, F32)
        n_s[...] = jnp.zeros(n_s.shape, F32)
        m_s[...] = jnp.full(m_s.shape, M_INIT, F32)

    gates = g_ref[...] + bias_ref[...]
    log_f = jnp.minimum(gates, 0.0) - jnp.log1p(jnp.exp(-jnp.abs(gates)))
    row = lax.broadcasted_iota(jnp.int32, (L, L), 0)
    col = lax.broadcasted_iota(jnp.int32, (L, L), 1)
    tri = (col >= row) if reverse else (col <= row)
    bcum = jnp.dot(tri.astype(F32), log_f, precision=lax.Precision.HIGHEST, preferred_element_type=F32)
    bcum_t = bcum.T
    gates_t = gates.T
    last = 0 if reverse else L - 1
    for h in range(MLSTM_HEADS):
        ic = (2 if reverse else 0) * MLSTM_HEADS + h
        fc = (3 if reverse else 1) * MLSTM_HEADS + h
        hs = slice(h * hd, (h + 1) * hd)
        b_col, b_row = bcum[:, fc:fc + 1], bcum_t[fc:fc + 1, :]
        i_col, i_row = gates[:, ic:ic + 1], gates_t[ic:ic + 1, :]
        m_prev = m_s[h, 0:1, 0:1]
        log_d = jnp.where(tri, b_col - b_row + i_row, -jnp.inf)
        log_inter = b_col + m_prev
        m_t = jnp.maximum(log_inter, jnp.max(log_d, axis=1, keepdims=True))
        dmat = jnp.exp(log_d - m_t)
        inter = jnp.exp(log_inter - m_t)
        q = q_ref[:, hs]
        k = k_ref[:, hs].astype(F32) * MLSTM_K_SCALE
        v = v_ref[:, hs]
        qk = lax.dot_general(q, k.astype(BF16), (((1,), (1,)), ((), ())), preferred_element_type=F32)
        sm = qk * dmat
        c_mat = c_s[h]
        n_vec = n_s[h, 0:1, :]
        num = inter * jnp.dot(q, c_mat.astype(BF16), preferred_element_type=F32) \
            + jnp.dot(sm.astype(BF16), v, preferred_element_type=F32)
        den = inter * jnp.sum(q.astype(F32) * n_vec, axis=1, keepdims=True) + jnp.sum(sm, axis=1, keepdims=True)
        hh = num / jnp.maximum(jnp.abs(den), jnp.exp(-m_t))

        b_last = bcum[last:last + 1, fc:fc + 1]
        log_w = b_last - b_col + i_col
        m_new = jnp.maximum(b_last + m_prev, jnp.max(log_w, axis=0, keepdims=True))
        w = jnp.exp(log_w - m_new)
        decay = jnp.exp(b_last + m_prev - m_new)
        kw = k * w
        c_s[h] = decay * c_mat + lax.dot_general(kw.astype(BF16), v, (((0,), (0,)), ((), ())),
                                                 preferred_element_type=F32)
        n_s[h, 0:1, :] = decay * n_vec + jnp.sum(kw, axis=0, keepdims=True)
        m_s[h] = jnp.broadcast_to(m_new, m_s.shape[1:])

        if final:
            ht = hh + hprev_ref[:, hs]
            hn = ht * lax.rsqrt(jnp.mean(ht * ht, axis=-1, keepdims=True) + EPS)
            y = hn * gain_ref[:, hs] * jax.nn.sigmoid(o_ref[:, hs].astype(F32)) * _silu(z_ref[:, hs].astype(F32))
            out_ref[:, hs] = y.astype(BF16)
        else:
            out_ref[:, hs] = hh


def _mlstm_pass(ymain, tail, bias, gain, hprev, batch, tiles_per_batch, reverse):
    m = ymain.shape[0]
    lat = tiles_per_batch - 1
    final = hprev is not None

    def rows(b, c):
        r = jnp.where(c == 0, lat, (lat - c) if reverse else (c - 1))
        return b * tiles_per_batch + r

    def col(cidx):
        return pl.BlockSpec((ROW_TILE, BRANCH_W), lambda b, c: (rows(b, c), cidx))

    gate_spec = pl.BlockSpec((ROW_TILE, 128), lambda b, c: (rows(b, c), (N_TAIL - 128) // 128))
    in_specs = [col(COL_MQ), col(COL_MK), col(COL_MV), gate_spec, pl.BlockSpec((1, 128), lambda b, c: (0, 0))]
    args = [ymain, ymain, ymain, tail, bias]
    if final:
        in_specs += [col(COL_MO), col(COL_MZ), pl.BlockSpec((1, BRANCH_W), lambda b, c: (0, 0)),
                     pl.BlockSpec((ROW_TILE, BRANCH_W), lambda b, c: (rows(b, c), 0))]
        args += [ymain, ymain, gain.reshape(1, BRANCH_W), hprev]
    return pl.pallas_call(
        functools.partial(_mlstm_kernel, reverse=reverse, final=final),
        out_shape=jax.ShapeDtypeStruct((m, BRANCH_W), BF16 if final else F32),
        grid=(batch, tiles_per_batch),
        in_specs=in_specs,
        out_specs=pl.BlockSpec((ROW_TILE, BRANCH_W), lambda b, c: (rows(b, c), 0)),
        scratch_shapes=[pltpu.VMEM((MLSTM_HEADS, MLSTM_HEAD_DIM, MLSTM_HEAD_DIM), F32),
                        pltpu.VMEM((MLSTM_HEADS, 8, MLSTM_HEAD_DIM), F32),
                        pltpu.VMEM((MLSTM_HEADS, 8, 128), F32)],
        compiler_params=_cparams(("parallel", "arbitrary")),
        name="mlstm_bwd" if reverse else "mlstm_fwd",
    )(*args)


def _local_kernel(cu_ref, cb_ref, cc_ref, cz_ref, pu_ref, pz_ref,
                  cu_p, cc_p, pu_p, cu_n, cc_n, pu_n,
                  cw_ref, pw_ref, ps_ref, yc_ref, yd_ref, *, tiles_per_batch):
    r = pl.program_id(0) % tiles_per_batch
    lat = tiles_per_batch - 1
    has_prev = jnp.logical_and(r != 0, r != lat)
    has_next = jnp.logical_and(r != lat - 1, r != lat)
    seg_len = jnp.where(r == lat, CTX_LEN, lat * ROW_TILE)
    t0 = jnp.where(r == lat, 0, r * ROW_TILE)
    rowi = lax.broadcasted_iota(jnp.int32, (ROW_TILE, 1), 0)

    a = cc_ref[...].astype(F32) * cu_ref[...].astype(F32)
    a_prev = jnp.where(has_prev, cc_p[HALO - 1:HALO, :].astype(F32) * cu_p[HALO - 1:HALO, :].astype(F32), 0.0)
    a_next = jnp.where(has_next, cc_n[0:1, :].astype(F32) * cu_n[0:1, :].astype(F32), 0.0)
    a_m1 = jnp.where(rowi == 0, a_prev, pltpu.roll(a, 1, 0))
    a_p1 = jnp.where(rowi == ROW_TILE - 1, a_next, pltpu.roll(a, ROW_TILE - 1, 0))
    y = cw_ref[0:1, :] * a_m1 + cw_ref[1:2, :] * a + cw_ref[2:3, :] * a_p1
    yc_ref[...] = (cb_ref[...].astype(F32) * y * _silu(cz_ref[...].astype(F32))).astype(BF16)

    u = pu_ref[...].astype(F32)
    ext = jnp.concatenate([jnp.where(has_prev, pu_p[...].astype(F32), 0.0), u,
                           jnp.where(has_next, pu_n[...].astype(F32), 0.0)], axis=0)
    n_ext = ROW_TILE + 2 * HALO
    t = t0 + rowi
    for g, w in enumerate(POOL_WINDOWS):
        gs = slice(g * POOL_GROUP, (g + 1) * POOL_GROUP)
        xg = ext[:, gs]
        acc = None
        for dlt in range(-(w // 2), w - w // 2):
            sh = xg if dlt == 0 else pltpu.roll(xg, (-dlt) % n_ext, 0)
            part = sh[HALO:HALO + ROW_TILE]
            acc = part if acc is None else acc + part
        cnt = (jnp.minimum(t + (w - w // 2), seg_len) - jnp.maximum(t - w // 2, 0)).astype(F32)
        dev = acc / cnt - u[:, gs]
        pg = jnp.dot(dev.astype(BF16), pw_ref[g], preferred_element_type=F32) * ps_ref[:, gs]
        yd_ref[:, gs] = (pg * _silu(pz_ref[:, gs].astype(F32))).astype(BF16)


def _local_mixers(ymain, conv_w, pool_w, pool_scale, tiles_per_batch):
    m = ymain.shape[0]
    nt = m // ROW_TILE
    hb = ROW_TILE // HALO
    n_halo = m // HALO

    def col(cidx):
        return pl.BlockSpec((ROW_TILE, BRANCH_W), lambda i: (i, cidx))

    def prev(cidx):
        return pl.BlockSpec((HALO, BRANCH_W), lambda i: (jnp.maximum(i * hb - 1, 0), cidx))

    def nxt(cidx):
        return pl.BlockSpec((HALO, BRANCH_W), lambda i: (jnp.minimum((i + 1) * hb, n_halo - 1), cidx))

    out_sd = jax.ShapeDtypeStruct((m, BRANCH_W), BF16)
    return pl.pallas_call(
        functools.partial(_local_kernel, tiles_per_batch=tiles_per_batch),
        out_shape=(out_sd, out_sd),
        grid=(nt,),
        in_specs=[col(COL_CU), col(COL_CB), col(COL_CC), col(COL_CZ), col(COL_PU), col(COL_PZ),
                  prev(COL_CU), prev(COL_CC), prev(COL_PU), nxt(COL_CU), nxt(COL_CC), nxt(COL_PU),
                  pl.BlockSpec((3, BRANCH_W), lambda i: (0, 0)),
                  pl.BlockSpec((len(POOL_WINDOWS), POOL_GROUP, POOL_GROUP), lambda i: (0, 0, 0)),
                  pl.BlockSpec((1, BRANCH_W), lambda i: (0, 0))],
        out_specs=(pl.BlockSpec((ROW_TILE, BRANCH_W), lambda i: (i, 0)),
                   pl.BlockSpec((ROW_TILE, BRANCH_W), lambda i: (i, 0))),
        compiler_params=_cparams(("parallel",)),
        name="local_mixers",
    )(*([ymain] * 12), conv_w, pool_w.astype(BF16), pool_scale.reshape(1, BRANCH_W))


def _mergeout_kernel(*refs, final):
    if final:
        b0, b1, b2, b3, g_ref, wb_ref, wo_ref, x_ref, gt_ref, fg_ref, o_ref = refs
    else:
        b0, b1, b2, b3, g_ref, wb_ref, wo_ref, x_ref, gt_ref, o_ref = refs
    acc = None
    for j, br in enumerate((b0, b1, b2, b3)):
        gate = jax.nn.sigmoid(g_ref[:, j * D_MODEL:(j + 1) * D_MODEL].astype(F32))
        term = gate * jnp.dot(br[...], wb_ref[j], preferred_element_type=F32)
        acc = term if acc is None else acc + term
    y = x_ref[...] + gt_ref[0] * jnp.dot(acc.astype(BF16), wo_ref[...], preferred_element_type=F32)
    if final:
        y = y * lax.rsqrt(jnp.mean(y * y, axis=-1, keepdims=True) + EPS) * fg_ref[...]
    o_ref[...] = y


def _mergeout(branches, ymain, w_branch, w_out, xs, gt_t, final_gain, batch, tiles_per_batch):
    m, d = xs.shape
    final = final_gain is not None
    lat = tiles_per_batch - 1
    n_r = lat if final else tiles_per_batch
    resident = pl.Buffered(1)

    def src(b, r):
        return b * tiles_per_batch + r

    br_spec = pl.BlockSpec((ROW_TILE, BRANCH_W), lambda b, r: (src(b, r), 0))
    in_specs = [br_spec, br_spec, br_spec, br_spec,
                pl.BlockSpec((ROW_TILE, N_BRANCH * d), lambda b, r: (src(b, r), COL_GATE)),
                pl.BlockSpec((N_BRANCH, BRANCH_W, d), lambda b, r: (0, 0, 0), pipeline_mode=resident),
                pl.BlockSpec((d, d), lambda b, r: (0, 0), pipeline_mode=resident),
                pl.BlockSpec((ROW_TILE, d), lambda b, r: (src(b, r), 0)),
                pl.BlockSpec((1, 1, d), lambda b, r: (src(b, r), 0, 0))]
    args = [*branches, ymain, w_branch, w_out, xs, gt_t]
    if final:
        in_specs.append(pl.BlockSpec((1, d), lambda b, r: (0, 0)))
        args.append(final_gain.reshape(1, d))
    return pl.pallas_call(
        functools.partial(_mergeout_kernel, final=final),
        out_shape=jax.ShapeDtypeStruct((batch * n_r * ROW_TILE, d), F32),
        grid=(batch, n_r),
        in_specs=in_specs,
        out_specs=pl.BlockSpec((ROW_TILE, d), lambda b, r: (b * n_r + r, 0)),
        compiler_params=_cparams(("parallel", "parallel")),
        name="mergeout_final" if final else "mergeout",
    )(*args)


def _rope_tables(t_lat):
    pos = np.arange(t_lat)
    quarter = ATT_HEAD_DIM // 4
    freq = ROPE_THETA ** (-jnp.arange(quarter, dtype=F32) / quarter)
    a_row = jnp.asarray(pos // GRID_W, F32)[:, None] * freq[None, :]
    a_col = jnp.asarray(pos % GRID_W, F32)[:, None] * freq[None, :]
    cos_t = jnp.concatenate([jnp.cos(a_row), jnp.cos(a_row), jnp.cos(a_col), jnp.cos(a_col)], axis=-1)
    sin_t = jnp.concatenate([-jnp.sin(a_row), jnp.sin(a_row), -jnp.sin(a_col), jnp.sin(a_col)], axis=-1)
    return cos_t, sin_t


def _reorder_w_in(w):
    main = jnp.concatenate([w[:, 13840:22032], w[:, 0:1024], w[:, 1536:2560], w[:, 2560:7680],
                            w[:, 7696:13840]], axis=1).astype(BF16)
    tail = jnp.concatenate([w[:, 1024:1536], w[:, 7680:7696],
                            jnp.zeros((w.shape[0], 112), w.dtype)], axis=1).astype(BF16)
    return main, tail


def kernel(x, c, ctx, c_ctx, norm_gain, w_mod, b_mod, w_in, q_norm_gain, k_norm_gain, mlstm_gate_bias,
           mlstm_norm_gain, conv_w, pool_w, pool_scale, w_branch, w_out, final_norm_gain):
    batch, t_lat, d = x.shape
    depth = w_in.shape[0]
    assert d == D_MODEL and ctx.shape[1] == CTX_LEN and t_lat % ROW_TILE == 0 and batch < 8
    s_len = t_lat + CTX_LEN
    tiles_per_batch = s_len // ROW_TILE
    m = batch * s_len

    xs = jnp.concatenate([x, ctx], axis=1).reshape(m, d)
    cc = jnp.zeros((8, d), F32).at[:batch].set(c).at[batch].set(c_ctx)
    mod = _modulation(cc, w_mod, b_mod)
    tile_row = np.array([b if r < tiles_per_batch - 1 else batch
                         for b in range(batch) for r in range(tiles_per_batch)], np.int32)
    cos_t, sin_t = _rope_tables(t_lat)

    out = None
    for l in range(depth):
        mod_t = mod[l][tile_row][:, None, :]
        sh_t, sc_t, gt_t = mod_t[..., :d], mod_t[..., d:2 * d], mod_t[..., 2 * d:]
        w_main, w_tail = _reorder_w_in(w_in[l])
        bias = jnp.zeros((1, 128), F32).at[0, :4 * MLSTM_HEADS].set(mlstm_gate_bias[l].reshape(-1))

        h = _normmod(xs, norm_gain[l], sc_t, sh_t)
        ymain = _matmul(h, w_main, BF16, 1024)
        tail = _matmul(h, w_tail, F32, N_TAIL)

        qb, kb, vt = _qkvprep(ymain, tail, cos_t, sin_t, q_norm_gain[l], k_norm_gain[l], batch, tiles_per_batch)
        ya = _attention(qb, ymain, kb, vt, batch, s_len)
        h_f = _mlstm_pass(ymain, tail, bias, None, None, batch, tiles_per_batch, reverse=False)
        yb = _mlstm_pass(ymain, tail, bias, mlstm_norm_gain[l], h_f, batch, tiles_per_batch, reverse=True)
        yc, yd = _local_mixers(ymain, conv_w[l], pool_w[l], pool_scale[l], tiles_per_batch)

        last = l == depth - 1
        res = _mergeout((ya, yb, yc, yd), ymain, w_branch[l].astype(BF16), w_out[l].astype(BF16), xs, gt_t,
                        final_norm_gain if last else None, batch, tiles_per_batch)
        if last:
            out = res.reshape(batch, t_lat, d)
        else:
            xs = res
    return out
```

```python
import functools

import numpy as np
import jax
import jax.numpy as jnp
from jax import lax
from jax.experimental import pallas as pl
from jax.experimental.pallas import tpu as pltpu

F32 = jnp.float32
BF16 = jnp.bfloat16

D_MODEL = 2048
BRANCH_W = 1024
GRID_W = 64
CTX_LEN = 256
EPS = 1e-6
ATT_HEAD_DIM = 128
ATT_GROUP = 4
ATT_KV_HEADS = 2
ATT_SCALE = ATT_HEAD_DIM ** -0.5
LOG2_E = 1.4426950408889634
ROPE_THETA = 10000.0
MLSTM_HEADS = 4
MLSTM_HEAD_DIM = 256
MLSTM_K_SCALE = MLSTM_HEAD_DIM ** -0.5
M_INIT = -1e30
POOL_WINDOWS = (2, 4, 8, 16)
POOL_GROUP = 256
N_BRANCH = 4

ROW_TILE = 256
HALO = 16
ATT_Q_TILE = 128
ATT_KEY_CHUNK = 1024
VT_ROWS = ATT_HEAD_DIM + 16
V7X_VMEM_LIMIT = 56 * 1024 * 1024

A_AQ, A_AZ, A_MQ, A_MK, A_MV, A_MO, A_MZ = range(7)
B_GATE = 0
B_CU, B_CB, B_CC, B_CZ, B_PU, B_PZ = range(8, 14)
N_TAIL = 640


def _cparams(sem, vmem=V7X_VMEM_LIMIT):
    return pltpu.CompilerParams(dimension_semantics=sem, vmem_limit_bytes=vmem)


def _silu(x):
    return x * jax.nn.sigmoid(x)


def _mod_kernel(c_ref, w_ref, b_ref, o_ref):
    a = _silu(c_ref[...]).astype(BF16)
    o_ref[...] = jnp.dot(a, w_ref[...].astype(BF16), preferred_element_type=F32) + b_ref[...]


def _modulation(cc, w_mod, b_mod):
    depth, d, n = w_mod.shape
    tn = 768
    return pl.pallas_call(
        _mod_kernel,
        out_shape=jax.ShapeDtypeStruct((depth, 8, n), F32),
        grid=(depth, n // tn),
        in_specs=[pl.BlockSpec((8, d), lambda l, j: (0, 0)),
                  pl.BlockSpec((None, d, tn), lambda l, j: (l, 0, j)),
                  pl.BlockSpec((None, 1, tn), lambda l, j: (l, 0, j))],
        out_specs=pl.BlockSpec((None, 8, tn), lambda l, j: (l, 0, j)),
        compiler_params=_cparams(("parallel", "parallel")),
        name="modulation",
    )(cc, w_mod, b_mod.reshape(depth, 1, n))


def _normmod_kernel(x_ref, g_ref, sc_ref, sh_ref, o_ref):
    x = x_ref[...]
    y = x * lax.rsqrt(jnp.mean(x * x, axis=-1, keepdims=True) + EPS) * g_ref[...]
    o_ref[...] = (y * (1.0 + sc_ref[0]) + sh_ref[0]).astype(BF16)


def _normmod(xs, gain, sc_t, sh_t):
    m, d = xs.shape
    nt = m // ROW_TILE
    return pl.pallas_call(
        _normmod_kernel,
        out_shape=jax.ShapeDtypeStruct((m, d), BF16),
        grid=(nt,),
        in_specs=[pl.BlockSpec((ROW_TILE, d), lambda i: (i, 0)),
                  pl.BlockSpec((1, d), lambda i: (0, 0)),
                  pl.BlockSpec((1, 1, d), lambda i: (i, 0, 0)),
                  pl.BlockSpec((1, 1, d), lambda i: (i, 0, 0))],
        out_specs=pl.BlockSpec((ROW_TILE, d), lambda i: (i, 0)),
        compiler_params=_cparams(("parallel",)),
        name="normmod",
    )(xs, gain.reshape(1, d), sc_t, sh_t)


def _normmod_first_kernel(x_ref, c_ref, g_ref, sc_ref, sh_ref, o_ref, xs_ref, *, tiles_per_batch):
    x = jnp.where(pl.program_id(1) == tiles_per_batch - 1, c_ref[...], x_ref[...])
    xs_ref[...] = x
    y = x * lax.rsqrt(jnp.mean(x * x, axis=-1, keepdims=True) + EPS) * g_ref[...]
    o_ref[...] = (y * (1.0 + sc_ref[0]) + sh_ref[0]).astype(BF16)


def _normmod_first(x, ctx, gain, sc_t, sh_t):
    batch, t_lat, d = x.shape
    tpb = t_lat // ROW_TILE + 1
    m = batch * tpb * ROW_TILE
    tab_spec = pl.BlockSpec((1, 1, d), lambda b, r: (b * tpb + r, 0, 0))
    row_spec = pl.BlockSpec((ROW_TILE, d), lambda b, r: (b * tpb + r, 0))
    return pl.pallas_call(
        functools.partial(_normmod_first_kernel, tiles_per_batch=tpb),
        out_shape=(jax.ShapeDtypeStruct((m, d), BF16), jax.ShapeDtypeStruct((m, d), F32)),
        grid=(batch, tpb),
        in_specs=[pl.BlockSpec((None, ROW_TILE, d), lambda b, r: (b, jnp.minimum(r, tpb - 2), 0)),
                  pl.BlockSpec((None, CTX_LEN, d), lambda b, r: (b, 0, 0)),
                  pl.BlockSpec((1, d), lambda b, r: (0, 0)),
                  tab_spec, tab_spec],
        out_specs=(row_spec, row_spec),
        compiler_params=_cparams(("parallel", "parallel")),
        name="normmod_first",
    )(x, ctx, gain.reshape(1, d), sc_t, sh_t)


def _matmul_kernel(a_ref, w_ref, o_ref):
    o_ref[...] = jnp.dot(a_ref[...], w_ref[...], preferred_element_type=F32).astype(o_ref.dtype)


def _pick_tile(n, candidates):
    for t in candidates:
        if n % t == 0:
            return t
    raise ValueError(f"no tile for {n}")


def _matmul(a, w, layer, out_dtype, tn, name):
    m, k = a.shape
    n = w.shape[2]
    tm = _pick_tile(m, (1024, 512, 256))
    return pl.pallas_call(
        _matmul_kernel,
        out_shape=jax.ShapeDtypeStruct((m, n), out_dtype),
        grid=(m // tm, n // tn),
        in_specs=[pl.BlockSpec((tm, k), lambda i, j: (i, 0)),
                  pl.BlockSpec((None, k, tn), lambda i, j: (layer, 0, j))],
        out_specs=pl.BlockSpec((tm, tn), lambda i, j: (i, j)),
        compiler_params=_cparams(("parallel", "parallel")),
        name=name,
    )(a, w)


def _swap32(x):
    lane = lax.broadcasted_iota(jnp.int32, x.shape, 1)
    return jnp.where((lane % 64) < 32, pltpu.roll(x, 96, 1), pltpu.roll(x, 32, 1))


def _norm_rope(x, gain, cos, sin, is_lat):
    x = x * lax.rsqrt(jnp.mean(x * x, axis=-1, keepdims=True) + EPS) * gain
    xr = x * cos + _swap32(x) * sin
    return jnp.where(is_lat, xr, x)


def _qkvprep_kernel(q_ref, t_ref, cos_ref, sin_ref, qg_ref, kg_ref, q_out, k_out, vt_out, *, tiles_per_batch):
    is_lat = pl.program_id(1) < tiles_per_batch - 1
    cos, sin, kg = cos_ref[...], sin_ref[...], kg_ref[...]
    qg = qg_ref[...] * (ATT_SCALE * LOG2_E)
    hd = ATT_HEAD_DIM
    for h in range(ATT_KV_HEADS * ATT_GROUP):
        sl = slice(h * hd, (h + 1) * hd)
        q_out[:, sl] = _norm_rope(q_ref[:, sl].astype(F32), qg, cos, sin, is_lat).astype(BF16)
    for h in range(ATT_KV_HEADS):
        sl = slice(h * hd, (h + 1) * hd)
        k_out[:, sl] = _norm_rope(t_ref[:, sl], kg, cos, sin, is_lat).astype(BF16)
        vsl = slice((ATT_KV_HEADS + h) * hd, (ATT_KV_HEADS + h + 1) * hd)
        vt_out[h, 0:hd, :] = t_ref[:, vsl].T.astype(BF16)
        vt_out[h, hd:, :] = jnp.ones((VT_ROWS - hd, ROW_TILE), BF16)


def _qkvprep(ya, tail, cos_t, sin_t, q_gain, k_gain, batch, tiles_per_batch):
    m = tail.shape[0]
    lat_tiles = tiles_per_batch - 1
    kvw = ATT_KV_HEADS * ATT_HEAD_DIM
    rope_spec = pl.BlockSpec((ROW_TILE, ATT_HEAD_DIM), lambda b, r: (jnp.minimum(r, lat_tiles - 1), 0))
    gain_spec = pl.BlockSpec((1, ATT_HEAD_DIM), lambda b, r: (0, 0))
    return pl.pallas_call(
        functools.partial(_qkvprep_kernel, tiles_per_batch=tiles_per_batch),
        out_shape=(jax.ShapeDtypeStruct((m, BRANCH_W), BF16),
                   jax.ShapeDtypeStruct((m, kvw), BF16),
                   jax.ShapeDtypeStruct((batch, ATT_KV_HEADS, VT_ROWS, tiles_per_batch * ROW_TILE), BF16)),
        grid=(batch, tiles_per_batch),
        in_specs=[pl.BlockSpec((ROW_TILE, BRANCH_W), lambda b, r: (b * tiles_per_batch + r, A_AQ)),
                  pl.BlockSpec((ROW_TILE, 2 * kvw), lambda b, r: (b * tiles_per_batch + r, 0)),
                  rope_spec, rope_spec, gain_spec, gain_spec],
        out_specs=(pl.BlockSpec((ROW_TILE, BRANCH_W), lambda b, r: (b * tiles_per_batch + r, 0)),
                   pl.BlockSpec((ROW_TILE, kvw), lambda b, r: (b * tiles_per_batch + r, 0)),
                   pl.BlockSpec((None, ATT_KV_HEADS, VT_ROWS, ROW_TILE), lambda b, r: (b, 0, 0, r))),
        compiler_params=_cparams(("parallel", "parallel")),
        name="qkvprep",
    )(ya, tail, cos_t, sin_t, q_gain.reshape(1, ATT_HEAD_DIM), k_gain.reshape(1, ATT_HEAD_DIM))


def _attn_kernel(q_ref, z_ref, k_ref, vt_ref, o_ref, *, n_lat_q, t_lat, kc):
    is_lat = pl.program_id(2) < n_lat_q
    hd = ATT_HEAD_DIM
    q = jnp.concatenate([q_ref[:, g * hd:(g + 1) * hd] for g in range(ATT_GROUP)], axis=0)

    def attend(chunks):
        def scores(c):
            off, n = chunks[c]
            return lax.dot_general(k_ref[off:off + n, :], q, (((1,), (1,)), ((), ())),
                                   preferred_element_type=F32)

        m, o = None, None
        s_next = scores(0)
        for c, (off, n) in enumerate(chunks):
            s = s_next
            if c + 1 < len(chunks):
                s_next = scores(c + 1)
            mc = jnp.max(s, axis=0, keepdims=True)
            m_new = mc if m is None else jnp.maximum(m, mc)
            p = jnp.exp2((s - m_new).astype(BF16))
            oc = jnp.dot(vt_ref[:, off:off + n], p, preferred_element_type=F32)
            o = oc if o is None else jnp.exp2(m - m_new) * o + oc
            m = m_new
        o = o[0:hd] / o[hd:hd + 1]
        for g in range(ATT_GROUP):
            sl = slice(g * ATT_HEAD_DIM, (g + 1) * ATT_HEAD_DIM)
            og = o[:, g * ATT_Q_TILE:(g + 1) * ATT_Q_TILE].T
            o_ref[:, sl] = (og * _silu(z_ref[:, sl].astype(F32))).astype(BF16)

    ctx_chunk = (t_lat, CTX_LEN)

    @pl.when(is_lat)
    def _():
        attend([(off, kc) for off in range(0, t_lat, kc)] + [ctx_chunk])

    @pl.when(jnp.logical_not(is_lat))
    def _():
        attend([ctx_chunk])


def _attention(qb, ya, kb, vt, batch, s_len):
    m = ya.shape[0]
    t_lat = s_len - CTX_LEN
    nq = s_len // ATT_Q_TILE
    n_lat_q = t_lat // ATT_Q_TILE
    gw = ATT_GROUP * ATT_HEAD_DIM
    return pl.pallas_call(
        functools.partial(_attn_kernel, n_lat_q=n_lat_q, t_lat=t_lat,
                          kc=_pick_tile(t_lat, (ATT_KEY_CHUNK, 512, 256))),
        out_shape=jax.ShapeDtypeStruct((m, BRANCH_W), BF16),
        grid=(batch, ATT_KV_HEADS, nq),
        in_specs=[pl.BlockSpec((ATT_Q_TILE, gw), lambda b, h, i: (b * nq + i, h)),
                  pl.BlockSpec((ATT_Q_TILE, gw), lambda b, h, i: (b * nq + i, A_AZ * 2 + h)),
                  pl.BlockSpec((s_len, ATT_HEAD_DIM), lambda b, h, i: (b, h)),
                  pl.BlockSpec((None, None, VT_ROWS, s_len), lambda b, h, i: (b, h, 0, 0))],
        out_specs=pl.BlockSpec((ATT_Q_TILE, gw), lambda b, h, i: (b * nq + i, h)),
        compiler_params=_cparams(("parallel", "parallel", "arbitrary")),
        name="attention",
    )(qb, ya, kb, vt)


def _mlstm_kernel(*refs, reverse, final):
    if final:
        (q_ref, k_ref, v_ref, g_ref, bias_ref, o_ref, z_ref, gain_ref, hprev_ref,
         out_ref, c_s, n_s, m_s) = refs
    else:
        q_ref, k_ref, v_ref, g_ref, bias_ref, out_ref, c_s, n_s, m_s = refs
    L = ROW_TILE
    hd = MLSTM_HEAD_DIM

    @pl.when(pl.program_id(1) == 0)
    def _():
        c_s[...] = jnp.zeros(c_s.shape, F32)
        n_s[...] = jnp.zeros(n_s.shape, F32)
        m_s[...] = jnp.full(m_s.shape, M_INIT, F32)

    gates = g_ref[...] + bias_ref[...]
    log_f = jnp.minimum(gates, 0.0) - jnp.log1p(jnp.exp(-jnp.abs(gates)))
    row = lax.broadcasted_iota(jnp.int32, (L, L), 0)
    col = lax.broadcasted_iota(jnp.int32, (L, L), 1)
    tri = (col >= row) if reverse else (col <= row)
    bcum = jnp.dot(tri.astype(F32), log_f, precision=lax.Precision.HIGHEST, preferred_element_type=F32)
    bcum_t = bcum.T
    gates_t = gates.T
    last = 0 if reverse else L - 1
    for h in range(MLSTM_HEADS):
        ic = (2 if reverse else 0) * MLSTM_HEADS + h
        fc = (3 if reverse else 1) * MLSTM_HEADS + h
        hs = slice(h * hd, (h + 1) * hd)
        b_col, b_row = bcum[:, fc:fc + 1], bcum_t[fc:fc + 1, :]
        i_col, i_row = gates[:, ic:ic + 1], gates_t[ic:ic + 1, :]
        m_prev = m_s[h, 0:1, 0:1]
        log_d = jnp.where(tri, b_col - b_row + i_row, -jnp.inf)
        log_inter = b_col + m_prev
        m_t = jnp.maximum(log_inter, jnp.max(log_d, axis=1, keepdims=True))
        dmat = jnp.exp(log_d - m_t)
        inter = jnp.exp(log_inter - m_t)
        q = q_ref[:, hs]
        k = k_ref[:, hs].astype(F32) * MLSTM_K_SCALE
        v = v_ref[:, hs]
        qk = lax.dot_general(q, k.astype(BF16), (((1,), (1,)), ((), ())), preferred_element_type=F32)
        sm = qk * dmat
        c_mat = c_s[h]
        n_vec = n_s[h, 0:1, :]
        num = inter * jnp.dot(q, c_mat.astype(BF16), preferred_element_type=F32) \
            + jnp.dot(sm.astype(BF16), v, preferred_element_type=F32)
        den = inter * jnp.sum(q.astype(F32) * n_vec, axis=1, keepdims=True) + jnp.sum(sm, axis=1, keepdims=True)
        hh = num / jnp.maximum(jnp.abs(den), jnp.exp(-m_t))

        b_last = bcum[last:last + 1, fc:fc + 1]
        log_w = b_last - b_col + i_col
        m_new = jnp.maximum(b_last + m_prev, jnp.max(log_w, axis=0, keepdims=True))
        w = jnp.exp(log_w - m_new)
        decay = jnp.exp(b_last + m_prev - m_new)
        kw = k * w
        c_s[h] = decay * c_mat + lax.dot_general(kw.astype(BF16), v, (((0,), (0,)), ((), ())),
                                                 preferred_element_type=F32)
        n_s[h, 0:1, :] = decay * n_vec + jnp.sum(kw, axis=0, keepdims=True)
        m_s[h] = jnp.broadcast_to(m_new, m_s.shape[1:])

        if final:
            ht = hh + hprev_ref[:, hs]
            hn = ht * lax.rsqrt(jnp.mean(ht * ht, axis=-1, keepdims=True) + EPS)
            y = hn * gain_ref[:, hs] * jax.nn.sigmoid(o_ref[:, hs].astype(F32)) * _silu(z_ref[:, hs].astype(F32))
            out_ref[:, hs] = y.astype(BF16)
        else:
            out_ref[:, hs] = hh


def _mlstm_pass(ya, tail, bias, gain, hprev, batch, tiles_per_batch, reverse):
    m = ya.shape[0]
    lat = tiles_per_batch - 1
    final = hprev is not None

    def rows(b, c):
        r = jnp.where(c == 0, lat, (lat - c) if reverse else (c - 1))
        return b * tiles_per_batch + r

    def col(cidx):
        return pl.BlockSpec((ROW_TILE, BRANCH_W), lambda b, c: (rows(b, c), cidx))

    gate_spec = pl.BlockSpec((ROW_TILE, 128), lambda b, c: (rows(b, c), (N_TAIL - 128) // 128))
    in_specs = [col(A_MQ), col(A_MK), col(A_MV), gate_spec, pl.BlockSpec((1, 128), lambda b, c: (0, 0))]
    args = [ya, ya, ya, tail, bias]
    if final:
        in_specs += [col(A_MO), col(A_MZ), pl.BlockSpec((1, BRANCH_W), lambda b, c: (0, 0)),
                     pl.BlockSpec((ROW_TILE, BRANCH_W), lambda b, c: (rows(b, c), 0))]
        args += [ya, ya, gain.reshape(1, BRANCH_W), hprev]
    return pl.pallas_call(
        functools.partial(_mlstm_kernel, reverse=reverse, final=final),
        out_shape=jax.ShapeDtypeStruct((m, BRANCH_W), BF16 if final else F32),
        grid=(batch, tiles_per_batch),
        in_specs=in_specs,
        out_specs=pl.BlockSpec((ROW_TILE, BRANCH_W), lambda b, c: (rows(b, c), 0)),
        scratch_shapes=[pltpu.VMEM((MLSTM_HEADS, MLSTM_HEAD_DIM, MLSTM_HEAD_DIM), F32),
                        pltpu.VMEM((MLSTM_HEADS, 8, MLSTM_HEAD_DIM), F32),
                        pltpu.VMEM((MLSTM_HEADS, 8, 128), F32)],
        compiler_params=_cparams(("parallel", "arbitrary")),
        name="mlstm_bwd" if reverse else "mlstm_fwd",
    )(*args)


def _local_kernel(cu_ref, cb_ref, cc_ref, cz_ref, pu_ref, pz_ref,
                  cu_p, cc_p, pu_p, cu_n, cc_n, pu_n,
                  cw_ref, pw_ref, ps_ref, yc_ref, yd_ref, *, tiles_per_batch):
    r = pl.program_id(0) % tiles_per_batch
    lat = tiles_per_batch - 1
    has_prev = jnp.logical_and(r != 0, r != lat)
    has_next = jnp.logical_and(r != lat - 1, r != lat)
    seg_len = jnp.where(r == lat, CTX_LEN, lat * ROW_TILE)
    t0 = jnp.where(r == lat, 0, r * ROW_TILE)
    rowi = lax.broadcasted_iota(jnp.int32, (ROW_TILE, 1), 0)

    a = cc_ref[...].astype(F32) * cu_ref[...].astype(F32)
    a_prev = jnp.where(has_prev, cc_p[HALO - 1:HALO, :].astype(F32) * cu_p[HALO - 1:HALO, :].astype(F32), 0.0)
    a_next = jnp.where(has_next, cc_n[0:1, :].astype(F32) * cu_n[0:1, :].astype(F32), 0.0)
    a_m1 = jnp.where(rowi == 0, a_prev, pltpu.roll(a, 1, 0))
    a_p1 = jnp.where(rowi == ROW_TILE - 1, a_next, pltpu.roll(a, ROW_TILE - 1, 0))
    y = cw_ref[0:1, :] * a_m1 + cw_ref[1:2, :] * a + cw_ref[2:3, :] * a_p1
    yc_ref[...] = (cb_ref[...].astype(F32) * y * _silu(cz_ref[...].astype(F32))).astype(BF16)

    u = pu_ref[...].astype(F32)
    ext = jnp.concatenate([jnp.where(has_prev, pu_p[...].astype(F32), 0.0), u,
                           jnp.where(has_next, pu_n[...].astype(F32), 0.0)], axis=0)
    n_ext = ROW_TILE + 2 * HALO
    t = t0 + rowi
    for g, w in enumerate(POOL_WINDOWS):
        gs = slice(g * POOL_GROUP, (g + 1) * POOL_GROUP)
        xg = ext[:, gs]
        acc = None
        for dlt in range(-(w // 2), w - w // 2):
            sh = xg if dlt == 0 else pltpu.roll(xg, (-dlt) % n_ext, 0)
            part = sh[HALO:HALO + ROW_TILE]
            acc = part if acc is None else acc + part
        cnt = (jnp.minimum(t + (w - w // 2), seg_len) - jnp.maximum(t - w // 2, 0)).astype(F32)
        dev = acc / cnt - u[:, gs]
        pg = jnp.dot(dev.astype(BF16), pw_ref[g], preferred_element_type=F32) * ps_ref[:, gs]
        yd_ref[:, gs] = (pg * _silu(pz_ref[:, gs].astype(F32))).astype(BF16)


def _local_mixers(yb, conv_w, pool_w, pool_scale, tiles_per_batch):
    m = yb.shape[0]
    nt = m // ROW_TILE
    hb = ROW_TILE // HALO
    n_halo = m // HALO

    def col(cidx):
        return pl.BlockSpec((ROW_TILE, BRANCH_W), lambda i: (i, cidx))

    def prev(cidx):
        return pl.BlockSpec((HALO, BRANCH_W), lambda i: (jnp.maximum(i * hb - 1, 0), cidx))

    def nxt(cidx):
        return pl.BlockSpec((HALO, BRANCH_W), lambda i: (jnp.minimum((i + 1) * hb, n_halo - 1), cidx))

    out_sd = jax.ShapeDtypeStruct((m, BRANCH_W), BF16)
    return pl.pallas_call(
        functools.partial(_local_kernel, tiles_per_batch=tiles_per_batch),
        out_shape=(out_sd, out_sd),
        grid=(nt,),
        in_specs=[col(B_CU), col(B_CB), col(B_CC), col(B_CZ), col(B_PU), col(B_PZ),
                  prev(B_CU), prev(B_CC), prev(B_PU), nxt(B_CU), nxt(B_CC), nxt(B_PU),
                  pl.BlockSpec((3, BRANCH_W), lambda i: (0, 0)),
                  pl.BlockSpec((len(POOL_WINDOWS), POOL_GROUP, POOL_GROUP), lambda i: (0, 0, 0)),
                  pl.BlockSpec((1, BRANCH_W), lambda i: (0, 0))],
        out_specs=(pl.BlockSpec((ROW_TILE, BRANCH_W), lambda i: (i, 0)),
                   pl.BlockSpec((ROW_TILE, BRANCH_W), lambda i: (i, 0))),
        compiler_params=_cparams(("parallel",)),
        name="local_mixers",
    )(*([yb] * 12), conv_w, pool_w, pool_scale.reshape(1, BRANCH_W))


def _mergeout_kernel(*refs, final):
    if final:
        b0, b1, b2, b3, g_ref, wb_ref, wo_ref, x_ref, gt_ref, fg_ref, o_ref = refs
    else:
        b0, b1, b2, b3, g_ref, wb_ref, wo_ref, x_ref, gt_ref, o_ref = refs
    acc = None
    for j, br in enumerate((b0, b1, b2, b3)):
        gate = jax.nn.sigmoid(g_ref[:, j * D_MODEL:(j + 1) * D_MODEL].astype(F32))
        term = gate * jnp.dot(br[...], wb_ref[j], preferred_element_type=F32)
        acc = term if acc is None else acc + term
    y = x_ref[...] + gt_ref[0] * jnp.dot(acc.astype(BF16), wo_ref[...], preferred_element_type=F32)
    if final:
        y = y * lax.rsqrt(jnp.mean(y * y, axis=-1, keepdims=True) + EPS) * fg_ref[...]
    o_ref[...] = y


def _mergeout(branches, yb, w_branch, w_out, layer, xs, gt_t, final_gain, batch, tiles_per_batch):
    m, d = xs.shape
    final = final_gain is not None
    lat = tiles_per_batch - 1
    n_r = lat if final else tiles_per_batch
    resident = pl.Buffered(1)

    def src(b, r):
        return b * tiles_per_batch + r

    br_spec = pl.BlockSpec((ROW_TILE, BRANCH_W), lambda b, r: (src(b, r), 0))
    in_specs = [br_spec, br_spec, br_spec, br_spec,
                pl.BlockSpec((ROW_TILE, N_BRANCH * d), lambda b, r: (src(b, r), B_GATE)),
                pl.BlockSpec((None, N_BRANCH, BRANCH_W, d), lambda b, r: (layer, 0, 0, 0), pipeline_mode=resident),
                pl.BlockSpec((None, d, d), lambda b, r: (layer, 0, 0), pipeline_mode=resident),
                pl.BlockSpec((ROW_TILE, d), lambda b, r: (src(b, r), 0)),
                pl.BlockSpec((1, 1, d), lambda b, r: (src(b, r), 0, 0))]
    args = [*branches, yb, w_branch, w_out, xs, gt_t]
    if final:
        in_specs.append(pl.BlockSpec((1, d), lambda b, r: (0, 0)))
        args.append(final_gain.reshape(1, d))
    return pl.pallas_call(
        functools.partial(_mergeout_kernel, final=final),
        out_shape=jax.ShapeDtypeStruct((batch * n_r * ROW_TILE, d), F32),
        grid=(batch, n_r),
        in_specs=in_specs,
        out_specs=pl.BlockSpec((ROW_TILE, d), lambda b, r: (b * n_r + r, 0)),
        compiler_params=_cparams(("parallel", "parallel")),
        name="mergeout_final" if final else "mergeout",
    )(*args)


def _rope_tables(t_lat):
    pos = np.arange(t_lat)
    quarter = ATT_HEAD_DIM // 4
    freq = ROPE_THETA ** (-jnp.arange(quarter, dtype=F32) / quarter)
    a_row = jnp.asarray(pos // GRID_W, F32)[:, None] * freq[None, :]
    a_col = jnp.asarray(pos % GRID_W, F32)[:, None] * freq[None, :]
    cos_t = jnp.concatenate([jnp.cos(a_row), jnp.cos(a_row), jnp.cos(a_col), jnp.cos(a_col)], axis=-1)
    sin_t = jnp.concatenate([-jnp.sin(a_row), jnp.sin(a_row), -jnp.sin(a_col), jnp.sin(a_col)], axis=-1)
    return cos_t, sin_t


def _split_w_in(w):
    w_a = jnp.concatenate([w[:, :, 0:1024], w[:, :, 1536:7680]], axis=2).astype(BF16)
    w_b = jnp.concatenate([w[:, :, 13840:22032], w[:, :, 7696:13840]], axis=2).astype(BF16)
    w_t = jnp.concatenate([w[:, :, 1024:1536], w[:, :, 7680:7696],
                           jnp.zeros(w.shape[:2] + (112,), w.dtype)], axis=2).astype(BF16)
    return w_a, w_b, w_t


def kernel(x, c, ctx, c_ctx, norm_gain, w_mod, b_mod, w_in, q_norm_gain, k_norm_gain, mlstm_gate_bias,
           mlstm_norm_gain, conv_w, pool_w, pool_scale, w_branch, w_out, final_norm_gain):
    batch, t_lat, d = x.shape
    depth = w_in.shape[0]
    assert d == D_MODEL and ctx.shape[1] == CTX_LEN and t_lat % ROW_TILE == 0 and batch < 8
    s_len = t_lat + CTX_LEN
    tiles_per_batch = s_len // ROW_TILE

    cc = jnp.zeros((8, d), F32).at[:batch].set(c).at[batch].set(c_ctx)
    mod = _modulation(cc, w_mod, b_mod)
    tile_row = np.array([b if r < tiles_per_batch - 1 else batch
                         for b in range(batch) for r in range(tiles_per_batch)], np.int32)
    cos_t, sin_t = _rope_tables(t_lat)
    w_a, w_b, w_t = _split_w_in(w_in)
    w_branch_b, w_out_b, pool_w_b = w_branch.astype(BF16), w_out.astype(BF16), pool_w.astype(BF16)

    xs, out = None, None
    for l in range(depth):
        mod_t = mod[l][tile_row][:, None, :]
        sh_t, sc_t, gt_t = mod_t[..., :d], mod_t[..., d:2 * d], mod_t[..., 2 * d:]
        bias = jnp.zeros((1, 128), F32).at[0, :4 * MLSTM_HEADS].set(mlstm_gate_bias[l].reshape(-1))

        if l == 0:
            h, xs = _normmod_first(x, ctx, norm_gain[l], sc_t, sh_t)
        else:
            h = _normmod(xs, norm_gain[l], sc_t, sh_t)
        ya = _matmul(h, w_a, l, BF16, 1024, "inproj_a")
        yb = _matmul(h, w_b, l, BF16, 1024, "inproj_b")
        tail = _matmul(h, w_t, l, F32, N_TAIL, "inproj_tail")

        qb, kb, vt = _qkvprep(ya, tail, cos_t, sin_t, q_norm_gain[l], k_norm_gain[l], batch, tiles_per_batch)
        y_att = _attention(qb, ya, kb, vt, batch, s_len)
        h_f = _mlstm_pass(ya, tail, bias, None, None, batch, tiles_per_batch, reverse=False)
        y_mls = _mlstm_pass(ya, tail, bias, mlstm_norm_gain[l], h_f, batch, tiles_per_batch, reverse=True)
        y_conv, y_pool = _local_mixers(yb, conv_w[l], pool_w_b[l], pool_scale[l], tiles_per_batch)

        last = l == depth - 1
        res = _mergeout((y_att, y_mls, y_conv, y_pool), yb, w_branch_b, w_out_b, l, xs, gt_t,
                        final_norm_gain if last else None, batch, tiles_per_batch)
        if last:
            out = res.reshape(batch, t_lat, d)
        else:
            xs = res
    return out
```

```python
import functools

import numpy as np
import jax
import jax.numpy as jnp
from jax import lax
from jax.experimental import pallas as pl
from jax.experimental.pallas import tpu as pltpu

F32 = jnp.float32
BF16 = jnp.bfloat16

D_MODEL = 2048
BRANCH_W = 1024
GRID_W = 64
CTX_LEN = 256
EPS = 1e-6
ATT_HEAD_DIM = 128
ATT_GROUP = 4
ATT_KV_HEADS = 2
ATT_SCALE = ATT_HEAD_DIM ** -0.5
LOG2_E = 1.4426950408889634
ROPE_THETA = 10000.0
MLSTM_HEADS = 4
MLSTM_HEAD_DIM = 256
MLSTM_K_SCALE = MLSTM_HEAD_DIM ** -0.5
M_INIT = -1e30
POOL_WINDOWS = (2, 4, 8, 16)
POOL_GROUP = 256
N_BRANCH = 4

ROW_TILE = 256
HALO = 16
POOL_K = 384
ATT_Q_TILE = 128
ATT_KEY_CHUNK = 1024
VT_ROWS = ATT_HEAD_DIM + 16
V7X_VMEM_LIMIT = 56 * 1024 * 1024

A_AQ, A_AZ, A_MQ, A_MK, A_MV, A_MO, A_MZ = range(7)
B_GATE = 0
B_CU, B_CB, B_CC, B_CZ, B_PU, B_PZ = range(8, 14)
N_TAIL = 640


def _cparams(sem, vmem=V7X_VMEM_LIMIT):
    return pltpu.CompilerParams(dimension_semantics=sem, vmem_limit_bytes=vmem)


def _silu(x):
    return x * jax.nn.sigmoid(x)


def _mod_kernel(c_ref, w_ref, b_ref, o_ref):
    a = _silu(c_ref[...]).astype(BF16)
    o_ref[...] = jnp.dot(a, w_ref[...].astype(BF16), preferred_element_type=F32) + b_ref[...]


def _modulation(cc, w_mod, b_mod):
    depth, d, n = w_mod.shape
    tn = 768
    return pl.pallas_call(
        _mod_kernel,
        out_shape=jax.ShapeDtypeStruct((depth, 8, n), F32),
        grid=(depth, n // tn),
        in_specs=[pl.BlockSpec((8, d), lambda l, j: (0, 0)),
                  pl.BlockSpec((None, d, tn), lambda l, j: (l, 0, j)),
                  pl.BlockSpec((None, 1, tn), lambda l, j: (l, 0, j))],
        out_specs=pl.BlockSpec((None, 8, tn), lambda l, j: (l, 0, j)),
        compiler_params=_cparams(("parallel", "parallel")),
        name="modulation",
    )(cc, w_mod, b_mod.reshape(depth, 1, n))


def _normmod_kernel(x_ref, g_ref, sc_ref, sh_ref, o_ref):
    x = x_ref[...]
    y = x * lax.rsqrt(jnp.mean(x * x, axis=-1, keepdims=True) + EPS) * g_ref[...]
    o_ref[...] = (y * (1.0 + sc_ref[0]) + sh_ref[0]).astype(BF16)


def _normmod(xs, gain, sc_t, sh_t):
    m, d = xs.shape
    nt = m // ROW_TILE
    return pl.pallas_call(
        _normmod_kernel,
        out_shape=jax.ShapeDtypeStruct((m, d), BF16),
        grid=(nt,),
        in_specs=[pl.BlockSpec((ROW_TILE, d), lambda i: (i, 0)),
                  pl.BlockSpec((1, d), lambda i: (0, 0)),
                  pl.BlockSpec((1, 1, d), lambda i: (i, 0, 0)),
                  pl.BlockSpec((1, 1, d), lambda i: (i, 0, 0))],
        out_specs=pl.BlockSpec((ROW_TILE, d), lambda i: (i, 0)),
        compiler_params=_cparams(("parallel",)),
        name="normmod",
    )(xs, gain.reshape(1, d), sc_t, sh_t)


def _normmod_first_kernel(x_ref, c_ref, g_ref, sc_ref, sh_ref, o_ref, xs_ref, *, tiles_per_batch):
    x = jnp.where(pl.program_id(1) == tiles_per_batch - 1, c_ref[...], x_ref[...])
    xs_ref[...] = x
    y = x * lax.rsqrt(jnp.mean(x * x, axis=-1, keepdims=True) + EPS) * g_ref[...]
    o_ref[...] = (y * (1.0 + sc_ref[0]) + sh_ref[0]).astype(BF16)


def _normmod_first(x, ctx, gain, sc_t, sh_t):
    batch, t_lat, d = x.shape
    tpb = t_lat // ROW_TILE + 1
    m = batch * tpb * ROW_TILE
    tab_spec = pl.BlockSpec((1, 1, d), lambda b, r: (b * tpb + r, 0, 0))
    row_spec = pl.BlockSpec((ROW_TILE, d), lambda b, r: (b * tpb + r, 0))
    return pl.pallas_call(
        functools.partial(_normmod_first_kernel, tiles_per_batch=tpb),
        out_shape=(jax.ShapeDtypeStruct((m, d), BF16), jax.ShapeDtypeStruct((m, d), F32)),
        grid=(batch, tpb),
        in_specs=[pl.BlockSpec((None, ROW_TILE, d), lambda b, r: (b, jnp.minimum(r, tpb - 2), 0)),
                  pl.BlockSpec((None, CTX_LEN, d), lambda b, r: (b, 0, 0)),
                  pl.BlockSpec((1, d), lambda b, r: (0, 0)),
                  tab_spec, tab_spec],
        out_specs=(row_spec, row_spec),
        compiler_params=_cparams(("parallel", "parallel")),
        name="normmod_first",
    )(x, ctx, gain.reshape(1, d), sc_t, sh_t)


def _matmul_kernel(a_ref, w_ref, o_ref):
    o_ref[...] = jnp.dot(a_ref[...], w_ref[...], preferred_element_type=F32).astype(o_ref.dtype)


def _pick_tile(n, candidates):
    for t in candidates:
        if n % t == 0:
            return t
    raise ValueError(f"no tile for {n}")


def _matmul(a, w, layer, out_dtype, tn, name):
    m, k = a.shape
    n = w.shape[2]
    tm = _pick_tile(m, (1024, 512, 256))
    return pl.pallas_call(
        _matmul_kernel,
        out_shape=jax.ShapeDtypeStruct((m, n), out_dtype),
        grid=(m // tm, n // tn),
        in_specs=[pl.BlockSpec((tm, k), lambda i, j: (i, 0)),
                  pl.BlockSpec((None, k, tn), lambda i, j: (layer, 0, j))],
        out_specs=pl.BlockSpec((tm, tn), lambda i, j: (i, j)),
        compiler_params=_cparams(("parallel", "parallel")),
        name=name,
    )(a, w)


def _qkvprep_kernel(q_ref, t_ref, cos_ref, sin_ref, qg_ref, kg_ref, q_out, k_out, vt_out, *, tiles_per_batch):
    is_lat = pl.program_id(1) < tiles_per_batch - 1
    cos, sin, kg = cos_ref[...], sin_ref[...], kg_ref[...]
    qg = qg_ref[...] * (ATT_SCALE * LOG2_E)
    hd = ATT_HEAD_DIM
    n_q = ATT_KV_HEADS * ATT_GROUP
    sls = [slice(h * hd, (h + 1) * hd) for h in range(n_q)]
    xs = [q_ref[:, sl].astype(F32) for sl in sls] + [t_ref[:, sl] for sl in sls[:ATT_KV_HEADS]]
    gains = [qg] * n_q + [kg] * ATT_KV_HEADS
    xs = [x * lax.rsqrt(jnp.mean(x * x, axis=-1, keepdims=True) + EPS) * g for x, g in zip(xs, gains)]
    lane = lax.broadcasted_iota(jnp.int32, xs[0].shape, 1)
    low = (lane % 64) < 32
    up = [pltpu.roll(x, 96, 1) for x in xs]
    down = [pltpu.roll(x, 32, 1) for x in xs]
    outs = [jnp.where(is_lat, x * cos + jnp.where(low, a, b) * sin, x).astype(BF16)
            for x, a, b in zip(xs, up, down)]
    for h in range(n_q):
        q_out[:, sls[h]] = outs[h]
    for h in range(ATT_KV_HEADS):
        k_out[:, sls[h]] = outs[n_q + h]
        vt_out[h, 0:hd, :] = t_ref[:, sls[ATT_KV_HEADS + h]].T.astype(BF16)
        vt_out[h, hd:, :] = jnp.ones((VT_ROWS - hd, ROW_TILE), BF16)


def _qkvprep(ya, tail, cos_t, sin_t, q_gain, k_gain, batch, tiles_per_batch):
    m = tail.shape[0]
    lat_tiles = tiles_per_batch - 1
    kvw = ATT_KV_HEADS * ATT_HEAD_DIM
    rope_spec = pl.BlockSpec((ROW_TILE, ATT_HEAD_DIM), lambda b, r: (jnp.minimum(r, lat_tiles - 1), 0))
    gain_spec = pl.BlockSpec((1, ATT_HEAD_DIM), lambda b, r: (0, 0))
    return pl.pallas_call(
        functools.partial(_qkvprep_kernel, tiles_per_batch=tiles_per_batch),
        out_shape=(jax.ShapeDtypeStruct((m, BRANCH_W), BF16),
                   jax.ShapeDtypeStruct((m, kvw), BF16),
                   jax.ShapeDtypeStruct((batch, ATT_KV_HEADS, VT_ROWS, tiles_per_batch * ROW_TILE), BF16)),
        grid=(batch, tiles_per_batch),
        in_specs=[pl.BlockSpec((ROW_TILE, BRANCH_W), lambda b, r: (b * tiles_per_batch + r, A_AQ)),
                  pl.BlockSpec((ROW_TILE, 2 * kvw), lambda b, r: (b * tiles_per_batch + r, 0)),
                  rope_spec, rope_spec, gain_spec, gain_spec],
        out_specs=(pl.BlockSpec((ROW_TILE, BRANCH_W), lambda b, r: (b * tiles_per_batch + r, 0)),
                   pl.BlockSpec((ROW_TILE, kvw), lambda b, r: (b * tiles_per_batch + r, 0)),
                   pl.BlockSpec((None, ATT_KV_HEADS, VT_ROWS, ROW_TILE), lambda b, r: (b, 0, 0, r))),
        compiler_params=_cparams(("parallel", "parallel")),
        name="qkvprep",
    )(ya, tail, cos_t, sin_t, q_gain.reshape(1, ATT_HEAD_DIM), k_gain.reshape(1, ATT_HEAD_DIM))


def _attn_kernel(q_ref, z_ref, k_ref, vt_ref, o_ref, *, n_lat_q, t_lat, kc):
    is_lat = pl.program_id(2) < n_lat_q
    hd = ATT_HEAD_DIM
    q = jnp.concatenate([q_ref[:, g * hd:(g + 1) * hd] for g in range(ATT_GROUP)], axis=0)

    def attend(chunks):
        def scores(c):
            off, n = chunks[c]
            return lax.dot_general(k_ref[off:off + n, :], q, (((1,), (1,)), ((), ())),
                                   preferred_element_type=F32)

        m, o = None, None
        s_next = scores(0)
        for c, (off, n) in enumerate(chunks):
            s = s_next
            if c + 1 < len(chunks):
                s_next = scores(c + 1)
            mc = jnp.max(s, axis=0, keepdims=True)
            m_new = mc if m is None else jnp.maximum(m, mc)
            p = jnp.exp2((s - m_new).astype(BF16))
            oc = jnp.dot(vt_ref[:, off:off + n], p, preferred_element_type=F32)
            o = oc if o is None else jnp.exp2(m - m_new) * o + oc
            m = m_new
        o = o[0:hd] / o[hd:hd + 1]
        for g in range(ATT_GROUP):
            sl = slice(g * ATT_HEAD_DIM, (g + 1) * ATT_HEAD_DIM)
            og = o[:, g * ATT_Q_TILE:(g + 1) * ATT_Q_TILE].T
            o_ref[:, sl] = (og * _silu(z_ref[:, sl].astype(F32))).astype(BF16)

    ctx_chunk = (t_lat, CTX_LEN)

    @pl.when(is_lat)
    def _():
        attend([(off, kc) for off in range(0, t_lat, kc)] + [ctx_chunk])

    @pl.when(jnp.logical_not(is_lat))
    def _():
        attend([ctx_chunk])


def _attention(qb, ya, kb, vt, batch, s_len):
    m = ya.shape[0]
    t_lat = s_len - CTX_LEN
    nq = s_len // ATT_Q_TILE
    n_lat_q = t_lat // ATT_Q_TILE
    gw = ATT_GROUP * ATT_HEAD_DIM
    return pl.pallas_call(
        functools.partial(_attn_kernel, n_lat_q=n_lat_q, t_lat=t_lat,
                          kc=_pick_tile(t_lat, (ATT_KEY_CHUNK, 512, 256))),
        out_shape=jax.ShapeDtypeStruct((m, BRANCH_W), BF16),
        grid=(batch, ATT_KV_HEADS, nq),
        in_specs=[pl.BlockSpec((ATT_Q_TILE, gw), lambda b, h, i: (b * nq + i, h)),
                  pl.BlockSpec((ATT_Q_TILE, gw), lambda b, h, i: (b * nq + i, A_AZ * 2 + h)),
                  pl.BlockSpec((s_len, ATT_HEAD_DIM), lambda b, h, i: (b, h)),
                  pl.BlockSpec((None, None, VT_ROWS, s_len), lambda b, h, i: (b, h, 0, 0))],
        out_specs=pl.BlockSpec((ATT_Q_TILE, gw), lambda b, h, i: (b * nq + i, h)),
        compiler_params=_cparams(("parallel", "parallel", "arbitrary")),
        name="attention",
    )(qb, ya, kb, vt)


def _gateprep_kernel(g_ref, bias_ref, g_out, gt_out, pre_out, pret_out, suf_out, suft_out):
    L = ROW_TILE
    gates = g_ref[...] + bias_ref[...]
    log_f = jnp.minimum(gates, 0.0) - jnp.log1p(jnp.exp(-jnp.abs(gates)))
    row = lax.broadcasted_iota(jnp.int32, (L, L), 0)
    col = lax.broadcasted_iota(jnp.int32, (L, L), 1)
    hi = lax.Precision.HIGHEST
    pre = jnp.dot((col <= row).astype(F32), log_f, precision=hi, preferred_element_type=F32)
    suf = jnp.dot((col >= row).astype(F32), log_f, precision=hi, preferred_element_type=F32)
    g_out[...] = gates
    gt_out[...] = gates.T
    pre_out[...] = pre
    pret_out[...] = pre.T
    suf_out[...] = suf
    suft_out[...] = suf.T


def _gateprep(tail, bias):
    m = tail.shape[0]
    nt = m // ROW_TILE
    row_sd, col_sd = jax.ShapeDtypeStruct((m, 128), F32), jax.ShapeDtypeStruct((128, m), F32)
    row_spec = pl.BlockSpec((ROW_TILE, 128), lambda i: (i, 0))
    col_spec = pl.BlockSpec((128, ROW_TILE), lambda i: (0, i))
    return pl.pallas_call(
        _gateprep_kernel,
        out_shape=(row_sd, col_sd, row_sd, col_sd, row_sd, col_sd),
        grid=(nt,),
        in_specs=[pl.BlockSpec((ROW_TILE, 128), lambda i: (i, (N_TAIL - 128) // 128)),
                  pl.BlockSpec((1, 128), lambda i: (0, 0))],
        out_specs=(row_spec, col_spec, row_spec, col_spec, row_spec, col_spec),
        compiler_params=_cparams(("parallel",)),
        name="gateprep",
    )(tail, bias)


def _mlstm_kernel(*refs, reverse, final):
    if final:
        (q_ref, k_ref, v_ref, g_ref, gt_ref, bc_ref, bct_ref, o_ref, z_ref, gain_ref, hprev_ref,
         out_ref, c_s, n_s, m_s) = refs
    else:
        q_ref, k_ref, v_ref, g_ref, gt_ref, bc_ref, bct_ref, out_ref, c_s, n_s, m_s = refs
    L = ROW_TILE
    hd = MLSTM_HEAD_DIM

    @pl.when(pl.program_id(1) == 0)
    def _():
        c_s[...] = jnp.zeros(c_s.shape, F32)
        n_s[...] = jnp.zeros(n_s.shape, F32)
        m_s[...] = jnp.full(m_s.shape, M_INIT, F32)

    gates, gates_t = g_ref[...], gt_ref[...]
    bcum, bcum_t = bc_ref[...], bct_ref[...]
    row = lax.broadcasted_iota(jnp.int32, (L, L), 0)
    col = lax.broadcasted_iota(jnp.int32, (L, L), 1)
    tri = (col >= row) if reverse else (col <= row)
    last = 0 if reverse else L - 1
    heads = range(MLSTM_HEADS)
    hsl = [slice(h * hd, (h + 1) * hd) for h in heads]
    icol = [(2 if reverse else 0) * MLSTM_HEADS + h for h in heads]
    fcol = [(3 if reverse else 1) * MLSTM_HEADS + h for h in heads]
    c_old = [c_s[h] for h in heads]
    n_old = [n_s[h, 0:1, :] for h in heads]
    m_old = [m_s[h, 0:1, 0:1] for h in heads]
    q = [q_ref[:, hsl[h]] for h in heads]
    k = [k_ref[:, hsl[h]].astype(F32) * MLSTM_K_SCALE for h in heads]
    v = [v_ref[:, hsl[h]] for h in heads]
    b_col = [bcum[:, fcol[h]:fcol[h] + 1] for h in heads]
    i_col = [gates[:, icol[h]:icol[h] + 1] for h in heads]

    qk = [lax.dot_general(q[h], k[h].astype(BF16), (((1,), (1,)), ((), ())), preferred_element_type=F32)
          for h in heads]
    qc = [jnp.dot(q[h], c_old[h].astype(BF16), preferred_element_type=F32) for h in heads]

    m_t, dmat, inter = [], [], []
    for h in heads:
        b_row, i_row = bcum_t[fcol[h]:fcol[h] + 1, :], gates_t[icol[h]:icol[h] + 1, :]
        log_d = jnp.where(tri, b_col[h] - b_row + i_row, -jnp.inf)
        log_inter = b_col[h] + m_old[h]
        m_t.append(jnp.maximum(log_inter, jnp.max(log_d, axis=1, keepdims=True)))
        dmat.append(jnp.exp(log_d - m_t[h]))
        inter.append(jnp.exp(log_inter - m_t[h]))

    sm = [qk[h] * dmat[h] for h in heads]
    sv = [jnp.dot(sm[h].astype(BF16), v[h], preferred_element_type=F32) for h in heads]

    kw, decay, m_new = [], [], []
    for h in heads:
        b_last = bcum[last:last + 1, fcol[h]:fcol[h] + 1]
        log_w = b_last - b_col[h] + i_col[h]
        m_new.append(jnp.maximum(b_last + m_old[h], jnp.max(log_w, axis=0, keepdims=True)))
        decay.append(jnp.exp(b_last + m_old[h] - m_new[h]))
        kw.append(k[h] * jnp.exp(log_w - m_new[h]))
    kv = [lax.dot_general(kw[h].astype(BF16), v[h], (((0,), (0,)), ((), ())), preferred_element_type=F32)
          for h in heads]

    for h in heads:
        num = inter[h] * qc[h] + sv[h]
        den = inter[h] * jnp.sum(q[h].astype(F32) * n_old[h], axis=1, keepdims=True) \
            + jnp.sum(sm[h], axis=1, keepdims=True)
        hh = num / jnp.maximum(jnp.abs(den), jnp.exp(-m_t[h]))
        if final:
            ht = hh + hprev_ref[:, hsl[h]]
            hn = ht * lax.rsqrt(jnp.mean(ht * ht, axis=-1, keepdims=True) + EPS)
            y = hn * gain_ref[:, hsl[h]] * jax.nn.sigmoid(o_ref[:, hsl[h]].astype(F32)) \
                * _silu(z_ref[:, hsl[h]].astype(F32))
            out_ref[:, hsl[h]] = y.astype(BF16)
        else:
            out_ref[:, hsl[h]] = hh

    for h in heads:
        c_s[h] = decay[h] * c_old[h] + kv[h]
        n_s[h, 0:1, :] = decay[h] * n_old[h] + jnp.sum(kw[h], axis=0, keepdims=True)
        m_s[h] = jnp.broadcast_to(m_new[h], m_s.shape[1:])


def _mlstm_pass(ya, gate_arrays, gain, hprev, batch, tiles_per_batch, reverse):
    m = ya.shape[0]
    lat = tiles_per_batch - 1
    final = hprev is not None

    def rows(b, c):
        r = jnp.where(c == 0, lat, (lat - c) if reverse else (c - 1))
        return b * tiles_per_batch + r

    def col(cidx):
        return pl.BlockSpec((ROW_TILE, BRANCH_W), lambda b, c: (rows(b, c), cidx))

    gate_spec = pl.BlockSpec((ROW_TILE, 128), lambda b, c: (rows(b, c), 0))
    gate_t_spec = pl.BlockSpec((128, ROW_TILE), lambda b, c: (0, rows(b, c)))
    in_specs = [col(A_MQ), col(A_MK), col(A_MV), gate_spec, gate_t_spec, gate_spec, gate_t_spec]
    args = [ya, ya, ya, *gate_arrays]
    if final:
        in_specs += [col(A_MO), col(A_MZ), pl.BlockSpec((1, BRANCH_W), lambda b, c: (0, 0)),
                     pl.BlockSpec((ROW_TILE, BRANCH_W), lambda b, c: (rows(b, c), 0))]
        args += [ya, ya, gain.reshape(1, BRANCH_W), hprev]
    return pl.pallas_call(
        functools.partial(_mlstm_kernel, reverse=reverse, final=final),
        out_shape=jax.ShapeDtypeStruct((m, BRANCH_W), BF16 if final else F32),
        grid=(batch, tiles_per_batch),
        in_specs=in_specs,
        out_specs=pl.BlockSpec((ROW_TILE, BRANCH_W), lambda b, c: (rows(b, c), 0)),
        scratch_shapes=[pltpu.VMEM((MLSTM_HEADS, MLSTM_HEAD_DIM, MLSTM_HEAD_DIM), F32),
                        pltpu.VMEM((MLSTM_HEADS, 8, MLSTM_HEAD_DIM), F32),
                        pltpu.VMEM((MLSTM_HEADS, 8, 128), F32)],
        compiler_params=_cparams(("parallel", "arbitrary")),
        name="mlstm_bwd" if reverse else "mlstm_fwd",
    )(*args)


def _local_kernel(cu_ref, cb_ref, cc_ref, cz_ref, pu_ref, pz_ref,
                  cu_p, cc_p, pu_p, cu_n, cc_n, pu_n,
                  cw_ref, pw_ref, ps_ref, band_ref, yc_ref, yd_ref, *, tiles_per_batch):
    r = pl.program_id(0) % tiles_per_batch
    lat = tiles_per_batch - 1
    has_prev = jnp.logical_and(r != 0, r != lat)
    has_next = jnp.logical_and(r != lat - 1, r != lat)
    seg_len = jnp.where(r == lat, CTX_LEN, lat * ROW_TILE)
    t0 = jnp.where(r == lat, 0, r * ROW_TILE)
    rowi = lax.broadcasted_iota(jnp.int32, (ROW_TILE, 1), 0)

    for j in range(BRANCH_W // 128):
        cs = slice(j * 128, (j + 1) * 128)
        a = cc_ref[:, cs].astype(F32) * cu_ref[:, cs].astype(F32)
        a_prev = jnp.where(has_prev, cc_p[HALO - 1:HALO, cs].astype(F32) * cu_p[HALO - 1:HALO, cs].astype(F32), 0.0)
        a_next = jnp.where(has_next, cc_n[0:1, cs].astype(F32) * cu_n[0:1, cs].astype(F32), 0.0)
        a_m1 = jnp.where(rowi == 0, a_prev, pltpu.roll(a, 1, 0))
        a_p1 = jnp.where(rowi == ROW_TILE - 1, a_next, pltpu.roll(a, ROW_TILE - 1, 0))
        y = cw_ref[0:1, cs] * a_m1 + cw_ref[1:2, cs] * a + cw_ref[2:3, cs] * a_p1
        yc_ref[:, cs] = (cb_ref[:, cs].astype(F32) * y * _silu(cz_ref[:, cs].astype(F32))).astype(BF16)

    u = pu_ref[...]
    halo_zero = jnp.zeros((HALO, BRANCH_W), u.dtype)
    ext = jnp.concatenate([u, jnp.where(has_prev, pu_p[...], halo_zero), jnp.where(has_next, pu_n[...], halo_zero),
                           jnp.zeros((POOL_K - ROW_TILE - 2 * HALO, BRANCH_W), u.dtype)], axis=0)
    t = t0 + rowi
    groups = range(len(POOL_WINDOWS))
    gsl = [slice(g * POOL_GROUP, (g + 1) * POOL_GROUP) for g in groups]
    acc = [jnp.dot(band_ref[g], ext[:, gsl[g]], preferred_element_type=F32) for g in groups]
    dev = []
    for g, w in enumerate(POOL_WINDOWS):
        inv_cnt = 1.0 / (jnp.minimum(t + (w - w // 2), seg_len) - jnp.maximum(t - w // 2, 0)).astype(F32)
        dev.append((acc[g] * inv_cnt - u[:, gsl[g]].astype(F32)).astype(BF16))
    pg = [jnp.dot(dev[g], pw_ref[g], preferred_element_type=F32) for g in groups]
    for g in groups:
        gate = _silu(pz_ref[:, gsl[g]].astype(F32)) * ps_ref[:, gsl[g]]
        yd_ref[:, gsl[g]] = (pg[g] * gate).astype(BF16)


def _pool_band():
    t = np.arange(ROW_TILE)[:, None]
    pos = np.concatenate([np.arange(ROW_TILE), np.arange(-HALO, 0), np.arange(ROW_TILE, ROW_TILE + HALO),
                          np.full(POOL_K - ROW_TILE - 2 * HALO, -10 * ROW_TILE)])[None, :]
    return np.stack([(pos >= t - w // 2) & (pos < t + w - w // 2) for w in POOL_WINDOWS]).astype(np.float32)


def _local_mixers(yb, conv_w, pool_w, pool_scale, tiles_per_batch):
    m = yb.shape[0]
    nt = m // ROW_TILE
    hb = ROW_TILE // HALO
    n_halo = m // HALO

    def col(cidx):
        return pl.BlockSpec((ROW_TILE, BRANCH_W), lambda i: (i, cidx))

    def prev(cidx):
        return pl.BlockSpec((HALO, BRANCH_W), lambda i: (jnp.maximum(i * hb - 1, 0), cidx))

    def nxt(cidx):
        return pl.BlockSpec((HALO, BRANCH_W), lambda i: (jnp.minimum((i + 1) * hb, n_halo - 1), cidx))

    out_sd = jax.ShapeDtypeStruct((m, BRANCH_W), BF16)
    return pl.pallas_call(
        functools.partial(_local_kernel, tiles_per_batch=tiles_per_batch),
        out_shape=(out_sd, out_sd),
        grid=(nt,),
        in_specs=[col(B_CU), col(B_CB), col(B_CC), col(B_CZ), col(B_PU), col(B_PZ),
                  prev(B_CU), prev(B_CC), prev(B_PU), nxt(B_CU), nxt(B_CC), nxt(B_PU),
                  pl.BlockSpec((3, BRANCH_W), lambda i: (0, 0)),
                  pl.BlockSpec((len(POOL_WINDOWS), POOL_GROUP, POOL_GROUP), lambda i: (0, 0, 0)),
                  pl.BlockSpec((1, BRANCH_W), lambda i: (0, 0)),
                  pl.BlockSpec((len(POOL_WINDOWS), ROW_TILE, POOL_K), lambda i: (0, 0, 0))],
        out_specs=(pl.BlockSpec((ROW_TILE, BRANCH_W), lambda i: (i, 0)),
                   pl.BlockSpec((ROW_TILE, BRANCH_W), lambda i: (i, 0))),
        compiler_params=_cparams(("parallel",)),
        name="local_mixers",
    )(*([yb] * 12), conv_w, pool_w, pool_scale.reshape(1, BRANCH_W), jnp.asarray(_pool_band(), BF16))


def _mergeout_kernel(*refs, final):
    if final:
        b0, b1, b2, b3, g_ref, wb_ref, wo_ref, x_ref, gt_ref, fg_ref, o_ref = refs
    else:
        b0, b1, b2, b3, g_ref, wb_ref, wo_ref, x_ref, gt_ref, o_ref = refs
    acc = None
    for j, br in enumerate((b0, b1, b2, b3)):
        gate = jax.nn.sigmoid(g_ref[:, j * D_MODEL:(j + 1) * D_MODEL].astype(F32))
        term = gate * jnp.dot(br[...], wb_ref[j], preferred_element_type=F32)
        acc = term if acc is None else acc + term
    y = x_ref[...] + gt_ref[0] * jnp.dot(acc.astype(BF16), wo_ref[...], preferred_element_type=F32)
    if final:
        y = y * lax.rsqrt(jnp.mean(y * y, axis=-1, keepdims=True) + EPS) * fg_ref[...]
    o_ref[...] = y


def _mergeout(branches, yb, w_branch, w_out, layer, xs, gt_t, final_gain, batch, tiles_per_batch):
    m, d = xs.shape
    final = final_gain is not None
    lat = tiles_per_batch - 1
    n_r = lat if final else tiles_per_batch
    resident = pl.Buffered(1)

    def src(b, r):
        return b * tiles_per_batch + r

    br_spec = pl.BlockSpec((ROW_TILE, BRANCH_W), lambda b, r: (src(b, r), 0))
    in_specs = [br_spec, br_spec, br_spec, br_spec,
                pl.BlockSpec((ROW_TILE, N_BRANCH * d), lambda b, r: (src(b, r), B_GATE)),
                pl.BlockSpec((None, N_BRANCH, BRANCH_W, d), lambda b, r: (layer, 0, 0, 0), pipeline_mode=resident),
                pl.BlockSpec((None, d, d), lambda b, r: (layer, 0, 0), pipeline_mode=resident),
                pl.BlockSpec((ROW_TILE, d), lambda b, r: (src(b, r), 0)),
                pl.BlockSpec((1, 1, d), lambda b, r: (src(b, r), 0, 0))]
    args = [*branches, yb, w_branch, w_out, xs, gt_t]
    if final:
        in_specs.append(pl.BlockSpec((1, d), lambda b, r: (0, 0)))
        args.append(final_gain.reshape(1, d))
    return pl.pallas_call(
        functools.partial(_mergeout_kernel, final=final),
        out_shape=jax.ShapeDtypeStruct((batch * n_r * ROW_TILE, d), F32),
        grid=(batch, n_r),
        in_specs=in_specs,
        out_specs=pl.BlockSpec((ROW_TILE, d), lambda b, r: (b * n_r + r, 0)),
        compiler_params=_cparams(("parallel", "parallel")),
        name="mergeout_final" if final else "mergeout",
    )(*args)


def _rope_tables(t_lat):
    pos = np.arange(t_lat)
    quarter = ATT_HEAD_DIM // 4
    freq = ROPE_THETA ** (-jnp.arange(quarter, dtype=F32) / quarter)
    a_row = jnp.asarray(pos // GRID_W, F32)[:, None] * freq[None, :]
    a_col = jnp.asarray(pos % GRID_W, F32)[:, None] * freq[None, :]
    cos_t = jnp.concatenate([jnp.cos(a_row), jnp.cos(a_row), jnp.cos(a_col), jnp.cos(a_col)], axis=-1)
    sin_t = jnp.concatenate([-jnp.sin(a_row), jnp.sin(a_row), -jnp.sin(a_col), jnp.sin(a_col)], axis=-1)
    return cos_t, sin_t


def _wcast_kernel(w_ref, o_ref):
    o_ref[...] = w_ref[...].astype(BF16)


def _wshift_kernel(a0_ref, a1_ref, b_ref, o_ref):
    x = jnp.concatenate([a0_ref[...], a1_ref[...], b_ref[...]], axis=1)
    n = x.shape[1]
    o_ref[...] = pltpu.roll(x, n - W_IN_LANE_SHIFT, 1)[:, :o_ref.shape[1]].astype(BF16)


def _wtail_kernel(kv_ref, g_ref, o_ref):
    kvw = kv_ref.shape[1]
    o_ref[:, :kvw] = kv_ref[...].astype(BF16)
    lane = lax.broadcasted_iota(jnp.int32, g_ref.shape, 1)
    o_ref[:, kvw:] = jnp.where(lane < 4 * MLSTM_HEADS, g_ref[...], 0.0).astype(BF16)


W_IN_LANE_SHIFT = 16
W_ROWS = 512


def _split_w_in(w):
    depth, d, _ = w.shape
    nrb = d // W_ROWS
    cp = _cparams(("parallel", "parallel", "parallel"))
    w_a = pl.pallas_call(
        _wcast_kernel,
        out_shape=jax.ShapeDtypeStruct((depth, d, 7 * 1024), BF16),
        grid=(depth, nrb, 14),
        in_specs=[pl.BlockSpec((None, W_ROWS, 512), lambda l, r, j: (l, r, jnp.where(j < 2, j, j + 1)))],
        out_specs=pl.BlockSpec((None, W_ROWS, 512), lambda l, r, j: (l, r, j)),
        compiler_params=cp, name="wprep_a",
    )(w)

    def unit(j):
        return jnp.where(j < 8, 108 + 8 * j, 60 + 8 * (j - 8))

    w_b = pl.pallas_call(
        _wshift_kernel,
        out_shape=jax.ShapeDtypeStruct((depth, d, 14 * 1024), BF16),
        grid=(depth, nrb, 14),
        in_specs=[pl.BlockSpec((None, W_ROWS, 512), lambda l, r, j: (l, r, unit(j) // 4)),
                  pl.BlockSpec((None, W_ROWS, 512), lambda l, r, j: (l, r, unit(j) // 4 + 1)),
                  pl.BlockSpec((None, W_ROWS, 128), lambda l, r, j: (l, r, unit(j) + 8))],
        out_specs=pl.BlockSpec((None, W_ROWS, 1024), lambda l, r, j: (l, r, j)),
        compiler_params=cp, name="wprep_b",
    )(w, w, w)

    w_t = pl.pallas_call(
        _wtail_kernel,
        out_shape=jax.ShapeDtypeStruct((depth, d, N_TAIL), BF16),
        grid=(depth, nrb),
        in_specs=[pl.BlockSpec((None, W_ROWS, 512), lambda l, r: (l, r, 2)),
                  pl.BlockSpec((None, W_ROWS, 128), lambda l, r: (l, r, 60))],
        out_specs=pl.BlockSpec((None, W_ROWS, N_TAIL), lambda l, r: (l, r, 0)),
        compiler_params=_cparams(("parallel", "parallel")), name="wprep_tail",
    )(w, w)
    return w_a, w_b, w_t


def kernel(x, c, ctx, c_ctx, norm_gain, w_mod, b_mod, w_in, q_norm_gain, k_norm_gain, mlstm_gate_bias,
           mlstm_norm_gain, conv_w, pool_w, pool_scale, w_branch, w_out, final_norm_gain):
    batch, t_lat, d = x.shape
    depth = w_in.shape[0]
    assert d == D_MODEL and ctx.shape[1] == CTX_LEN and t_lat % ROW_TILE == 0 and batch < 8
    s_len = t_lat + CTX_LEN
    tiles_per_batch = s_len // ROW_TILE

    cc = jnp.zeros((8, d), F32).at[:batch].set(c).at[batch].set(c_ctx)
    mod = _modulation(cc, w_mod, b_mod)
    tile_row = np.array([b if r < tiles_per_batch - 1 else batch
                         for b in range(batch) for r in range(tiles_per_batch)], np.int32)
    cos_t, sin_t = _rope_tables(t_lat)
    w_a, w_b, w_t = _split_w_in(w_in)
    w_branch_b, w_out_b, pool_w_b = w_branch.astype(BF16), w_out.astype(BF16), pool_w.astype(BF16)

    xs, out = None, None
    for l in range(depth):
        mod_t = mod[l][tile_row][:, None, :]
        sh_t, sc_t, gt_t = mod_t[..., :d], mod_t[..., d:2 * d], mod_t[..., 2 * d:]
        bias = jnp.zeros((1, 128), F32).at[0, :4 * MLSTM_HEADS].set(mlstm_gate_bias[l].reshape(-1))

        if l == 0:
            h, xs = _normmod_first(x, ctx, norm_gain[l], sc_t, sh_t)
        else:
            h = _normmod(xs, norm_gain[l], sc_t, sh_t)
        ya = _matmul(h, w_a, l, BF16, 1024, "inproj_a")
        yb = _matmul(h, w_b, l, BF16, 1024, "inproj_b")
        tail = _matmul(h, w_t, l, F32, N_TAIL, "inproj_tail")

        qb, kb, vt = _qkvprep(ya, tail, cos_t, sin_t, q_norm_gain[l], k_norm_gain[l], batch, tiles_per_batch)
        y_att = _attention(qb, ya, kb, vt, batch, s_len)
        g, g_t, pre, pre_t, suf, suf_t = _gateprep(tail, bias)
        h_f = _mlstm_pass(ya, (g, g_t, pre, pre_t), None, None, batch, tiles_per_batch, reverse=False)
        y_mls = _mlstm_pass(ya, (g, g_t, suf, suf_t), mlstm_norm_gain[l], h_f, batch, tiles_per_batch,
                            reverse=True)
        y_conv, y_pool = _local_mixers(yb, conv_w[l], pool_w_b[l], pool_scale[l], tiles_per_batch)

        last = l == depth - 1
        res = _mergeout((y_att, y_mls, y_conv, y_pool), yb, w_branch_b, w_out_b, l, xs, gt_t,
                        final_norm_gain if last else None, batch, tiles_per_batch)
        if last:
            out = res.reshape(batch, t_lat, d)
        else:
            xs = res
    return out
```

```python
import functools

import numpy as np
import jax
import jax.numpy as jnp
from jax import lax
from jax.experimental import pallas as pl
from jax.experimental.pallas import tpu as pltpu

F32 = jnp.float32
BF16 = jnp.bfloat16

D_MODEL = 2048
BRANCH_W = 1024
GRID_W = 64
CTX_LEN = 256
EPS = 1e-6
ATT_HEAD_DIM = 128
ATT_GROUP = 4
ATT_KV_HEADS = 2
ATT_SCALE = ATT_HEAD_DIM ** -0.5
LOG2_E = 1.4426950408889634
ROPE_THETA = 10000.0
MLSTM_HEADS = 4
MLSTM_HEAD_DIM = 256
MLSTM_K_SCALE = MLSTM_HEAD_DIM ** -0.5
M_INIT = -1e30
POOL_WINDOWS = (2, 4, 8, 16)
POOL_GROUP = 256
N_BRANCH = 4

ROW_TILE = 256
HALO = 16
POOL_K = 384
ATT_Q_TILE = 128
ATT_KEY_CHUNK = 1024
VT_ROWS = ATT_HEAD_DIM + 16
V7X_VMEM_LIMIT = 56 * 1024 * 1024

A_AQ, A_AZ, A_MQ, A_MK, A_MV, A_MO, A_MZ = range(7)
B_GATE = 0
B_CU, B_CB, B_CC, B_CZ, B_PU, B_PZ = range(8, 14)
N_TAIL = 640


def _cparams(sem, vmem=V7X_VMEM_LIMIT):
    return pltpu.CompilerParams(dimension_semantics=sem, vmem_limit_bytes=vmem)


def _silu(x):
    return x * jax.nn.sigmoid(x)


def _mod_kernel(c_ref, w_ref, b_ref, o_ref):
    a = _silu(c_ref[...]).astype(BF16)
    o_ref[...] = jnp.dot(a, w_ref[...].astype(BF16), preferred_element_type=F32) + b_ref[...]


def _modulation(cc, w_mod, b_mod):
    depth, d, n = w_mod.shape
    tn = 768
    return pl.pallas_call(
        _mod_kernel,
        out_shape=jax.ShapeDtypeStruct((depth, 8, n), F32),
        grid=(depth, n // tn),
        in_specs=[pl.BlockSpec((8, d), lambda l, j: (0, 0)),
                  pl.BlockSpec((None, d, tn), lambda l, j: (l, 0, j)),
                  pl.BlockSpec((None, 1, tn), lambda l, j: (l, 0, j))],
        out_specs=pl.BlockSpec((None, 8, tn), lambda l, j: (l, 0, j)),
        compiler_params=_cparams(("parallel", "parallel")),
        name="modulation",
    )(cc, w_mod, b_mod.reshape(depth, 1, n))


def _normmod_kernel(x_ref, g_ref, sc_ref, sh_ref, o_ref):
    x = x_ref[...]
    y = x * lax.rsqrt(jnp.mean(x * x, axis=-1, keepdims=True) + EPS) * g_ref[...]
    o_ref[...] = (y * (1.0 + sc_ref[0]) + sh_ref[0]).astype(BF16)


def _normmod(xs, gain, sc_t, sh_t):
    m, d = xs.shape
    nt = m // ROW_TILE
    return pl.pallas_call(
        _normmod_kernel,
        out_shape=jax.ShapeDtypeStruct((m, d), BF16),
        grid=(nt,),
        in_specs=[pl.BlockSpec((ROW_TILE, d), lambda i: (i, 0)),
                  pl.BlockSpec((1, d), lambda i: (0, 0)),
                  pl.BlockSpec((1, 1, d), lambda i: (i, 0, 0)),
                  pl.BlockSpec((1, 1, d), lambda i: (i, 0, 0))],
        out_specs=pl.BlockSpec((ROW_TILE, d), lambda i: (i, 0)),
        compiler_params=_cparams(("parallel",)),
        name="normmod",
    )(xs, gain.reshape(1, d), sc_t, sh_t)


def _normmod_first_kernel(x_ref, c_ref, g_ref, sc_ref, sh_ref, o_ref, xs_ref, *, tiles_per_batch):
    x = jnp.where(pl.program_id(1) == tiles_per_batch - 1, c_ref[...], x_ref[...])
    xs_ref[...] = x
    y = x * lax.rsqrt(jnp.mean(x * x, axis=-1, keepdims=True) + EPS) * g_ref[...]
    o_ref[...] = (y * (1.0 + sc_ref[0]) + sh_ref[0]).astype(BF16)


def _normmod_first(x, ctx, gain, sc_t, sh_t):
    batch, t_lat, d = x.shape
    tpb = t_lat // ROW_TILE + 1
    m = batch * tpb * ROW_TILE
    tab_spec = pl.BlockSpec((1, 1, d), lambda b, r: (b * tpb + r, 0, 0))
    row_spec = pl.BlockSpec((ROW_TILE, d), lambda b, r: (b * tpb + r, 0))
    return pl.pallas_call(
        functools.partial(_normmod_first_kernel, tiles_per_batch=tpb),
        out_shape=(jax.ShapeDtypeStruct((m, d), BF16), jax.ShapeDtypeStruct((m, d), F32)),
        grid=(batch, tpb),
        in_specs=[pl.BlockSpec((None, ROW_TILE, d), lambda b, r: (b, jnp.minimum(r, tpb - 2), 0)),
                  pl.BlockSpec((None, CTX_LEN, d), lambda b, r: (b, 0, 0)),
                  pl.BlockSpec((1, d), lambda b, r: (0, 0)),
                  tab_spec, tab_spec],
        out_specs=(row_spec, row_spec),
        compiler_params=_cparams(("parallel", "parallel")),
        name="normmod_first",
    )(x, ctx, gain.reshape(1, d), sc_t, sh_t)


IN_UNIT = 1024
W_KV_ROW, W_MGATE_ROW, W_LOCAL_ROW, W_MERGE_ROW = 1024, 7680, 7696, 13840
N_IN = W_MERGE_ROW + N_BRANCH * D_MODEL


def _pick_tile(n, candidates):
    for t in candidates:
        if n % t == 0:
            return t
    raise ValueError(f"no tile for {n}")


def _inproj_kernel(h_ref, w_ref, o_ref, wt_s):
    @pl.when(pl.program_id(1) == 0)
    def _():
        wt_s[...] = w_ref[...].T.astype(BF16)

    o_ref[...] = jnp.dot(h_ref[...], wt_s[...], preferred_element_type=F32).astype(o_ref.dtype)


def _inproj(h, w_t, layer, n_units, unit_row, name):
    m, k = h.shape
    tm = _pick_tile(m, (1024, 512, 256))
    base = layer * N_IN
    return pl.pallas_call(
        _inproj_kernel,
        out_shape=jax.ShapeDtypeStruct((m, n_units * IN_UNIT), BF16),
        grid=(n_units, m // tm),
        in_specs=[pl.BlockSpec((tm, k), lambda j, i: (i, 0)),
                  pl.BlockSpec((pl.Element(IN_UNIT), pl.Element(k)),
                               lambda j, i: (pl.multiple_of(base + unit_row(j), 8), 0))],
        out_specs=pl.BlockSpec((tm, IN_UNIT), lambda j, i: (i, j)),
        scratch_shapes=[pltpu.VMEM((k, IN_UNIT), BF16)],
        compiler_params=_cparams(("parallel", "arbitrary")),
        name=name,
    )(h, w_t)


def _inproj_tail_kernel(h_ref, wkv_ref, wg_ref, o_ref):
    nt = (((1,), (1,)), ((), ()))
    h = h_ref[...]
    kvw = wkv_ref.shape[0]
    o_ref[:, :kvw] = lax.dot_general(h, wkv_ref[...].astype(BF16), nt, preferred_element_type=F32)
    g = lax.dot_general(h, wg_ref[...].astype(BF16), nt, preferred_element_type=F32)
    lane = lax.broadcasted_iota(jnp.int32, g.shape, 1)
    o_ref[:, kvw:] = jnp.where(lane < 4 * MLSTM_HEADS, g, 0.0)


def _inproj_tail(h, w_t, layer):
    m, k = h.shape
    tm = _pick_tile(m, (1024, 512, 256))
    kvw = 2 * ATT_KV_HEADS * ATT_HEAD_DIM
    return pl.pallas_call(
        _inproj_tail_kernel,
        out_shape=jax.ShapeDtypeStruct((m, N_TAIL), F32),
        grid=(m // tm,),
        in_specs=[pl.BlockSpec((tm, k), lambda i: (i, 0)),
                  pl.BlockSpec((pl.Element(kvw), pl.Element(k)), lambda i: (layer * N_IN + W_KV_ROW, 0)),
                  pl.BlockSpec((pl.Element(128), pl.Element(k)), lambda i: (layer * N_IN + W_MGATE_ROW, 0))],
        out_specs=pl.BlockSpec((tm, N_TAIL), lambda i: (i, 0)),
        compiler_params=_cparams(("parallel",)),
        name="inproj_tail",
    )(h, w_t, w_t)


def _qkvprep_kernel(q_ref, t_ref, cos_ref, sin_ref, qg_ref, kg_ref, q_out, k_out, vt_out, *, tiles_per_batch):
    is_lat = pl.program_id(1) < tiles_per_batch - 1
    cos, sin, kg = cos_ref[...], sin_ref[...], kg_ref[...]
    qg = qg_ref[...] * (ATT_SCALE * LOG2_E)
    hd = ATT_HEAD_DIM
    n_q = ATT_KV_HEADS * ATT_GROUP
    sls = [slice(h * hd, (h + 1) * hd) for h in range(n_q)]
    xs = [q_ref[:, sl].astype(F32) for sl in sls] + [t_ref[:, sl] for sl in sls[:ATT_KV_HEADS]]
    gains = [qg] * n_q + [kg] * ATT_KV_HEADS
    xs = [x * lax.rsqrt(jnp.mean(x * x, axis=-1, keepdims=True) + EPS) * g for x, g in zip(xs, gains)]
    lane = lax.broadcasted_iota(jnp.int32, xs[0].shape, 1)
    low = (lane % 64) < 32
    up = [pltpu.roll(x, 96, 1) for x in xs]
    down = [pltpu.roll(x, 32, 1) for x in xs]
    outs = [jnp.where(is_lat, x * cos + jnp.where(low, a, b) * sin, x).astype(BF16)
            for x, a, b in zip(xs, up, down)]
    for h in range(n_q):
        q_out[:, sls[h]] = outs[h]
    for h in range(ATT_KV_HEADS):
        k_out[:, sls[h]] = outs[n_q + h]
        vt_out[h, 0:hd, :] = t_ref[:, sls[ATT_KV_HEADS + h]].T.astype(BF16)
        vt_out[h, hd:, :] = jnp.ones((VT_ROWS - hd, ROW_TILE), BF16)


def _qkvprep(ya, tail, cos_t, sin_t, q_gain, k_gain, batch, tiles_per_batch):
    m = tail.shape[0]
    lat_tiles = tiles_per_batch - 1
    kvw = ATT_KV_HEADS * ATT_HEAD_DIM
    rope_spec = pl.BlockSpec((ROW_TILE, ATT_HEAD_DIM), lambda b, r: (jnp.minimum(r, lat_tiles - 1), 0))
    gain_spec = pl.BlockSpec((1, ATT_HEAD_DIM), lambda b, r: (0, 0))
    return pl.pallas_call(
        functools.partial(_qkvprep_kernel, tiles_per_batch=tiles_per_batch),
        out_shape=(jax.ShapeDtypeStruct((m, BRANCH_W), BF16),
                   jax.ShapeDtypeStruct((m, kvw), BF16),
                   jax.ShapeDtypeStruct((batch, ATT_KV_HEADS, VT_ROWS, tiles_per_batch * ROW_TILE), BF16)),
        grid=(batch, tiles_per_batch),
        in_specs=[pl.BlockSpec((ROW_TILE, BRANCH_W), lambda b, r: (b * tiles_per_batch + r, A_AQ)),
                  pl.BlockSpec((ROW_TILE, 2 * kvw), lambda b, r: (b * tiles_per_batch + r, 0)),
                  rope_spec, rope_spec, gain_spec, gain_spec],
        out_specs=(pl.BlockSpec((ROW_TILE, BRANCH_W), lambda b, r: (b * tiles_per_batch + r, 0)),
                   pl.BlockSpec((ROW_TILE, kvw), lambda b, r: (b * tiles_per_batch + r, 0)),
                   pl.BlockSpec((None, ATT_KV_HEADS, VT_ROWS, ROW_TILE), lambda b, r: (b, 0, 0, r))),
        compiler_params=_cparams(("parallel", "parallel")),
        name="qkvprep",
    )(ya, tail, cos_t, sin_t, q_gain.reshape(1, ATT_HEAD_DIM), k_gain.reshape(1, ATT_HEAD_DIM))


def _attn_kernel(q_ref, z_ref, k_ref, vt_ref, o_ref, *, n_lat_q, t_lat, kc):
    is_lat = pl.program_id(2) < n_lat_q
    hd = ATT_HEAD_DIM
    q = jnp.concatenate([q_ref[:, g * hd:(g + 1) * hd] for g in range(ATT_GROUP)], axis=0)

    def attend(chunks):
        def scores(c):
            off, n = chunks[c]
            return lax.dot_general(k_ref[off:off + n, :], q, (((1,), (1,)), ((), ())),
                                   preferred_element_type=F32)

        m, o = None, None
        s_next = scores(0)
        for c, (off, n) in enumerate(chunks):
            s = s_next
            if c + 1 < len(chunks):
                s_next = scores(c + 1)
            mc = jnp.max(s, axis=0, keepdims=True)
            m_new = mc if m is None else jnp.maximum(m, mc)
            p = jnp.exp2((s - m_new).astype(BF16))
            oc = jnp.dot(vt_ref[:, off:off + n], p, preferred_element_type=F32)
            o = oc if o is None else jnp.exp2(m - m_new) * o + oc
            m = m_new
        o = o[0:hd] / o[hd:hd + 1]
        for g in range(ATT_GROUP):
            sl = slice(g * ATT_HEAD_DIM, (g + 1) * ATT_HEAD_DIM)
            og = o[:, g * ATT_Q_TILE:(g + 1) * ATT_Q_TILE].T
            o_ref[:, sl] = (og * _silu(z_ref[:, sl].astype(F32))).astype(BF16)

    ctx_chunk = (t_lat, CTX_LEN)

    @pl.when(is_lat)
    def _():
        attend([(off, kc) for off in range(0, t_lat, kc)] + [ctx_chunk])

    @pl.when(jnp.logical_not(is_lat))
    def _():
        attend([ctx_chunk])


def _attention(qb, ya, kb, vt, batch, s_len):
    m = ya.shape[0]
    t_lat = s_len - CTX_LEN
    nq = s_len // ATT_Q_TILE
    n_lat_q = t_lat // ATT_Q_TILE
    gw = ATT_GROUP * ATT_HEAD_DIM
    return pl.pallas_call(
        functools.partial(_attn_kernel, n_lat_q=n_lat_q, t_lat=t_lat,
                          kc=_pick_tile(t_lat, (ATT_KEY_CHUNK, 512, 256))),
        out_shape=jax.ShapeDtypeStruct((m, BRANCH_W), BF16),
        grid=(batch, ATT_KV_HEADS, nq),
        in_specs=[pl.BlockSpec((ATT_Q_TILE, gw), lambda b, h, i: (b * nq + i, h)),
                  pl.BlockSpec((ATT_Q_TILE, gw), lambda b, h, i: (b * nq + i, A_AZ * 2 + h)),
                  pl.BlockSpec((s_len, ATT_HEAD_DIM), lambda b, h, i: (b, h)),
                  pl.BlockSpec((None, None, VT_ROWS, s_len), lambda b, h, i: (b, h, 0, 0))],
        out_specs=pl.BlockSpec((ATT_Q_TILE, gw), lambda b, h, i: (b * nq + i, h)),
        compiler_params=_cparams(("parallel", "parallel", "arbitrary")),
        name="attention",
    )(qb, ya, kb, vt)


def _gateprep_kernel(g_ref, bias_ref, g_out, gt_out, pre_out, pret_out, suf_out, suft_out):
    L = ROW_TILE
    gates = g_ref[...] + bias_ref[...]
    log_f = jnp.minimum(gates, 0.0) - jnp.log1p(jnp.exp(-jnp.abs(gates)))
    row = lax.broadcasted_iota(jnp.int32, (L, L), 0)
    col = lax.broadcasted_iota(jnp.int32, (L, L), 1)
    hi = lax.Precision.HIGHEST
    pre = jnp.dot((col <= row).astype(F32), log_f, precision=hi, preferred_element_type=F32)
    suf = jnp.dot((col >= row).astype(F32), log_f, precision=hi, preferred_element_type=F32)
    g_out[...] = gates
    gt_out[...] = gates.T
    pre_out[...] = pre
    pret_out[...] = pre.T
    suf_out[...] = suf
    suft_out[...] = suf.T


def _gateprep(tail, bias):
    m = tail.shape[0]
    nt = m // ROW_TILE
    row_sd, col_sd = jax.ShapeDtypeStruct((m, 128), F32), jax.ShapeDtypeStruct((128, m), F32)
    row_spec = pl.BlockSpec((ROW_TILE, 128), lambda i: (i, 0))
    col_spec = pl.BlockSpec((128, ROW_TILE), lambda i: (0, i))
    return pl.pallas_call(
        _gateprep_kernel,
        out_shape=(row_sd, col_sd, row_sd, col_sd, row_sd, col_sd),
        grid=(nt,),
        in_specs=[pl.BlockSpec((ROW_TILE, 128), lambda i: (i, (N_TAIL - 128) // 128)),
                  pl.BlockSpec((1, 128), lambda i: (0, 0))],
        out_specs=(row_spec, col_spec, row_spec, col_spec, row_spec, col_spec),
        compiler_params=_cparams(("parallel",)),
        name="gateprep",
    )(tail, bias)


def _mlstm_kernel(*refs, reverse, final):
    if final:
        (q_ref, k_ref, v_ref, g_ref, gt_ref, bc_ref, bct_ref, o_ref, z_ref, gain_ref, hprev_ref,
         out_ref, c_s, n_s, m_s) = refs
    else:
        q_ref, k_ref, v_ref, g_ref, gt_ref, bc_ref, bct_ref, out_ref, c_s, n_s, m_s = refs
    L = ROW_TILE
    hd = MLSTM_HEAD_DIM

    @pl.when(pl.program_id(1) == 0)
    def _():
        c_s[...] = jnp.zeros(c_s.shape, F32)
        n_s[...] = jnp.zeros(n_s.shape, F32)
        m_s[...] = jnp.full(m_s.shape, M_INIT, F32)

    gates, gates_t = g_ref[...], gt_ref[...]
    bcum, bcum_t = bc_ref[...], bct_ref[...]
    row = lax.broadcasted_iota(jnp.int32, (L, L), 0)
    col = lax.broadcasted_iota(jnp.int32, (L, L), 1)
    tri = (col >= row) if reverse else (col <= row)
    last = 0 if reverse else L - 1
    heads = range(MLSTM_HEADS)
    hsl = [slice(h * hd, (h + 1) * hd) for h in heads]
    icol = [(2 if reverse else 0) * MLSTM_HEADS + h for h in heads]
    fcol = [(3 if reverse else 1) * MLSTM_HEADS + h for h in heads]
    c_old = [c_s[h] for h in heads]
    n_old = [n_s[h, 0:1, :] for h in heads]
    m_old = [m_s[h, 0:1, 0:1] for h in heads]
    q = [q_ref[:, hsl[h]] for h in heads]
    k = [k_ref[:, hsl[h]].astype(F32) * MLSTM_K_SCALE for h in heads]
    v = [v_ref[:, hsl[h]] for h in heads]
    b_col = [bcum[:, fcol[h]:fcol[h] + 1] for h in heads]
    i_col = [gates[:, icol[h]:icol[h] + 1] for h in heads]

    qk = [lax.dot_general(q[h], k[h].astype(BF16), (((1,), (1,)), ((), ())), preferred_element_type=F32)
          for h in heads]
    qc = [jnp.dot(q[h], c_old[h].astype(BF16), preferred_element_type=F32) for h in heads]

    m_t, dmat, inter = [], [], []
    for h in heads:
        b_row, i_row = bcum_t[fcol[h]:fcol[h] + 1, :], gates_t[icol[h]:icol[h] + 1, :]
        log_d = jnp.where(tri, b_col[h] - b_row + i_row, -jnp.inf)
        log_inter = b_col[h] + m_old[h]
        m_t.append(jnp.maximum(log_inter, jnp.max(log_d, axis=1, keepdims=True)))
        dmat.append(jnp.exp(log_d - m_t[h]))
        inter.append(jnp.exp(log_inter - m_t[h]))

    sm = [qk[h] * dmat[h] for h in heads]
    sv = [jnp.dot(sm[h].astype(BF16), v[h], preferred_element_type=F32) for h in heads]

    kw, decay, m_new = [], [], []
    for h in heads:
        b_last = bcum[last:last + 1, fcol[h]:fcol[h] + 1]
        log_w = b_last - b_col[h] + i_col[h]
        m_new.append(jnp.maximum(b_last + m_old[h], jnp.max(log_w, axis=0, keepdims=True)))
        decay.append(jnp.exp(b_last + m_old[h] - m_new[h]))
        kw.append(k[h] * jnp.exp(log_w - m_new[h]))
    kv = [lax.dot_general(kw[h].astype(BF16), v[h], (((0,), (0,)), ((), ())), preferred_element_type=F32)
          for h in heads]

    for h in heads:
        num = inter[h] * qc[h] + sv[h]
        den = inter[h] * jnp.sum(q[h].astype(F32) * n_old[h], axis=1, keepdims=True) \
            + jnp.sum(sm[h], axis=1, keepdims=True)
        hh = num / jnp.maximum(jnp.abs(den), jnp.exp(-m_t[h]))
        if final:
            ht = hh + hprev_ref[:, hsl[h]]
            hn = ht * lax.rsqrt(jnp.mean(ht * ht, axis=-1, keepdims=True) + EPS)
            y = hn * gain_ref[:, hsl[h]] * jax.nn.sigmoid(o_ref[:, hsl[h]].astype(F32)) \
                * _silu(z_ref[:, hsl[h]].astype(F32))
            out_ref[:, hsl[h]] = y.astype(BF16)
        else:
            out_ref[:, hsl[h]] = hh

    for h in heads:
        c_s[h] = decay[h] * c_old[h] + kv[h]
        n_s[h, 0:1, :] = decay[h] * n_old[h] + jnp.sum(kw[h], axis=0, keepdims=True)
        m_s[h] = jnp.broadcast_to(m_new[h], m_s.shape[1:])


def _mlstm_pass(ya, gate_arrays, gain, hprev, batch, tiles_per_batch, reverse):
    m = ya.shape[0]
    lat = tiles_per_batch - 1
    final = hprev is not None

    def rows(b, c):
        r = jnp.where(c == 0, lat, (lat - c) if reverse else (c - 1))
        return b * tiles_per_batch + r

    def col(cidx):
        return pl.BlockSpec((ROW_TILE, BRANCH_W), lambda b, c: (rows(b, c), cidx))

    gate_spec = pl.BlockSpec((ROW_TILE, 128), lambda b, c: (rows(b, c), 0))
    gate_t_spec = pl.BlockSpec((128, ROW_TILE), lambda b, c: (0, rows(b, c)))
    in_specs = [col(A_MQ), col(A_MK), col(A_MV), gate_spec, gate_t_spec, gate_spec, gate_t_spec]
    args = [ya, ya, ya, *gate_arrays]
    if final:
        in_specs += [col(A_MO), col(A_MZ), pl.BlockSpec((1, BRANCH_W), lambda b, c: (0, 0)),
                     pl.BlockSpec((ROW_TILE, BRANCH_W), lambda b, c: (rows(b, c), 0))]
        args += [ya, ya, gain.reshape(1, BRANCH_W), hprev]
    return pl.pallas_call(
        functools.partial(_mlstm_kernel, reverse=reverse, final=final),
        out_shape=jax.ShapeDtypeStruct((m, BRANCH_W), BF16 if final else F32),
        grid=(batch, tiles_per_batch),
        in_specs=in_specs,
        out_specs=pl.BlockSpec((ROW_TILE, BRANCH_W), lambda b, c: (rows(b, c), 0)),
        scratch_shapes=[pltpu.VMEM((MLSTM_HEADS, MLSTM_HEAD_DIM, MLSTM_HEAD_DIM), F32),
                        pltpu.VMEM((MLSTM_HEADS, 8, MLSTM_HEAD_DIM), F32),
                        pltpu.VMEM((MLSTM_HEADS, 8, 128), F32)],
        compiler_params=_cparams(("parallel", "arbitrary")),
        name="mlstm_bwd" if reverse else "mlstm_fwd",
    )(*args)


def _local_kernel(cu_ref, cb_ref, cc_ref, cz_ref, pu_ref, pz_ref,
                  cu_p, cc_p, pu_p, cu_n, cc_n, pu_n,
                  cw_ref, pw_ref, ps_ref, band_ref, yc_ref, yd_ref, *, tiles_per_batch):
    r = pl.program_id(0) % tiles_per_batch
    lat = tiles_per_batch - 1
    has_prev = jnp.logical_and(r != 0, r != lat)
    has_next = jnp.logical_and(r != lat - 1, r != lat)
    seg_len = jnp.where(r == lat, CTX_LEN, lat * ROW_TILE)
    t0 = jnp.where(r == lat, 0, r * ROW_TILE)
    rowi = lax.broadcasted_iota(jnp.int32, (ROW_TILE, 1), 0)

    for j in range(BRANCH_W // 128):
        cs = slice(j * 128, (j + 1) * 128)
        a = cc_ref[:, cs].astype(F32) * cu_ref[:, cs].astype(F32)
        a_prev = jnp.where(has_prev, cc_p[HALO - 1:HALO, cs].astype(F32) * cu_p[HALO - 1:HALO, cs].astype(F32), 0.0)
        a_next = jnp.where(has_next, cc_n[0:1, cs].astype(F32) * cu_n[0:1, cs].astype(F32), 0.0)
        a_m1 = jnp.where(rowi == 0, a_prev, pltpu.roll(a, 1, 0))
        a_p1 = jnp.where(rowi == ROW_TILE - 1, a_next, pltpu.roll(a, ROW_TILE - 1, 0))
        y = cw_ref[0:1, cs] * a_m1 + cw_ref[1:2, cs] * a + cw_ref[2:3, cs] * a_p1
        yc_ref[:, cs] = (cb_ref[:, cs].astype(F32) * y * _silu(cz_ref[:, cs].astype(F32))).astype(BF16)

    u = pu_ref[...]
    halo_zero = jnp.zeros((HALO, BRANCH_W), u.dtype)
    ext = jnp.concatenate([u, jnp.where(has_prev, pu_p[...], halo_zero), jnp.where(has_next, pu_n[...], halo_zero),
                           jnp.zeros((POOL_K - ROW_TILE - 2 * HALO, BRANCH_W), u.dtype)], axis=0)
    t = t0 + rowi
    groups = range(len(POOL_WINDOWS))
    gsl = [slice(g * POOL_GROUP, (g + 1) * POOL_GROUP) for g in groups]
    acc = [jnp.dot(band_ref[g], ext[:, gsl[g]], preferred_element_type=F32) for g in groups]
    dev = []
    for g, w in enumerate(POOL_WINDOWS):
        inv_cnt = 1.0 / (jnp.minimum(t + (w - w // 2), seg_len) - jnp.maximum(t - w // 2, 0)).astype(F32)
        dev.append((acc[g] * inv_cnt - u[:, gsl[g]].astype(F32)).astype(BF16))
    pg = [jnp.dot(dev[g], pw_ref[g], preferred_element_type=F32) for g in groups]
    for g in groups:
        gate = _silu(pz_ref[:, gsl[g]].astype(F32)) * ps_ref[:, gsl[g]]
        yd_ref[:, gsl[g]] = (pg[g] * gate).astype(BF16)


def _pool_band():
    t = np.arange(ROW_TILE)[:, None]
    pos = np.concatenate([np.arange(ROW_TILE), np.arange(-HALO, 0), np.arange(ROW_TILE, ROW_TILE + HALO),
                          np.full(POOL_K - ROW_TILE - 2 * HALO, -10 * ROW_TILE)])[None, :]
    return np.stack([(pos >= t - w // 2) & (pos < t + w - w // 2) for w in POOL_WINDOWS]).astype(np.float32)


def _local_mixers(yb, conv_w, pool_w, pool_scale, tiles_per_batch):
    m = yb.shape[0]
    nt = m // ROW_TILE
    hb = ROW_TILE // HALO
    n_halo = m // HALO

    def col(cidx):
        return pl.BlockSpec((ROW_TILE, BRANCH_W), lambda i: (i, cidx))

    def prev(cidx):
        return pl.BlockSpec((HALO, BRANCH_W), lambda i: (jnp.maximum(i * hb - 1, 0), cidx))

    def nxt(cidx):
        return pl.BlockSpec((HALO, BRANCH_W), lambda i: (jnp.minimum((i + 1) * hb, n_halo - 1), cidx))

    out_sd = jax.ShapeDtypeStruct((m, BRANCH_W), BF16)
    return pl.pallas_call(
        functools.partial(_local_kernel, tiles_per_batch=tiles_per_batch),
        out_shape=(out_sd, out_sd),
        grid=(nt,),
        in_specs=[col(B_CU), col(B_CB), col(B_CC), col(B_CZ), col(B_PU), col(B_PZ),
                  prev(B_CU), prev(B_CC), prev(B_PU), nxt(B_CU), nxt(B_CC), nxt(B_PU),
                  pl.BlockSpec((3, BRANCH_W), lambda i: (0, 0)),
                  pl.BlockSpec((len(POOL_WINDOWS), POOL_GROUP, POOL_GROUP), lambda i: (0, 0, 0)),
                  pl.BlockSpec((1, BRANCH_W), lambda i: (0, 0)),
                  pl.BlockSpec((len(POOL_WINDOWS), ROW_TILE, POOL_K), lambda i: (0, 0, 0))],
        out_specs=(pl.BlockSpec((ROW_TILE, BRANCH_W), lambda i: (i, 0)),
                   pl.BlockSpec((ROW_TILE, BRANCH_W), lambda i: (i, 0))),
        compiler_params=_cparams(("parallel",)),
        name="local_mixers",
    )(*([yb] * 12), conv_w, pool_w, pool_scale.reshape(1, BRANCH_W), jnp.asarray(_pool_band(), BF16))


def _mergeout_kernel(*refs, final):
    if final:
        b0, b1, b2, b3, g_ref, wb_ref, wo_ref, x_ref, gt_ref, fg_ref, o_ref = refs
    else:
        b0, b1, b2, b3, g_ref, wb_ref, wo_ref, x_ref, gt_ref, o_ref = refs
    acc = None
    for j, br in enumerate((b0, b1, b2, b3)):
        gate = jax.nn.sigmoid(g_ref[:, j * D_MODEL:(j + 1) * D_MODEL].astype(F32))
        term = gate * jnp.dot(br[...], wb_ref[j], preferred_element_type=F32)
        acc = term if acc is None else acc + term
    y = x_ref[...] + gt_ref[0] * jnp.dot(acc.astype(BF16), wo_ref[...], preferred_element_type=F32)
    if final:
        y = y * lax.rsqrt(jnp.mean(y * y, axis=-1, keepdims=True) + EPS) * fg_ref[...]
    o_ref[...] = y


def _mergeout(branches, yb, w_branch, w_out, layer, xs, gt_t, final_gain, batch, tiles_per_batch):
    m, d = xs.shape
    final = final_gain is not None
    lat = tiles_per_batch - 1
    n_r = lat if final else tiles_per_batch
    resident = pl.Buffered(1)

    def src(b, r):
        return b * tiles_per_batch + r

    br_spec = pl.BlockSpec((ROW_TILE, BRANCH_W), lambda b, r: (src(b, r), 0))
    in_specs = [br_spec, br_spec, br_spec, br_spec,
                pl.BlockSpec((ROW_TILE, N_BRANCH * d), lambda b, r: (src(b, r), B_GATE)),
                pl.BlockSpec((None, N_BRANCH, BRANCH_W, d), lambda b, r: (layer, 0, 0, 0), pipeline_mode=resident),
                pl.BlockSpec((None, d, d), lambda b, r: (layer, 0, 0), pipeline_mode=resident),
                pl.BlockSpec((ROW_TILE, d), lambda b, r: (src(b, r), 0)),
                pl.BlockSpec((1, 1, d), lambda b, r: (src(b, r), 0, 0))]
    args = [*branches, yb, w_branch, w_out, xs, gt_t]
    if final:
        in_specs.append(pl.BlockSpec((1, d), lambda b, r: (0, 0)))
        args.append(final_gain.reshape(1, d))
    return pl.pallas_call(
        functools.partial(_mergeout_kernel, final=final),
        out_shape=jax.ShapeDtypeStruct((batch * n_r * ROW_TILE, d), F32),
        grid=(batch, n_r),
        in_specs=in_specs,
        out_specs=pl.BlockSpec((ROW_TILE, d), lambda b, r: (b * n_r + r, 0)),
        compiler_params=_cparams(("parallel", "parallel")),
        name="mergeout_final" if final else "mergeout",
    )(*args)


def _rope_tables(t_lat):
    pos = np.arange(t_lat)
    quarter = ATT_HEAD_DIM // 4
    freq = ROPE_THETA ** (-jnp.arange(quarter, dtype=F32) / quarter)
    a_row = jnp.asarray(pos // GRID_W, F32)[:, None] * freq[None, :]
    a_col = jnp.asarray(pos % GRID_W, F32)[:, None] * freq[None, :]
    cos_t = jnp.concatenate([jnp.cos(a_row), jnp.cos(a_row), jnp.cos(a_col), jnp.cos(a_col)], axis=-1)
    sin_t = jnp.concatenate([-jnp.sin(a_row), jnp.sin(a_row), -jnp.sin(a_col), jnp.sin(a_col)], axis=-1)
    return cos_t, sin_t


def kernel(x, c, ctx, c_ctx, norm_gain, w_mod, b_mod, w_in, q_norm_gain, k_norm_gain, mlstm_gate_bias,
           mlstm_norm_gain, conv_w, pool_w, pool_scale, w_branch, w_out, final_norm_gain):
    batch, t_lat, d = x.shape
    depth = w_in.shape[0]
    assert d == D_MODEL and ctx.shape[1] == CTX_LEN and t_lat % ROW_TILE == 0 and batch < 8
    s_len = t_lat + CTX_LEN
    tiles_per_batch = s_len // ROW_TILE

    cc = jnp.zeros((8, d), F32).at[:batch].set(c).at[batch].set(c_ctx)
    mod = _modulation(cc, w_mod, b_mod)
    tile_row = np.array([b if r < tiles_per_batch - 1 else batch
                         for b in range(batch) for r in range(tiles_per_batch)], np.int32)
    cos_t, sin_t = _rope_tables(t_lat)
    assert w_in.shape[1:] == (d, N_IN)
    w_t = jnp.swapaxes(w_in, 1, 2).reshape(depth * N_IN, d)
    w_branch_b, w_out_b, pool_w_b = w_branch.astype(BF16), w_out.astype(BF16), pool_w.astype(BF16)

    xs, out = None, None
    for l in range(depth):
        mod_t = mod[l][tile_row][:, None, :]
        sh_t, sc_t, gt_t = mod_t[..., :d], mod_t[..., d:2 * d], mod_t[..., 2 * d:]
        bias = jnp.zeros((1, 128), F32).at[0, :4 * MLSTM_HEADS].set(mlstm_gate_bias[l].reshape(-1))

        if l == 0:
            h, xs = _normmod_first(x, ctx, norm_gain[l], sc_t, sh_t)
        else:
            h = _normmod(xs, norm_gain[l], sc_t, sh_t)
        ya = _inproj(h, w_t, l, 7, lambda j: jnp.where(j == 0, 0, 512 + IN_UNIT * j), "inproj_a")
        yb = _inproj(h, w_t, l, 14, lambda j: jnp.where(j < 8, W_MERGE_ROW + IN_UNIT * j,
                                                       W_LOCAL_ROW + IN_UNIT * (j - 8)), "inproj_b")
        tail = _inproj_tail(h, w_t, l)

        qb, kb, vt = _qkvprep(ya, tail, cos_t, sin_t, q_norm_gain[l], k_norm_gain[l], batch, tiles_per_batch)
        y_att = _attention(qb, ya, kb, vt, batch, s_len)
        g, g_t, pre, pre_t, suf, suf_t = _gateprep(tail, bias)
        h_f = _mlstm_pass(ya, (g, g_t, pre, pre_t), None, None, batch, tiles_per_batch, reverse=False)
        y_mls = _mlstm_pass(ya, (g, g_t, suf, suf_t), mlstm_norm_gain[l], h_f, batch, tiles_per_batch,
                            reverse=True)
        y_conv, y_pool = _local_mixers(yb, conv_w[l], pool_w_b[l], pool_scale[l], tiles_per_batch)

        last = l == depth - 1
        res = _mergeout((y_att, y_mls, y_conv, y_pool), yb, w_branch_b, w_out_b, l, xs, gt_t,
                        final_norm_gain if last else None, batch, tiles_per_batch)
        if last:
            out = res.reshape(batch, t_lat, d)
        else:
            xs = res
    return out
```

```python
import functools

import numpy as np
import jax
import jax.numpy as jnp
from jax import lax
from jax.experimental import pallas as pl
from jax.experimental.pallas import tpu as pltpu

F32 = jnp.float32
BF16 = jnp.bfloat16

D_MODEL = 2048
BRANCH_W = 1024
GRID_W = 64
CTX_LEN = 256
EPS = 1e-6
ATT_HEAD_DIM = 128
ATT_GROUP = 4
ATT_KV_HEADS = 2
ATT_SCALE = ATT_HEAD_DIM ** -0.5
LOG2_E = 1.4426950408889634
ROPE_THETA = 10000.0
MLSTM_HEADS = 4
MLSTM_HEAD_DIM = 256
MLSTM_K_SCALE = MLSTM_HEAD_DIM ** -0.5
M_INIT = -1e30
POOL_WINDOWS = (2, 4, 8, 16)
POOL_GROUP = 256
N_BRANCH = 4

ROW_TILE = 256
HALO = 16
POOL_K = 384
ATT_Q_TILE = 256
ATT_KEY_CHUNK = 1024
VT_ROWS = ATT_HEAD_DIM + 16
V7X_VMEM_LIMIT = 56 * 1024 * 1024

A_AQ, A_AZ, A_MQ, A_MK, A_MV, A_MO, A_MZ = range(7)
B_GATE = 0
B_CU, B_CB, B_CC, B_CZ, B_PU, B_PZ = range(8, 14)
N_TAIL = 640


def _cparams(sem, vmem=V7X_VMEM_LIMIT):
    return pltpu.CompilerParams(dimension_semantics=sem, vmem_limit_bytes=vmem)


def _silu(x):
    return x * jax.nn.sigmoid(x)


def _mod_kernel(c_ref, w_ref, b_ref, o_ref):
    a = _silu(c_ref[...]).astype(BF16)
    o_ref[...] = jnp.dot(a, w_ref[...].astype(BF16), preferred_element_type=F32) + b_ref[...]


def _modulation(cc, w_mod, b_mod):
    depth, d, n = w_mod.shape
    tn = 768
    return pl.pallas_call(
        _mod_kernel,
        out_shape=jax.ShapeDtypeStruct((depth, 8, n), F32),
        grid=(depth, n // tn),
        in_specs=[pl.BlockSpec((8, d), lambda l, j: (0, 0)),
                  pl.BlockSpec((None, d, tn), lambda l, j: (l, 0, j)),
                  pl.BlockSpec((None, 1, tn), lambda l, j: (l, 0, j))],
        out_specs=pl.BlockSpec((None, 8, tn), lambda l, j: (l, 0, j)),
        compiler_params=_cparams(("parallel", "parallel")),
        name="modulation",
    )(cc, w_mod, b_mod.reshape(depth, 1, n))


def _normmod_kernel(x_ref, g_ref, sc_ref, sh_ref, o_ref):
    x = x_ref[...]
    y = x * lax.rsqrt(jnp.mean(x * x, axis=-1, keepdims=True) + EPS) * g_ref[...]
    o_ref[...] = (y * (1.0 + sc_ref[0]) + sh_ref[0]).astype(BF16)


def _normmod(xs, gain, sc_t, sh_t):
    m, d = xs.shape
    nt = m // ROW_TILE
    return pl.pallas_call(
        _normmod_kernel,
        out_shape=jax.ShapeDtypeStruct((m, d), BF16),
        grid=(nt,),
        in_specs=[pl.BlockSpec((ROW_TILE, d), lambda i: (i, 0)),
                  pl.BlockSpec((1, d), lambda i: (0, 0)),
                  pl.BlockSpec((1, 1, d), lambda i: (i, 0, 0)),
                  pl.BlockSpec((1, 1, d), lambda i: (i, 0, 0))],
        out_specs=pl.BlockSpec((ROW_TILE, d), lambda i: (i, 0)),
        compiler_params=_cparams(("parallel",)),
        name="normmod",
    )(xs, gain.reshape(1, d), sc_t, sh_t)


def _normmod_first_kernel(x_ref, c_ref, g_ref, sc_ref, sh_ref, o_ref, xs_ref, *, tiles_per_batch):
    x = jnp.where(pl.program_id(1) == tiles_per_batch - 1, c_ref[...], x_ref[...])
    xs_ref[...] = x
    y = x * lax.rsqrt(jnp.mean(x * x, axis=-1, keepdims=True) + EPS) * g_ref[...]
    o_ref[...] = (y * (1.0 + sc_ref[0]) + sh_ref[0]).astype(BF16)


def _normmod_first(x, ctx, gain, sc_t, sh_t):
    batch, t_lat, d = x.shape
    tpb = t_lat // ROW_TILE + 1
    m = batch * tpb * ROW_TILE
    tab_spec = pl.BlockSpec((1, 1, d), lambda b, r: (b * tpb + r, 0, 0))
    row_spec = pl.BlockSpec((ROW_TILE, d), lambda b, r: (b * tpb + r, 0))
    return pl.pallas_call(
        functools.partial(_normmod_first_kernel, tiles_per_batch=tpb),
        out_shape=(jax.ShapeDtypeStruct((m, d), BF16), jax.ShapeDtypeStruct((m, d), F32)),
        grid=(batch, tpb),
        in_specs=[pl.BlockSpec((None, ROW_TILE, d), lambda b, r: (b, jnp.minimum(r, tpb - 2), 0)),
                  pl.BlockSpec((None, CTX_LEN, d), lambda b, r: (b, 0, 0)),
                  pl.BlockSpec((1, d), lambda b, r: (0, 0)),
                  tab_spec, tab_spec],
        out_specs=(row_spec, row_spec),
        compiler_params=_cparams(("parallel", "parallel")),
        name="normmod_first",
    )(x, ctx, gain.reshape(1, d), sc_t, sh_t)


IN_UNIT = 1024
W_KV_ROW, W_MGATE_ROW, W_LOCAL_ROW, W_MERGE_ROW = 1024, 7680, 7696, 13840
N_IN = W_MERGE_ROW + N_BRANCH * D_MODEL


def _pick_tile(n, candidates):
    for t in candidates:
        if n % t == 0:
            return t
    raise ValueError(f"no tile for {n}")


def _inproj_kernel(h_ref, w_ref, o_ref, wt_s):
    @pl.when(pl.program_id(1) == 0)
    def _():
        wt_s[...] = w_ref[...].T.astype(BF16)

    o_ref[...] = jnp.dot(h_ref[...], wt_s[...], preferred_element_type=F32).astype(o_ref.dtype)


def _inproj(h, w_t, layer, n_units, unit_row, name):
    m, k = h.shape
    tm = _pick_tile(m, (1024, 512, 256))
    base = layer * N_IN
    return pl.pallas_call(
        _inproj_kernel,
        out_shape=jax.ShapeDtypeStruct((m, n_units * IN_UNIT), BF16),
        grid=(n_units, m // tm),
        in_specs=[pl.BlockSpec((tm, k), lambda j, i: (i, 0)),
                  pl.BlockSpec((pl.Element(IN_UNIT), pl.Element(k)),
                               lambda j, i: (pl.multiple_of(base + unit_row(j), 8), 0))],
        out_specs=pl.BlockSpec((tm, IN_UNIT), lambda j, i: (i, j)),
        scratch_shapes=[pltpu.VMEM((k, IN_UNIT), BF16)],
        compiler_params=_cparams(("parallel", "arbitrary")),
        name=name,
    )(h, w_t)


def _inproj_tail_kernel(h_ref, wkv_ref, wg_ref, o_ref):
    nt = (((1,), (1,)), ((), ()))
    h = h_ref[...]
    kvw = wkv_ref.shape[0]
    o_ref[:, :kvw] = lax.dot_general(h, wkv_ref[...].astype(BF16), nt, preferred_element_type=F32)
    g = lax.dot_general(h, wg_ref[...].astype(BF16), nt, preferred_element_type=F32)
    lane = lax.broadcasted_iota(jnp.int32, g.shape, 1)
    o_ref[:, kvw:] = jnp.where(lane < 4 * MLSTM_HEADS, g, 0.0)


def _inproj_tail(h, w_t, layer):
    m, k = h.shape
    tm = _pick_tile(m, (1024, 512, 256))
    kvw = 2 * ATT_KV_HEADS * ATT_HEAD_DIM
    return pl.pallas_call(
        _inproj_tail_kernel,
        out_shape=jax.ShapeDtypeStruct((m, N_TAIL), F32),
        grid=(m // tm,),
        in_specs=[pl.BlockSpec((tm, k), lambda i: (i, 0)),
                  pl.BlockSpec((pl.Element(kvw), pl.Element(k)), lambda i: (layer * N_IN + W_KV_ROW, 0)),
                  pl.BlockSpec((pl.Element(128), pl.Element(k)), lambda i: (layer * N_IN + W_MGATE_ROW, 0))],
        out_specs=pl.BlockSpec((tm, N_TAIL), lambda i: (i, 0)),
        compiler_params=_cparams(("parallel",)),
        name="inproj_tail",
    )(h, w_t, w_t)


def _qkvprep_kernel(q_ref, t_ref, cos_ref, sin_ref, qg_ref, kg_ref, q_out, k_out, vt_out, *, tiles_per_batch):
    is_lat = pl.program_id(1) < tiles_per_batch - 1
    cos, sin, kg = cos_ref[...], sin_ref[...], kg_ref[...]
    qg = qg_ref[...] * (ATT_SCALE * LOG2_E)
    hd = ATT_HEAD_DIM
    n_q = ATT_KV_HEADS * ATT_GROUP
    sls = [slice(h * hd, (h + 1) * hd) for h in range(n_q)]
    xs = [q_ref[:, sl].astype(F32) for sl in sls] + [t_ref[:, sl] for sl in sls[:ATT_KV_HEADS]]
    gains = [qg] * n_q + [kg] * ATT_KV_HEADS
    xs = [x * lax.rsqrt(jnp.mean(x * x, axis=-1, keepdims=True) + EPS) * g for x, g in zip(xs, gains)]
    lane = lax.broadcasted_iota(jnp.int32, xs[0].shape, 1)
    low = (lane % 64) < 32
    up = [pltpu.roll(x, 96, 1) for x in xs]
    down = [pltpu.roll(x, 32, 1) for x in xs]
    outs = [jnp.where(is_lat, x * cos + jnp.where(low, a, b) * sin, x).astype(BF16)
            for x, a, b in zip(xs, up, down)]
    for h in range(n_q):
        q_out[:, sls[h]] = outs[h]
    for h in range(ATT_KV_HEADS):
        k_out[:, sls[h]] = outs[n_q + h]
        vt_out[h, 0:hd, :] = t_ref[:, sls[ATT_KV_HEADS + h]].T.astype(BF16)
        vt_out[h, hd:, :] = jnp.ones((VT_ROWS - hd, ROW_TILE), BF16)


def _qkvprep(ya, tail, cos_t, sin_t, q_gain, k_gain, batch, tiles_per_batch):
    m = tail.shape[0]
    lat_tiles = tiles_per_batch - 1
    kvw = ATT_KV_HEADS * ATT_HEAD_DIM
    rope_spec = pl.BlockSpec((ROW_TILE, ATT_HEAD_DIM), lambda b, r: (jnp.minimum(r, lat_tiles - 1), 0))
    gain_spec = pl.BlockSpec((1, ATT_HEAD_DIM), lambda b, r: (0, 0))
    return pl.pallas_call(
        functools.partial(_qkvprep_kernel, tiles_per_batch=tiles_per_batch),
        out_shape=(jax.ShapeDtypeStruct((m, BRANCH_W), BF16),
                   jax.ShapeDtypeStruct((m, kvw), BF16),
                   jax.ShapeDtypeStruct((batch, ATT_KV_HEADS, VT_ROWS, tiles_per_batch * ROW_TILE), BF16)),
        grid=(batch, tiles_per_batch),
        in_specs=[pl.BlockSpec((ROW_TILE, BRANCH_W), lambda b, r: (b * tiles_per_batch + r, A_AQ)),
                  pl.BlockSpec((ROW_TILE, 2 * kvw), lambda b, r: (b * tiles_per_batch + r, 0)),
                  rope_spec, rope_spec, gain_spec, gain_spec],
        out_specs=(pl.BlockSpec((ROW_TILE, BRANCH_W), lambda b, r: (b * tiles_per_batch + r, 0)),
                   pl.BlockSpec((ROW_TILE, kvw), lambda b, r: (b * tiles_per_batch + r, 0)),
                   pl.BlockSpec((None, ATT_KV_HEADS, VT_ROWS, ROW_TILE), lambda b, r: (b, 0, 0, r))),
        compiler_params=_cparams(("parallel", "parallel")),
        name="qkvprep",
    )(ya, tail, cos_t, sin_t, q_gain.reshape(1, ATT_HEAD_DIM), k_gain.reshape(1, ATT_HEAD_DIM))


def _attn_kernel(q_ref, z_ref, k_ref, vt_ref, o_ref, *, n_lat_q, t_lat, kc):
    is_lat = pl.program_id(2) < n_lat_q
    hd = ATT_HEAD_DIM
    q = jnp.concatenate([q_ref[:, g * hd:(g + 1) * hd] for g in range(ATT_GROUP)], axis=0)

    def attend(chunks):
        def scores(c):
            off, n = chunks[c]
            return lax.dot_general(k_ref[off:off + n, :], q, (((1,), (1,)), ((), ())),
                                   preferred_element_type=F32)

        m, o = None, None
        s_next = scores(0)
        for c, (off, n) in enumerate(chunks):
            s = s_next
            if c + 1 < len(chunks):
                s_next = scores(c + 1)
            mc = jnp.max(s, axis=0, keepdims=True)
            m_new = mc if m is None else jnp.maximum(m, mc)
            p = jnp.exp2((s - m_new).astype(BF16))
            oc = jnp.dot(vt_ref[:, off:off + n], p, preferred_element_type=F32)
            o = oc if o is None else jnp.exp2(m - m_new) * o + oc
            m = m_new
        o = o[0:hd] / o[hd:hd + 1]
        for g in range(ATT_GROUP):
            sl = slice(g * ATT_HEAD_DIM, (g + 1) * ATT_HEAD_DIM)
            og = o[:, g * ATT_Q_TILE:(g + 1) * ATT_Q_TILE].T
            o_ref[:, sl] = (og * _silu(z_ref[:, sl].astype(F32))).astype(BF16)

    ctx_chunk = (t_lat, CTX_LEN)

    @pl.when(is_lat)
    def _():
        attend([(off, kc) for off in range(0, t_lat, kc)] + [ctx_chunk])

    @pl.when(jnp.logical_not(is_lat))
    def _():
        attend([ctx_chunk])


def _attention(qb, ya, kb, vt, batch, s_len):
    m = ya.shape[0]
    t_lat = s_len - CTX_LEN
    nq = s_len // ATT_Q_TILE
    n_lat_q = t_lat // ATT_Q_TILE
    gw = ATT_GROUP * ATT_HEAD_DIM
    return pl.pallas_call(
        functools.partial(_attn_kernel, n_lat_q=n_lat_q, t_lat=t_lat,
                          kc=_pick_tile(t_lat, (ATT_KEY_CHUNK, 512, 256))),
        out_shape=jax.ShapeDtypeStruct((m, BRANCH_W), BF16),
        grid=(batch, ATT_KV_HEADS, nq),
        in_specs=[pl.BlockSpec((ATT_Q_TILE, gw), lambda b, h, i: (b * nq + i, h)),
                  pl.BlockSpec((ATT_Q_TILE, gw), lambda b, h, i: (b * nq + i, A_AZ * 2 + h)),
                  pl.BlockSpec((s_len, ATT_HEAD_DIM), lambda b, h, i: (b, h)),
                  pl.BlockSpec((None, None, VT_ROWS, s_len), lambda b, h, i: (b, h, 0, 0))],
        out_specs=pl.BlockSpec((ATT_Q_TILE, gw), lambda b, h, i: (b * nq + i, h)),
        compiler_params=_cparams(("parallel", "parallel", "arbitrary")),
        name="attention",
    )(qb, ya, kb, vt)


def _gateprep_kernel(g_ref, bias_ref, g_out, gt_out, pre_out, pret_out, suf_out, suft_out):
    L = ROW_TILE
    gates = g_ref[...] + bias_ref[...]
    log_f = jnp.minimum(gates, 0.0) - jnp.log1p(jnp.exp(-jnp.abs(gates)))
    row = lax.broadcasted_iota(jnp.int32, (L, L), 0)
    col = lax.broadcasted_iota(jnp.int32, (L, L), 1)
    hi = lax.Precision.HIGHEST
    pre = jnp.dot((col <= row).astype(F32), log_f, precision=hi, preferred_element_type=F32)
    suf = jnp.dot((col >= row).astype(F32), log_f, precision=hi, preferred_element_type=F32)
    g_out[...] = gates
    gt_out[...] = gates.T
    pre_out[...] = pre
    pret_out[...] = pre.T
    suf_out[...] = suf
    suft_out[...] = suf.T


def _gateprep(tail, bias):
    m = tail.shape[0]
    nt = m // ROW_TILE
    row_sd, col_sd = jax.ShapeDtypeStruct((m, 128), F32), jax.ShapeDtypeStruct((128, m), F32)
    row_spec = pl.BlockSpec((ROW_TILE, 128), lambda i: (i, 0))
    col_spec = pl.BlockSpec((128, ROW_TILE), lambda i: (0, i))
    return pl.pallas_call(
        _gateprep_kernel,
        out_shape=(row_sd, col_sd, row_sd, col_sd, row_sd, col_sd),
        grid=(nt,),
        in_specs=[pl.BlockSpec((ROW_TILE, 128), lambda i: (i, (N_TAIL - 128) // 128)),
                  pl.BlockSpec((1, 128), lambda i: (0, 0))],
        out_specs=(row_spec, col_spec, row_spec, col_spec, row_spec, col_spec),
        compiler_params=_cparams(("parallel",)),
        name="gateprep",
    )(tail, bias)


def _mlstm_kernel(*refs, reverse, final):
    if final:
        (q_ref, k_ref, v_ref, g_ref, gt_ref, bc_ref, bct_ref, o_ref, z_ref, gain_ref, hprev_ref,
         out_ref, c_s, n_s, m_s) = refs
    else:
        q_ref, k_ref, v_ref, g_ref, gt_ref, bc_ref, bct_ref, out_ref, c_s, n_s, m_s = refs
    L = ROW_TILE
    hd = MLSTM_HEAD_DIM

    @pl.when(pl.program_id(1) == 0)
    def _():
        c_s[...] = jnp.zeros(c_s.shape, F32)
        n_s[...] = jnp.zeros(n_s.shape, F32)
        m_s[...] = jnp.full(m_s.shape, M_INIT, F32)

    gates, gates_t = g_ref[...], gt_ref[...]
    bcum, bcum_t = bc_ref[...], bct_ref[...]
    row = lax.broadcasted_iota(jnp.int32, (L, L), 0)
    col = lax.broadcasted_iota(jnp.int32, (L, L), 1)
    tri = (col >= row) if reverse else (col <= row)
    last = 0 if reverse else L - 1
    heads = range(MLSTM_HEADS)
    hsl = [slice(h * hd, (h + 1) * hd) for h in heads]
    icol = [(2 if reverse else 0) * MLSTM_HEADS + h for h in heads]
    fcol = [(3 if reverse else 1) * MLSTM_HEADS + h for h in heads]
    c_old = [c_s[h] for h in heads]
    n_old = [n_s[h, 0:1, :] for h in heads]
    m_old = [m_s[h, 0:1, 0:1] for h in heads]
    q = [q_ref[:, hsl[h]] for h in heads]
    k = [k_ref[:, hsl[h]].astype(F32) * MLSTM_K_SCALE for h in heads]
    v = [v_ref[:, hsl[h]] for h in heads]
    b_col = [bcum[:, fcol[h]:fcol[h] + 1] for h in heads]
    i_col = [gates[:, icol[h]:icol[h] + 1] for h in heads]

    qk = [lax.dot_general(q[h], k[h].astype(BF16), (((1,), (1,)), ((), ())), preferred_element_type=F32)
          for h in heads]
    qc = [jnp.dot(q[h], c_old[h].astype(BF16), preferred_element_type=F32) for h in heads]

    m_t, dmat, inter = [], [], []
    for h in heads:
        b_row, i_row = bcum_t[fcol[h]:fcol[h] + 1, :], gates_t[icol[h]:icol[h] + 1, :]
        log_d = jnp.where(tri, b_col[h] - b_row + i_row, -jnp.inf)
        log_inter = b_col[h] + m_old[h]
        m_t.append(jnp.maximum(log_inter, jnp.max(log_d, axis=1, keepdims=True)))
        dmat.append(jnp.exp(log_d - m_t[h]))
        inter.append(jnp.exp(log_inter - m_t[h]))

    sm = [qk[h] * dmat[h] for h in heads]
    sv = [jnp.dot(sm[h].astype(BF16), v[h], preferred_element_type=F32) for h in heads]

    kw, decay, m_new = [], [], []
    for h in heads:
        b_last = bcum[last:last + 1, fcol[h]:fcol[h] + 1]
        log_w = b_last - b_col[h] + i_col[h]
        m_new.append(jnp.maximum(b_last + m_old[h], jnp.max(log_w, axis=0, keepdims=True)))
        decay.append(jnp.exp(b_last + m_old[h] - m_new[h]))
        kw.append(k[h] * jnp.exp(log_w - m_new[h]))
    kv = [lax.dot_general(kw[h].astype(BF16), v[h], (((0,), (0,)), ((), ())), preferred_element_type=F32)
          for h in heads]

    for h in heads:
        num = inter[h] * qc[h] + sv[h]
        den = inter[h] * jnp.sum(q[h].astype(F32) * n_old[h], axis=1, keepdims=True) \
            + jnp.sum(sm[h], axis=1, keepdims=True)
        hh = num / jnp.maximum(jnp.abs(den), jnp.exp(-m_t[h]))
        if final:
            ht = hh + hprev_ref[:, hsl[h]]
            hn = ht * lax.rsqrt(jnp.mean(ht * ht, axis=-1, keepdims=True) + EPS)
            y = hn * gain_ref[:, hsl[h]] * jax.nn.sigmoid(o_ref[:, hsl[h]].astype(F32)) \
                * _silu(z_ref[:, hsl[h]].astype(F32))
            out_ref[:, hsl[h]] = y.astype(BF16)
        else:
            out_ref[:, hsl[h]] = hh

    for h in heads:
        c_s[h] = decay[h] * c_old[h] + kv[h]
        n_s[h, 0:1, :] = decay[h] * n_old[h] + jnp.sum(kw[h], axis=0, keepdims=True)
        m_s[h] = jnp.broadcast_to(m_new[h], m_s.shape[1:])


def _mlstm_pass(ya, gate_arrays, gain, hprev, batch, tiles_per_batch, reverse):
    m = ya.shape[0]
    lat = tiles_per_batch - 1
    final = hprev is not None

    def rows(b, c):
        r = jnp.where(c == 0, lat, (lat - c) if reverse else (c - 1))
        return b * tiles_per_batch + r

    def col(cidx):
        return pl.BlockSpec((ROW_TILE, BRANCH_W), lambda b, c: (rows(b, c), cidx))

    gate_spec = pl.BlockSpec((ROW_TILE, 128), lambda b, c: (rows(b, c), 0))
    gate_t_spec = pl.BlockSpec((128, ROW_TILE), lambda b, c: (0, rows(b, c)))
    in_specs = [col(A_MQ), col(A_MK), col(A_MV), gate_spec, gate_t_spec, gate_spec, gate_t_spec]
    args = [ya, ya, ya, *gate_arrays]
    if final:
        in_specs += [col(A_MO), col(A_MZ), pl.BlockSpec((1, BRANCH_W), lambda b, c: (0, 0)),
                     pl.BlockSpec((ROW_TILE, BRANCH_W), lambda b, c: (rows(b, c), 0))]
        args += [ya, ya, gain.reshape(1, BRANCH_W), hprev]
    return pl.pallas_call(
        functools.partial(_mlstm_kernel, reverse=reverse, final=final),
        out_shape=jax.ShapeDtypeStruct((m, BRANCH_W), BF16 if final else F32),
        grid=(batch, tiles_per_batch),
        in_specs=in_specs,
        out_specs=pl.BlockSpec((ROW_TILE, BRANCH_W), lambda b, c: (rows(b, c), 0)),
        scratch_shapes=[pltpu.VMEM((MLSTM_HEADS, MLSTM_HEAD_DIM, MLSTM_HEAD_DIM), F32),
                        pltpu.VMEM((MLSTM_HEADS, 8, MLSTM_HEAD_DIM), F32),
                        pltpu.VMEM((MLSTM_HEADS, 8, 128), F32)],
        compiler_params=_cparams(("parallel", "arbitrary")),
        name="mlstm_bwd" if reverse else "mlstm_fwd",
    )(*args)


def _local_kernel(cu_ref, cb_ref, cc_ref, cz_ref, pu_ref, pz_ref,
                  cu_p, cc_p, pu_p, cu_n, cc_n, pu_n,
                  cw_ref, pw_ref, ps_ref, band_ref, yc_ref, yd_ref, *, tiles_per_batch):
    r = pl.program_id(0) % tiles_per_batch
    lat = tiles_per_batch - 1
    has_prev = jnp.logical_and(r != 0, r != lat)
    has_next = jnp.logical_and(r != lat - 1, r != lat)
    seg_len = jnp.where(r == lat, CTX_LEN, lat * ROW_TILE)
    t0 = jnp.where(r == lat, 0, r * ROW_TILE)
    rowi = lax.broadcasted_iota(jnp.int32, (ROW_TILE, 1), 0)

    for j in range(BRANCH_W // 128):
        cs = slice(j * 128, (j + 1) * 128)
        a = cc_ref[:, cs].astype(F32) * cu_ref[:, cs].astype(F32)
        a_prev = jnp.where(has_prev, cc_p[HALO - 1:HALO, cs].astype(F32) * cu_p[HALO - 1:HALO, cs].astype(F32), 0.0)
        a_next = jnp.where(has_next, cc_n[0:1, cs].astype(F32) * cu_n[0:1, cs].astype(F32), 0.0)
        a_m1 = jnp.where(rowi == 0, a_prev, pltpu.roll(a, 1, 0))
        a_p1 = jnp.where(rowi == ROW_TILE - 1, a_next, pltpu.roll(a, ROW_TILE - 1, 0))
        y = cw_ref[0:1, cs] * a_m1 + cw_ref[1:2, cs] * a + cw_ref[2:3, cs] * a_p1
        yc_ref[:, cs] = (cb_ref[:, cs].astype(F32) * y * _silu(cz_ref[:, cs].astype(F32))).astype(BF16)

    u = pu_ref[...]
    halo_zero = jnp.zeros((HALO, BRANCH_W), u.dtype)
    ext = jnp.concatenate([u, jnp.where(has_prev, pu_p[...], halo_zero), jnp.where(has_next, pu_n[...], halo_zero),
                           jnp.zeros((POOL_K - ROW_TILE - 2 * HALO, BRANCH_W), u.dtype)], axis=0)
    t = t0 + rowi
    groups = range(len(POOL_WINDOWS))
    gsl = [slice(g * POOL_GROUP, (g + 1) * POOL_GROUP) for g in groups]
    acc = [jnp.dot(band_ref[g], ext[:, gsl[g]], preferred_element_type=F32) for g in groups]
    dev = []
    for g, w in enumerate(POOL_WINDOWS):
        inv_cnt = 1.0 / (jnp.minimum(t + (w - w // 2), seg_len) - jnp.maximum(t - w // 2, 0)).astype(F32)
        dev.append((acc[g] * inv_cnt - u[:, gsl[g]].astype(F32)).astype(BF16))
    pg = [jnp.dot(dev[g], pw_ref[g], preferred_element_type=F32) for g in groups]
    for g in groups:
        gate = _silu(pz_ref[:, gsl[g]].astype(F32)) * ps_ref[:, gsl[g]]
        yd_ref[:, gsl[g]] = (pg[g] * gate).astype(BF16)


def _pool_band():
    t = np.arange(ROW_TILE)[:, None]
    pos = np.concatenate([np.arange(ROW_TILE), np.arange(-HALO, 0), np.arange(ROW_TILE, ROW_TILE + HALO),
                          np.full(POOL_K - ROW_TILE - 2 * HALO, -10 * ROW_TILE)])[None, :]
    return np.stack([(pos >= t - w // 2) & (pos < t + w - w // 2) for w in POOL_WINDOWS]).astype(np.float32)


def _local_mixers(yb, conv_w, pool_w, pool_scale, tiles_per_batch):
    m = yb.shape[0]
    nt = m // ROW_TILE
    hb = ROW_TILE // HALO
    n_halo = m // HALO

    def col(cidx):
        return pl.BlockSpec((ROW_TILE, BRANCH_W), lambda i: (i, cidx))

    def prev(cidx):
        return pl.BlockSpec((HALO, BRANCH_W), lambda i: (jnp.maximum(i * hb - 1, 0), cidx))

    def nxt(cidx):
        return pl.BlockSpec((HALO, BRANCH_W), lambda i: (jnp.minimum((i + 1) * hb, n_halo - 1), cidx))

    out_sd = jax.ShapeDtypeStruct((m, BRANCH_W), BF16)
    return pl.pallas_call(
        functools.partial(_local_kernel, tiles_per_batch=tiles_per_batch),
        out_shape=(out_sd, out_sd),
        grid=(nt,),
        in_specs=[col(B_CU), col(B_CB), col(B_CC), col(B_CZ), col(B_PU), col(B_PZ),
                  prev(B_CU), prev(B_CC), prev(B_PU), nxt(B_CU), nxt(B_CC), nxt(B_PU),
                  pl.BlockSpec((3, BRANCH_W), lambda i: (0, 0)),
                  pl.BlockSpec((len(POOL_WINDOWS), POOL_GROUP, POOL_GROUP), lambda i: (0, 0, 0)),
                  pl.BlockSpec((1, BRANCH_W), lambda i: (0, 0)),
                  pl.BlockSpec((len(POOL_WINDOWS), ROW_TILE, POOL_K), lambda i: (0, 0, 0))],
        out_specs=(pl.BlockSpec((ROW_TILE, BRANCH_W), lambda i: (i, 0)),
                   pl.BlockSpec((ROW_TILE, BRANCH_W), lambda i: (i, 0))),
        compiler_params=_cparams(("parallel",)),
        name="local_mixers",
    )(*([yb] * 12), conv_w, pool_w, pool_scale.reshape(1, BRANCH_W), jnp.asarray(_pool_band(), BF16))


def _mergeout_kernel(*refs, final):
    if final:
        b0, b1, b2, b3, g_ref, wb_ref, wo_ref, x_ref, gt_ref, fg_ref, o_ref = refs
    else:
        b0, b1, b2, b3, g_ref, wb_ref, wo_ref, x_ref, gt_ref, o_ref = refs
    acc = None
    for j, br in enumerate((b0, b1, b2, b3)):
        gate = jax.nn.sigmoid(g_ref[:, j * D_MODEL:(j + 1) * D_MODEL].astype(F32))
        term = gate * jnp.dot(br[...], wb_ref[j], preferred_element_type=F32)
        acc = term if acc is None else acc + term
    y = x_ref[...] + gt_ref[0] * jnp.dot(acc.astype(BF16), wo_ref[...], preferred_element_type=F32)
    if final:
        y = y * lax.rsqrt(jnp.mean(y * y, axis=-1, keepdims=True) + EPS) * fg_ref[...]
    o_ref[...] = y


def _mergeout(branches, yb, w_branch, w_out, layer, xs, gt_t, final_gain, batch, tiles_per_batch):
    m, d = xs.shape
    final = final_gain is not None
    lat = tiles_per_batch - 1
    n_r = lat if final else tiles_per_batch
    resident = pl.Buffered(1)

    def src(b, r):
        return b * tiles_per_batch + r

    br_spec = pl.BlockSpec((ROW_TILE, BRANCH_W), lambda b, r: (src(b, r), 0))
    in_specs = [br_spec, br_spec, br_spec, br_spec,
                pl.BlockSpec((ROW_TILE, N_BRANCH * d), lambda b, r: (src(b, r), B_GATE)),
                pl.BlockSpec((None, N_BRANCH, BRANCH_W, d), lambda b, r: (layer, 0, 0, 0), pipeline_mode=resident),
                pl.BlockSpec((None, d, d), lambda b, r: (layer, 0, 0), pipeline_mode=resident),
                pl.BlockSpec((ROW_TILE, d), lambda b, r: (src(b, r), 0)),
                pl.BlockSpec((1, 1, d), lambda b, r: (src(b, r), 0, 0))]
    args = [*branches, yb, w_branch, w_out, xs, gt_t]
    if final:
        in_specs.append(pl.BlockSpec((1, d), lambda b, r: (0, 0)))
        args.append(final_gain.reshape(1, d))
    return pl.pallas_call(
        functools.partial(_mergeout_kernel, final=final),
        out_shape=jax.ShapeDtypeStruct((batch * n_r * ROW_TILE, d), F32),
        grid=(batch, n_r),
        in_specs=in_specs,
        out_specs=pl.BlockSpec((ROW_TILE, d), lambda b, r: (b * n_r + r, 0)),
        compiler_params=_cparams(("parallel", "parallel")),
        name="mergeout_final" if final else "mergeout",
    )(*args)


def _rope_tables(t_lat):
    pos = np.arange(t_lat)
    quarter = ATT_HEAD_DIM // 4
    freq = ROPE_THETA ** (-jnp.arange(quarter, dtype=F32) / quarter)
    a_row = jnp.asarray(pos // GRID_W, F32)[:, None] * freq[None, :]
    a_col = jnp.asarray(pos % GRID_W, F32)[:, None] * freq[None, :]
    cos_t = jnp.concatenate([jnp.cos(a_row), jnp.cos(a_row), jnp.cos(a_col), jnp.cos(a_col)], axis=-1)
    sin_t = jnp.concatenate([-jnp.sin(a_row), jnp.sin(a_row), -jnp.sin(a_col), jnp.sin(a_col)], axis=-1)
    return cos_t, sin_t


def kernel(x, c, ctx, c_ctx, norm_gain, w_mod, b_mod, w_in, q_norm_gain, k_norm_gain, mlstm_gate_bias,
           mlstm_norm_gain, conv_w, pool_w, pool_scale, w_branch, w_out, final_norm_gain):
    batch, t_lat, d = x.shape
    depth = w_in.shape[0]
    assert d == D_MODEL and ctx.shape[1] == CTX_LEN and t_lat % ROW_TILE == 0 and batch < 8
    s_len = t_lat + CTX_LEN
    tiles_per_batch = s_len // ROW_TILE

    cc = jnp.zeros((8, d), F32).at[:batch].set(c).at[batch].set(c_ctx)
    mod = _modulation(cc, w_mod, b_mod)
    tile_row = np.array([b if r < tiles_per_batch - 1 else batch
                         for b in range(batch) for r in range(tiles_per_batch)], np.int32)
    cos_t, sin_t = _rope_tables(t_lat)
    assert w_in.shape[1:] == (d, N_IN)
    w_t = jnp.swapaxes(w_in, 1, 2).reshape(depth * N_IN, d)
    w_branch_b, w_out_b, pool_w_b = w_branch.astype(BF16), w_out.astype(BF16), pool_w.astype(BF16)

    xs, out = None, None
    for l in range(depth):
        mod_t = mod[l][tile_row][:, None, :]
        sh_t, sc_t, gt_t = mod_t[..., :d], mod_t[..., d:2 * d], mod_t[..., 2 * d:]
        bias = jnp.zeros((1, 128), F32).at[0, :4 * MLSTM_HEADS].set(mlstm_gate_bias[l].reshape(-1))

        if l == 0:
            h, xs = _normmod_first(x, ctx, norm_gain[l], sc_t, sh_t)
        else:
            h = _normmod(xs, norm_gain[l], sc_t, sh_t)
        ya = _inproj(h, w_t, l, 7, lambda j: jnp.where(j == 0, 0, 512 + IN_UNIT * j), "inproj_a")
        yb = _inproj(h, w_t, l, 14, lambda j: jnp.where(j < 8, W_MERGE_ROW + IN_UNIT * j,
                                                       W_LOCAL_ROW + IN_UNIT * (j - 8)), "inproj_b")
        tail = _inproj_tail(h, w_t, l)

        qb, kb, vt = _qkvprep(ya, tail, cos_t, sin_t, q_norm_gain[l], k_norm_gain[l], batch, tiles_per_batch)
        y_att = _attention(qb, ya, kb, vt, batch, s_len)
        g, g_t, pre, pre_t, suf, suf_t = _gateprep(tail, bias)
        h_f = _mlstm_pass(ya, (g, g_t, pre, pre_t), None, None, batch, tiles_per_batch, reverse=False)
        y_mls = _mlstm_pass(ya, (g, g_t, suf, suf_t), mlstm_norm_gain[l], h_f, batch, tiles_per_batch,
                            reverse=True)
        y_conv, y_pool = _local_mixers(yb, conv_w[l], pool_w_b[l], pool_scale[l], tiles_per_batch)

        last = l == depth - 1
        res = _mergeout((y_att, y_mls, y_conv, y_pool), yb, w_branch_b, w_out_b, l, xs, gt_t,
                        final_norm_gain if last else None, batch, tiles_per_batch)
        if last:
            out = res.reshape(batch, t_lat, d)
        else:
            xs = res
    return out
```

```python
import functools

import numpy as np
import jax
import jax.numpy as jnp
from jax import lax
from jax.experimental import pallas as pl
from jax.experimental.pallas import tpu as pltpu

F32 = jnp.float32
BF16 = jnp.bfloat16

D_MODEL = 2048
BRANCH_W = 1024
GRID_W = 64
CTX_LEN = 256
EPS = 1e-6
ATT_HEAD_DIM = 128
ATT_GROUP = 4
ATT_KV_HEADS = 2
ATT_SCALE = ATT_HEAD_DIM ** -0.5
LOG2_E = 1.4426950408889634
ROPE_THETA = 10000.0
MLSTM_HEADS = 4
MLSTM_HEAD_DIM = 256
MLSTM_K_SCALE = MLSTM_HEAD_DIM ** -0.5
M_INIT = -1e30
POOL_WINDOWS = (2, 4, 8, 16)
POOL_GROUP = 256
N_BRANCH = 4

ROW_TILE = 256
HALO = 16
POOL_K = 384
ATT_Q_TILE = 256
ATT_KEY_CHUNK = 256
VT_ROWS = ATT_HEAD_DIM + 16
V7X_VMEM_LIMIT = 56 * 1024 * 1024

A_AQ, A_AZ, A_MQ, A_MK, A_MV, A_MO, A_MZ = range(7)
B_GATE = 0
B_CU, B_CB, B_CC, B_CZ, B_PU, B_PZ = range(8, 14)
N_TAIL = 640


def _cparams(sem, vmem=V7X_VMEM_LIMIT):
    return pltpu.CompilerParams(dimension_semantics=sem, vmem_limit_bytes=vmem)


def _silu(x):
    return x * jax.nn.sigmoid(x)


def _mod_kernel(c_ref, w_ref, b_ref, o_ref):
    a = _silu(c_ref[...]).astype(BF16)
    o_ref[...] = jnp.dot(a, w_ref[...].astype(BF16), preferred_element_type=F32) + b_ref[...]


def _modulation(cc, w_mod, b_mod):
    depth, d, n = w_mod.shape
    tn = 768
    return pl.pallas_call(
        _mod_kernel,
        out_shape=jax.ShapeDtypeStruct((depth, 8, n), F32),
        grid=(depth, n // tn),
        in_specs=[pl.BlockSpec((8, d), lambda l, j: (0, 0)),
                  pl.BlockSpec((None, d, tn), lambda l, j: (l, 0, j)),
                  pl.BlockSpec((None, 1, tn), lambda l, j: (l, 0, j))],
        out_specs=pl.BlockSpec((None, 8, tn), lambda l, j: (l, 0, j)),
        compiler_params=_cparams(("parallel", "parallel")),
        name="modulation",
    )(cc, w_mod, b_mod.reshape(depth, 1, n))


def _normmod_kernel(x_ref, g_ref, sc_ref, sh_ref, o_ref):
    x = x_ref[...]
    y = x * lax.rsqrt(jnp.mean(x * x, axis=-1, keepdims=True) + EPS) * g_ref[...]
    o_ref[...] = (y * (1.0 + sc_ref[0]) + sh_ref[0]).astype(BF16)


def _normmod(xs, gain, sc_t, sh_t):
    m, d = xs.shape
    nt = m // ROW_TILE
    return pl.pallas_call(
        _normmod_kernel,
        out_shape=jax.ShapeDtypeStruct((m, d), BF16),
        grid=(nt,),
        in_specs=[pl.BlockSpec((ROW_TILE, d), lambda i: (i, 0)),
                  pl.BlockSpec((1, d), lambda i: (0, 0)),
                  pl.BlockSpec((1, 1, d), lambda i: (i, 0, 0)),
                  pl.BlockSpec((1, 1, d), lambda i: (i, 0, 0))],
        out_specs=pl.BlockSpec((ROW_TILE, d), lambda i: (i, 0)),
        compiler_params=_cparams(("parallel",)),
        name="normmod",
    )(xs, gain.reshape(1, d), sc_t, sh_t)


def _normmod_first_kernel(x_ref, c_ref, g_ref, sc_ref, sh_ref, o_ref, xs_ref, *, tiles_per_batch):
    x = jnp.where(pl.program_id(1) == tiles_per_batch - 1, c_ref[...], x_ref[...])
    xs_ref[...] = x
    y = x * lax.rsqrt(jnp.mean(x * x, axis=-1, keepdims=True) + EPS) * g_ref[...]
    o_ref[...] = (y * (1.0 + sc_ref[0]) + sh_ref[0]).astype(BF16)


def _normmod_first(x, ctx, gain, sc_t, sh_t):
    batch, t_lat, d = x.shape
    tpb = t_lat // ROW_TILE + 1
    m = batch * tpb * ROW_TILE
    tab_spec = pl.BlockSpec((1, 1, d), lambda b, r: (b * tpb + r, 0, 0))
    row_spec = pl.BlockSpec((ROW_TILE, d), lambda b, r: (b * tpb + r, 0))
    return pl.pallas_call(
        functools.partial(_normmod_first_kernel, tiles_per_batch=tpb),
        out_shape=(jax.ShapeDtypeStruct((m, d), BF16), jax.ShapeDtypeStruct((m, d), F32)),
        grid=(batch, tpb),
        in_specs=[pl.BlockSpec((None, ROW_TILE, d), lambda b, r: (b, jnp.minimum(r, tpb - 2), 0)),
                  pl.BlockSpec((None, CTX_LEN, d), lambda b, r: (b, 0, 0)),
                  pl.BlockSpec((1, d), lambda b, r: (0, 0)),
                  tab_spec, tab_spec],
        out_specs=(row_spec, row_spec),
        compiler_params=_cparams(("parallel", "parallel")),
        name="normmod_first",
    )(x, ctx, gain.reshape(1, d), sc_t, sh_t)


IN_UNIT = 1024
W_KV_ROW, W_MGATE_ROW, W_LOCAL_ROW, W_MERGE_ROW = 1024, 7680, 7696, 13840
N_IN = W_MERGE_ROW + N_BRANCH * D_MODEL


def _pick_tile(n, candidates):
    for t in candidates:
        if n % t == 0:
            return t
    raise ValueError(f"no tile for {n}")


def _inproj_kernel(h_ref, w_ref, o_ref, wt_s):
    @pl.when(pl.program_id(1) == 0)
    def _():
        wt_s[...] = w_ref[...].T.astype(BF16)

    o_ref[...] = jnp.dot(h_ref[...], wt_s[...], preferred_element_type=F32).astype(o_ref.dtype)


def _inproj(h, w_t, layer, n_units, unit_row, name):
    m, k = h.shape
    tm = _pick_tile(m, (1024, 512, 256))
    base = layer * N_IN
    return pl.pallas_call(
        _inproj_kernel,
        out_shape=jax.ShapeDtypeStruct((m, n_units * IN_UNIT), BF16),
        grid=(n_units, m // tm),
        in_specs=[pl.BlockSpec((tm, k), lambda j, i: (i, 0)),
                  pl.BlockSpec((pl.Element(IN_UNIT), pl.Element(k)),
                               lambda j, i: (pl.multiple_of(base + unit_row(j), 8), 0))],
        out_specs=pl.BlockSpec((tm, IN_UNIT), lambda j, i: (i, j)),
        scratch_shapes=[pltpu.VMEM((k, IN_UNIT), BF16)],
        compiler_params=_cparams(("parallel", "arbitrary")),
        name=name,
    )(h, w_t)


def _inproj_tail_kernel(h_ref, wkv_ref, wg_ref, o_ref):
    nt = (((1,), (1,)), ((), ()))
    h = h_ref[...]
    kvw = wkv_ref.shape[0]
    o_ref[:, :kvw] = lax.dot_general(h, wkv_ref[...].astype(BF16), nt, preferred_element_type=F32)
    g = lax.dot_general(h, wg_ref[...].astype(BF16), nt, preferred_element_type=F32)
    lane = lax.broadcasted_iota(jnp.int32, g.shape, 1)
    o_ref[:, kvw:] = jnp.where(lane < 4 * MLSTM_HEADS, g, 0.0)


def _inproj_tail(h, w_t, layer):
    m, k = h.shape
    tm = _pick_tile(m, (1024, 512, 256))
    kvw = 2 * ATT_KV_HEADS * ATT_HEAD_DIM
    return pl.pallas_call(
        _inproj_tail_kernel,
        out_shape=jax.ShapeDtypeStruct((m, N_TAIL), F32),
        grid=(m // tm,),
        in_specs=[pl.BlockSpec((tm, k), lambda i: (i, 0)),
                  pl.BlockSpec((pl.Element(kvw), pl.Element(k)), lambda i: (layer * N_IN + W_KV_ROW, 0)),
                  pl.BlockSpec((pl.Element(128), pl.Element(k)), lambda i: (layer * N_IN + W_MGATE_ROW, 0))],
        out_specs=pl.BlockSpec((tm, N_TAIL), lambda i: (i, 0)),
        compiler_params=_cparams(("parallel",)),
        name="inproj_tail",
    )(h, w_t, w_t)


def _qkvprep_kernel(q_ref, t_ref, cos_ref, sin_ref, qg_ref, kg_ref, q_out, k_out, vt_out, *, tiles_per_batch):
    is_lat = pl.program_id(1) < tiles_per_batch - 1
    cos, sin, kg = cos_ref[...], sin_ref[...], kg_ref[...]
    qg = qg_ref[...] * (ATT_SCALE * LOG2_E)
    hd = ATT_HEAD_DIM
    n_q = ATT_KV_HEADS * ATT_GROUP
    sls = [slice(h * hd, (h + 1) * hd) for h in range(n_q)]
    xs = [q_ref[:, sl].astype(F32) for sl in sls] + [t_ref[:, sl] for sl in sls[:ATT_KV_HEADS]]
    gains = [qg] * n_q + [kg] * ATT_KV_HEADS
    xs = [x * lax.rsqrt(jnp.mean(x * x, axis=-1, keepdims=True) + EPS) * g for x, g in zip(xs, gains)]
    lane = lax.broadcasted_iota(jnp.int32, xs[0].shape, 1)
    low = (lane % 64) < 32
    up = [pltpu.roll(x, 96, 1) for x in xs]
    down = [pltpu.roll(x, 32, 1) for x in xs]
    outs = [jnp.where(is_lat, x * cos + jnp.where(low, a, b) * sin, x).astype(BF16)
            for x, a, b in zip(xs, up, down)]
    for h in range(n_q):
        q_out[:, sls[h]] = outs[h]
    for h in range(ATT_KV_HEADS):
        k_out[:, sls[h]] = outs[n_q + h]
        vt_out[h, 0:hd, :] = t_ref[:, sls[ATT_KV_HEADS + h]].T.astype(BF16)
        vt_out[h, hd:, :] = jnp.ones((VT_ROWS - hd, ROW_TILE), BF16)


def _qkvprep(ya, tail, cos_t, sin_t, q_gain, k_gain, batch, tiles_per_batch):
    m = tail.shape[0]
    lat_tiles = tiles_per_batch - 1
    kvw = ATT_KV_HEADS * ATT_HEAD_DIM
    rope_spec = pl.BlockSpec((ROW_TILE, ATT_HEAD_DIM), lambda b, r: (jnp.minimum(r, lat_tiles - 1), 0))
    gain_spec = pl.BlockSpec((1, ATT_HEAD_DIM), lambda b, r: (0, 0))
    return pl.pallas_call(
        functools.partial(_qkvprep_kernel, tiles_per_batch=tiles_per_batch),
        out_shape=(jax.ShapeDtypeStruct((m, BRANCH_W), BF16),
                   jax.ShapeDtypeStruct((m, kvw), BF16),
                   jax.ShapeDtypeStruct((batch, ATT_KV_HEADS, VT_ROWS, tiles_per_batch * ROW_TILE), BF16)),
        grid=(batch, tiles_per_batch),
        in_specs=[pl.BlockSpec((ROW_TILE, BRANCH_W), lambda b, r: (b * tiles_per_batch + r, A_AQ)),
                  pl.BlockSpec((ROW_TILE, 2 * kvw), lambda b, r: (b * tiles_per_batch + r, 0)),
                  rope_spec, rope_spec, gain_spec, gain_spec],
        out_specs=(pl.BlockSpec((ROW_TILE, BRANCH_W), lambda b, r: (b * tiles_per_batch + r, 0)),
                   pl.BlockSpec((ROW_TILE, kvw), lambda b, r: (b * tiles_per_batch + r, 0)),
                   pl.BlockSpec((None, ATT_KV_HEADS, VT_ROWS, ROW_TILE), lambda b, r: (b, 0, 0, r))),
        compiler_params=_cparams(("parallel", "parallel")),
        name="qkvprep",
    )(ya, tail, cos_t, sin_t, q_gain.reshape(1, ATT_HEAD_DIM), k_gain.reshape(1, ATT_HEAD_DIM))


def _attn_kernel(q_ref, z_ref, k_ref, vt_ref, o_ref, *, n_lat_q, t_lat, kc):
    is_lat = pl.program_id(2) < n_lat_q
    hd = ATT_HEAD_DIM
    q = jnp.concatenate([q_ref[:, g * hd:(g + 1) * hd] for g in range(ATT_GROUP)], axis=0)

    def attend(chunks):
        def scores(c):
            off, n = chunks[c]
            return lax.dot_general(k_ref[off:off + n, :], q, (((1,), (1,)), ((), ())),
                                   preferred_element_type=F32)

        m, o = None, None
        s_next = scores(0)
        for c, (off, n) in enumerate(chunks):
            s = s_next
            if c + 1 < len(chunks):
                s_next = scores(c + 1)
            mc = jnp.max(s, axis=0, keepdims=True)
            m_new = mc if m is None else jnp.maximum(m, mc)
            p = jnp.exp2((s - m_new).astype(BF16))
            oc = jnp.dot(vt_ref[:, off:off + n], p, preferred_element_type=F32)
            o = oc if o is None else jnp.exp2(m - m_new) * o + oc
            m = m_new
        o = o[0:hd] / o[hd:hd + 1]
        for g in range(ATT_GROUP):
            sl = slice(g * ATT_HEAD_DIM, (g + 1) * ATT_HEAD_DIM)
            og = o[:, g * ATT_Q_TILE:(g + 1) * ATT_Q_TILE].T
            o_ref[:, sl] = (og * _silu(z_ref[:, sl].astype(F32))).astype(BF16)

    ctx_chunk = (t_lat, CTX_LEN)

    @pl.when(is_lat)
    def _():
        attend([(off, kc) for off in range(0, t_lat, kc)] + [ctx_chunk])

    @pl.when(jnp.logical_not(is_lat))
    def _():
        attend([ctx_chunk])


def _attention(qb, ya, kb, vt, batch, s_len):
    m = ya.shape[0]
    t_lat = s_len - CTX_LEN
    nq = s_len // ATT_Q_TILE
    n_lat_q = t_lat // ATT_Q_TILE
    gw = ATT_GROUP * ATT_HEAD_DIM
    return pl.pallas_call(
        functools.partial(_attn_kernel, n_lat_q=n_lat_q, t_lat=t_lat,
                          kc=_pick_tile(t_lat, (ATT_KEY_CHUNK, 512, 256))),
        out_shape=jax.ShapeDtypeStruct((m, BRANCH_W), BF16),
        grid=(batch, ATT_KV_HEADS, nq),
        in_specs=[pl.BlockSpec((ATT_Q_TILE, gw), lambda b, h, i: (b * nq + i, h)),
                  pl.BlockSpec((ATT_Q_TILE, gw), lambda b, h, i: (b * nq + i, A_AZ * 2 + h)),
                  pl.BlockSpec((s_len, ATT_HEAD_DIM), lambda b, h, i: (b, h)),
                  pl.BlockSpec((None, None, VT_ROWS, s_len), lambda b, h, i: (b, h, 0, 0))],
        out_specs=pl.BlockSpec((ATT_Q_TILE, gw), lambda b, h, i: (b * nq + i, h)),
        compiler_params=_cparams(("parallel", "parallel", "arbitrary")),
        name="attention",
    )(qb, ya, kb, vt)


def _gateprep_kernel(g_ref, bias_ref, g_out, gt_out, pre_out, pret_out, suf_out, suft_out):
    L = ROW_TILE
    gates = g_ref[...] + bias_ref[...]
    log_f = jnp.minimum(gates, 0.0) - jnp.log1p(jnp.exp(-jnp.abs(gates)))
    row = lax.broadcasted_iota(jnp.int32, (L, L), 0)
    col = lax.broadcasted_iota(jnp.int32, (L, L), 1)
    hi = lax.Precision.HIGHEST
    pre = jnp.dot((col <= row).astype(F32), log_f, precision=hi, preferred_element_type=F32)
    suf = jnp.dot((col >= row).astype(F32), log_f, precision=hi, preferred_element_type=F32)
    g_out[...] = gates
    gt_out[...] = gates.T
    pre_out[...] = pre
    pret_out[...] = pre.T
    suf_out[...] = suf
    suft_out[...] = suf.T


def _gateprep(tail, bias):
    m = tail.shape[0]
    nt = m // ROW_TILE
    row_sd, col_sd = jax.ShapeDtypeStruct((m, 128), F32), jax.ShapeDtypeStruct((128, m), F32)
    row_spec = pl.BlockSpec((ROW_TILE, 128), lambda i: (i, 0))
    col_spec = pl.BlockSpec((128, ROW_TILE), lambda i: (0, i))
    return pl.pallas_call(
        _gateprep_kernel,
        out_shape=(row_sd, col_sd, row_sd, col_sd, row_sd, col_sd),
        grid=(nt,),
        in_specs=[pl.BlockSpec((ROW_TILE, 128), lambda i: (i, (N_TAIL - 128) // 128)),
                  pl.BlockSpec((1, 128), lambda i: (0, 0))],
        out_specs=(row_spec, col_spec, row_spec, col_spec, row_spec, col_spec),
        compiler_params=_cparams(("parallel",)),
        name="gateprep",
    )(tail, bias)


def _mlstm_kernel(*refs, reverse, final):
    if final:
        (q_ref, k_ref, v_ref, g_ref, gt_ref, bc_ref, bct_ref, o_ref, z_ref, gain_ref, hprev_ref,
         out_ref, c_s, n_s, m_s) = refs
    else:
        q_ref, k_ref, v_ref, g_ref, gt_ref, bc_ref, bct_ref, out_ref, c_s, n_s, m_s = refs
    L = ROW_TILE
    hd = MLSTM_HEAD_DIM

    @pl.when(pl.program_id(1) == 0)
    def _():
        c_s[...] = jnp.zeros(c_s.shape, F32)
        n_s[...] = jnp.zeros(n_s.shape, F32)
        m_s[...] = jnp.full(m_s.shape, M_INIT, F32)

    gates, gates_t = g_ref[...], gt_ref[...]
    bcum, bcum_t = bc_ref[...], bct_ref[...]
    row = lax.broadcasted_iota(jnp.int32, (L, L), 0)
    col = lax.broadcasted_iota(jnp.int32, (L, L), 1)
    tri = (col >= row) if reverse else (col <= row)
    last = 0 if reverse else L - 1
    heads = range(MLSTM_HEADS)
    hsl = [slice(h * hd, (h + 1) * hd) for h in heads]
    icol = [(2 if reverse else 0) * MLSTM_HEADS + h for h in heads]
    fcol = [(3 if reverse else 1) * MLSTM_HEADS + h for h in heads]
    c_old = [c_s[h] for h in heads]
    n_old = [n_s[h, 0:1, :] for h in heads]
    m_old = [m_s[h, 0:1, 0:1] for h in heads]
    q = [q_ref[:, hsl[h]] for h in heads]
    k = [k_ref[:, hsl[h]].astype(F32) * MLSTM_K_SCALE for h in heads]
    v = [v_ref[:, hsl[h]] for h in heads]
    b_col = [bcum[:, fcol[h]:fcol[h] + 1] for h in heads]
    i_col = [gates[:, icol[h]:icol[h] + 1] for h in heads]

    qk = [lax.dot_general(q[h], k[h].astype(BF16), (((1,), (1,)), ((), ())), preferred_element_type=F32)
          for h in heads]
    qc = [jnp.dot(q[h], c_old[h].astype(BF16), preferred_element_type=F32) for h in heads]

    m_t, dmat, inter = [], [], []
    for h in heads:
        b_row, i_row = bcum_t[fcol[h]:fcol[h] + 1, :], gates_t[icol[h]:icol[h] + 1, :]
        log_d = jnp.where(tri, b_col[h] - b_row + i_row, -jnp.inf)
        log_inter = b_col[h] + m_old[h]
        m_t.append(jnp.maximum(log_inter, jnp.max(log_d, axis=1, keepdims=True)))
        dmat.append(jnp.exp(log_d - m_t[h]))
        inter.append(jnp.exp(log_inter - m_t[h]))

    sm = [qk[h] * dmat[h] for h in heads]
    sv = [jnp.dot(sm[h].astype(BF16), v[h], preferred_element_type=F32) for h in heads]

    kw, decay, m_new = [], [], []
    for h in heads:
        b_last = bcum[last:last + 1, fcol[h]:fcol[h] + 1]
        log_w = b_last - b_col[h] + i_col[h]
        m_new.append(jnp.maximum(b_last + m_old[h], jnp.max(log_w, axis=0, keepdims=True)))
        decay.append(jnp.exp(b_last + m_old[h] - m_new[h]))
        kw.append(k[h] * jnp.exp(log_w - m_new[h]))
    kv = [lax.dot_general(kw[h].astype(BF16), v[h], (((0,), (0,)), ((), ())), preferred_element_type=F32)
          for h in heads]

    for h in heads:
        num = inter[h] * qc[h] + sv[h]
        den = inter[h] * jnp.sum(q[h].astype(F32) * n_old[h], axis=1, keepdims=True) \
            + jnp.sum(sm[h], axis=1, keepdims=True)
        hh = num / jnp.maximum(jnp.abs(den), jnp.exp(-m_t[h]))
        if final:
            ht = hh + hprev_ref[:, hsl[h]]
            hn = ht * lax.rsqrt(jnp.mean(ht * ht, axis=-1, keepdims=True) + EPS)
            y = hn * gain_ref[:, hsl[h]] * jax.nn.sigmoid(o_ref[:, hsl[h]].astype(F32)) \
                * _silu(z_ref[:, hsl[h]].astype(F32))
            out_ref[:, hsl[h]] = y.astype(BF16)
        else:
            out_ref[:, hsl[h]] = hh

    for h in heads:
        c_s[h] = decay[h] * c_old[h] + kv[h]
        n_s[h, 0:1, :] = decay[h] * n_old[h] + jnp.sum(kw[h], axis=0, keepdims=True)
        m_s[h] = jnp.broadcast_to(m_new[h], m_s.shape[1:])


def _mlstm_pass(ya, gate_arrays, gain, hprev, batch, tiles_per_batch, reverse):
    m = ya.shape[0]
    lat = tiles_per_batch - 1
    final = hprev is not None

    def rows(b, c):
        r = jnp.where(c == 0, lat, (lat - c) if reverse else (c - 1))
        return b * tiles_per_batch + r

    def col(cidx):
        return pl.BlockSpec((ROW_TILE, BRANCH_W), lambda b, c: (rows(b, c), cidx))

    gate_spec = pl.BlockSpec((ROW_TILE, 128), lambda b, c: (rows(b, c), 0))
    gate_t_spec = pl.BlockSpec((128, ROW_TILE), lambda b, c: (0, rows(b, c)))
    in_specs = [col(A_MQ), col(A_MK), col(A_MV), gate_spec, gate_t_spec, gate_spec, gate_t_spec]
    args = [ya, ya, ya, *gate_arrays]
    if final:
        in_specs += [col(A_MO), col(A_MZ), pl.BlockSpec((1, BRANCH_W), lambda b, c: (0, 0)),
                     pl.BlockSpec((ROW_TILE, BRANCH_W), lambda b, c: (rows(b, c), 0))]
        args += [ya, ya, gain.reshape(1, BRANCH_W), hprev]
    return pl.pallas_call(
        functools.partial(_mlstm_kernel, reverse=reverse, final=final),
        out_shape=jax.ShapeDtypeStruct((m, BRANCH_W), BF16 if final else F32),
        grid=(batch, tiles_per_batch),
        in_specs=in_specs,
        out_specs=pl.BlockSpec((ROW_TILE, BRANCH_W), lambda b, c: (rows(b, c), 0)),
        scratch_shapes=[pltpu.VMEM((MLSTM_HEADS, MLSTM_HEAD_DIM, MLSTM_HEAD_DIM), F32),
                        pltpu.VMEM((MLSTM_HEADS, 8, MLSTM_HEAD_DIM), F32),
                        pltpu.VMEM((MLSTM_HEADS, 8, 128), F32)],
        compiler_params=_cparams(("parallel", "arbitrary")),
        name="mlstm_bwd" if reverse else "mlstm_fwd",
    )(*args)


def _local_branches(r, lat, cu_ref, cb_ref, cc_ref, cz_ref, pu_ref, pz_ref,
                    cu_p, cc_p, pu_p, cu_n, cc_n, pu_n,
                    cw_ref, pw_ref, ps_ref, band_ref, yc_ref, yd_ref):
    has_prev = jnp.logical_and(r != 0, r != lat)
    has_next = jnp.logical_and(r != lat - 1, r != lat)
    seg_len = jnp.where(r == lat, CTX_LEN, lat * ROW_TILE)
    t0 = jnp.where(r == lat, 0, r * ROW_TILE)
    rowi = lax.broadcasted_iota(jnp.int32, (ROW_TILE, 1), 0)

    for j in range(BRANCH_W // 128):
        cs = slice(j * 128, (j + 1) * 128)
        a = cc_ref[:, cs].astype(F32) * cu_ref[:, cs].astype(F32)
        a_prev = jnp.where(has_prev, cc_p[HALO - 1:HALO, cs].astype(F32) * cu_p[HALO - 1:HALO, cs].astype(F32), 0.0)
        a_next = jnp.where(has_next, cc_n[0:1, cs].astype(F32) * cu_n[0:1, cs].astype(F32), 0.0)
        a_m1 = jnp.where(rowi == 0, a_prev, pltpu.roll(a, 1, 0))
        a_p1 = jnp.where(rowi == ROW_TILE - 1, a_next, pltpu.roll(a, ROW_TILE - 1, 0))
        y = cw_ref[0:1, cs] * a_m1 + cw_ref[1:2, cs] * a + cw_ref[2:3, cs] * a_p1
        yc_ref[:, cs] = (cb_ref[:, cs].astype(F32) * y * _silu(cz_ref[:, cs].astype(F32))).astype(BF16)

    u = pu_ref[...]
    halo_zero = jnp.zeros((HALO, BRANCH_W), u.dtype)
    ext = jnp.concatenate([u, jnp.where(has_prev, pu_p[...], halo_zero), jnp.where(has_next, pu_n[...], halo_zero),
                           jnp.zeros((POOL_K - ROW_TILE - 2 * HALO, BRANCH_W), u.dtype)], axis=0)
    t = t0 + rowi
    groups = range(len(POOL_WINDOWS))
    gsl = [slice(g * POOL_GROUP, (g + 1) * POOL_GROUP) for g in groups]
    acc = [jnp.dot(band_ref[g], ext[:, gsl[g]], preferred_element_type=F32) for g in groups]
    dev = []
    for g, w in enumerate(POOL_WINDOWS):
        inv_cnt = 1.0 / (jnp.minimum(t + (w - w // 2), seg_len) - jnp.maximum(t - w // 2, 0)).astype(F32)
        dev.append((acc[g] * inv_cnt - u[:, gsl[g]].astype(F32)).astype(BF16))
    pg = [jnp.dot(dev[g], pw_ref[g], preferred_element_type=F32) for g in groups]
    for g in groups:
        gate = _silu(pz_ref[:, gsl[g]].astype(F32)) * ps_ref[:, gsl[g]]
        yd_ref[:, gsl[g]] = (pg[g] * gate).astype(BF16)


def _pool_band():
    t = np.arange(ROW_TILE)[:, None]
    pos = np.concatenate([np.arange(ROW_TILE), np.arange(-HALO, 0), np.arange(ROW_TILE, ROW_TILE + HALO),
                          np.full(POOL_K - ROW_TILE - 2 * HALO, -10 * ROW_TILE)])[None, :]
    return np.stack([(pos >= t - w // 2) & (pos < t + w - w // 2) for w in POOL_WINDOWS]).astype(np.float32)


N_LOCAL_REFS = 16


def _mergeout_kernel(*refs, final, tiles_per_batch):
    b0, b1 = refs[:2]
    local_refs = refs[2:2 + N_LOCAL_REFS]
    rest = refs[2 + N_LOCAL_REFS:]
    if final:
        g_ref, wb_ref, wo_ref, x_ref, gt_ref, fg_ref, o_ref, yc_s, yd_s = rest
    else:
        g_ref, wb_ref, wo_ref, x_ref, gt_ref, o_ref, yc_s, yd_s = rest

    def gated(j, br):
        gate = jax.nn.sigmoid(g_ref[:, j * D_MODEL:(j + 1) * D_MODEL].astype(F32))
        return gate * jnp.dot(br, wb_ref[j], preferred_element_type=F32)

    acc = gated(0, b0[...]) + gated(1, b1[...])
    _local_branches(pl.program_id(1), tiles_per_batch - 1, *local_refs, yc_s, yd_s)
    acc = acc + gated(2, yc_s[...]) + gated(3, yd_s[...])
    y = x_ref[...] + gt_ref[0] * jnp.dot(acc.astype(BF16), wo_ref[...], preferred_element_type=F32)
    if final:
        y = y * lax.rsqrt(jnp.mean(y * y, axis=-1, keepdims=True) + EPS) * fg_ref[...]
    o_ref[...] = y


def _mergeout(y_att, y_mls, yb, conv_w, pool_w, pool_scale, w_branch, w_out, layer, xs, gt_t, final_gain,
              batch, tiles_per_batch):
    m, d = xs.shape
    final = final_gain is not None
    lat = tiles_per_batch - 1
    n_r = lat if final else tiles_per_batch
    resident = pl.Buffered(1)
    hb = ROW_TILE // HALO
    n_halo = m // HALO
    n_win = len(POOL_WINDOWS)

    def src(b, r):
        return b * tiles_per_batch + r

    def col(cidx):
        return pl.BlockSpec((ROW_TILE, BRANCH_W), lambda b, r: (src(b, r), cidx))

    def prev(cidx):
        return pl.BlockSpec((HALO, BRANCH_W), lambda b, r: (jnp.maximum(src(b, r) * hb - 1, 0), cidx))

    def nxt(cidx):
        return pl.BlockSpec((HALO, BRANCH_W), lambda b, r: (jnp.minimum((src(b, r) + 1) * hb, n_halo - 1), cidx))

    in_specs = [col(0), col(0),
                col(B_CU), col(B_CB), col(B_CC), col(B_CZ), col(B_PU), col(B_PZ),
                prev(B_CU), prev(B_CC), prev(B_PU), nxt(B_CU), nxt(B_CC), nxt(B_PU),
                pl.BlockSpec((None, 3, BRANCH_W), lambda b, r: (layer, 0, 0)),
                pl.BlockSpec((None, n_win, POOL_GROUP, POOL_GROUP), lambda b, r: (layer, 0, 0, 0)),
                pl.BlockSpec((None, 1, BRANCH_W), lambda b, r: (layer, 0, 0)),
                pl.BlockSpec((n_win, ROW_TILE, POOL_K), lambda b, r: (0, 0, 0)),
                pl.BlockSpec((ROW_TILE, N_BRANCH * d), lambda b, r: (src(b, r), B_GATE)),
                pl.BlockSpec((None, N_BRANCH, BRANCH_W, d), lambda b, r: (layer, 0, 0, 0), pipeline_mode=resident),
                pl.BlockSpec((None, d, d), lambda b, r: (layer, 0, 0), pipeline_mode=resident),
                pl.BlockSpec((ROW_TILE, d), lambda b, r: (src(b, r), 0)),
                pl.BlockSpec((1, 1, d), lambda b, r: (src(b, r), 0, 0))]
    args = [y_att, y_mls, *([yb] * 12), conv_w, pool_w, pool_scale.reshape(-1, 1, BRANCH_W),
            jnp.asarray(_pool_band(), BF16), yb, w_branch, w_out, xs, gt_t]
    if final:
        in_specs.append(pl.BlockSpec((1, d), lambda b, r: (0, 0)))
        args.append(final_gain.reshape(1, d))
    return pl.pallas_call(
        functools.partial(_mergeout_kernel, final=final, tiles_per_batch=tiles_per_batch),
        out_shape=jax.ShapeDtypeStruct((batch * n_r * ROW_TILE, d), F32),
        grid=(batch, n_r),
        in_specs=in_specs,
        out_specs=pl.BlockSpec((ROW_TILE, d), lambda b, r: (b * n_r + r, 0)),
        scratch_shapes=[pltpu.VMEM((ROW_TILE, BRANCH_W), BF16), pltpu.VMEM((ROW_TILE, BRANCH_W), BF16)],
        compiler_params=_cparams(("parallel", "parallel")),
        name="mergeout_final" if final else "mergeout",
    )(*args)


def _rope_tables(t_lat):
    pos = np.arange(t_lat)
    quarter = ATT_HEAD_DIM // 4
    freq = ROPE_THETA ** (-jnp.arange(quarter, dtype=F32) / quarter)
    a_row = jnp.asarray(pos // GRID_W, F32)[:, None] * freq[None, :]
    a_col = jnp.asarray(pos % GRID_W, F32)[:, None] * freq[None, :]
    cos_t = jnp.concatenate([jnp.cos(a_row), jnp.cos(a_row), jnp.cos(a_col), jnp.cos(a_col)], axis=-1)
    sin_t = jnp.concatenate([-jnp.sin(a_row), jnp.sin(a_row), -jnp.sin(a_col), jnp.sin(a_col)], axis=-1)
    return cos_t, sin_t


def kernel(x, c, ctx, c_ctx, norm_gain, w_mod, b_mod, w_in, q_norm_gain, k_norm_gain, mlstm_gate_bias,
           mlstm_norm_gain, conv_w, pool_w, pool_scale, w_branch, w_out, final_norm_gain):
    batch, t_lat, d = x.shape
    depth = w_in.shape[0]
    assert d == D_MODEL and ctx.shape[1] == CTX_LEN and t_lat % ROW_TILE == 0 and batch < 8
    s_len = t_lat + CTX_LEN
    tiles_per_batch = s_len // ROW_TILE

    cc = jnp.zeros((8, d), F32).at[:batch].set(c).at[batch].set(c_ctx)
    mod = _modulation(cc, w_mod, b_mod)
    tile_row = np.array([b if r < tiles_per_batch - 1 else batch
                         for b in range(batch) for r in range(tiles_per_batch)], np.int32)
    cos_t, sin_t = _rope_tables(t_lat)
    assert w_in.shape[1:] == (d, N_IN)
    w_t = jnp.swapaxes(w_in, 1, 2).reshape(depth * N_IN, d)
    w_branch_b, w_out_b, pool_w_b = w_branch.astype(BF16), w_out.astype(BF16), pool_w.astype(BF16)

    xs, out = None, None
    for l in range(depth):
        mod_t = mod[l][tile_row][:, None, :]
        sh_t, sc_t, gt_t = mod_t[..., :d], mod_t[..., d:2 * d], mod_t[..., 2 * d:]
        bias = jnp.zeros((1, 128), F32).at[0, :4 * MLSTM_HEADS].set(mlstm_gate_bias[l].reshape(-1))

        if l == 0:
            h, xs = _normmod_first(x, ctx, norm_gain[l], sc_t, sh_t)
        else:
            h = _normmod(xs, norm_gain[l], sc_t, sh_t)
        ya = _inproj(h, w_t, l, 7, lambda j: jnp.where(j == 0, 0, 512 + IN_UNIT * j), "inproj_a")
        yb = _inproj(h, w_t, l, 14, lambda j: jnp.where(j < 8, W_MERGE_ROW + IN_UNIT * j,
                                                       W_LOCAL_ROW + IN_UNIT * (j - 8)), "inproj_b")
        tail = _inproj_tail(h, w_t, l)

        qb, kb, vt = _qkvprep(ya, tail, cos_t, sin_t, q_norm_gain[l], k_norm_gain[l], batch, tiles_per_batch)
        y_att = _attention(qb, ya, kb, vt, batch, s_len)
        g, g_t, pre, pre_t, suf, suf_t = _gateprep(tail, bias)
        h_f = _mlstm_pass(ya, (g, g_t, pre, pre_t), None, None, batch, tiles_per_batch, reverse=False)
        y_mls = _mlstm_pass(ya, (g, g_t, suf, suf_t), mlstm_norm_gain[l], h_f, batch, tiles_per_batch,
                            reverse=True)
        last = l == depth - 1
        res = _mergeout(y_att, y_mls, yb, conv_w, pool_w_b, pool_scale, w_branch_b, w_out_b, l, xs, gt_t,
                        final_norm_gain if last else None, batch, tiles_per_batch)
        if last:
            out = res.reshape(batch, t_lat, d)
        else:
            xs = res
    return out
```

```python
import functools

import numpy as np
import jax
import jax.numpy as jnp
from jax import lax
from jax.experimental import pallas as pl
from jax.experimental.pallas import tpu as pltpu

F32 = jnp.float32
BF16 = jnp.bfloat16

D_MODEL = 2048
BRANCH_W = 1024
GRID_W = 64
CTX_LEN = 256
EPS = 1e-6
ATT_HEAD_DIM = 128
ATT_GROUP = 4
ATT_KV_HEADS = 2
ATT_SCALE = ATT_HEAD_DIM ** -0.5
LOG2_E = 1.4426950408889634
ROPE_THETA = 10000.0
MLSTM_HEADS = 4
MLSTM_HEAD_DIM = 256
MLSTM_K_SCALE = MLSTM_HEAD_DIM ** -0.5
M_INIT = -1e30
POOL_WINDOWS = (2, 4, 8, 16)
POOL_GROUP = 256
N_BRANCH = 4

ROW_TILE = 256
HALO = 16
POOL_K = 384
ATT_Q_TILE = 256
ATT_KEY_CHUNK = 256
VT_ROWS = ATT_HEAD_DIM + 16
V7X_VMEM_LIMIT = 56 * 1024 * 1024

A_AQ, A_AZ, A_MQ, A_MK, A_MV, A_MO, A_MZ = range(7)
B_GATE = 0
B_CU, B_CB, B_CC, B_CZ, B_PU, B_PZ = range(8, 14)


def _cparams(sem, vmem=V7X_VMEM_LIMIT):
    return pltpu.CompilerParams(dimension_semantics=sem, vmem_limit_bytes=vmem)


def _silu(x):
    return x * jax.nn.sigmoid(x)


def _mod_kernel(c_ref, w_ref, b_ref, o_ref):
    a = _silu(c_ref[...]).astype(BF16)
    o_ref[...] = jnp.dot(a, w_ref[...].astype(BF16), preferred_element_type=F32) + b_ref[...]


def _modulation(cc, w_mod, b_mod):
    depth, d, n = w_mod.shape
    tn = 768
    return pl.pallas_call(
        _mod_kernel,
        out_shape=jax.ShapeDtypeStruct((depth, 8, n), F32),
        grid=(depth, n // tn),
        in_specs=[pl.BlockSpec((8, d), lambda l, j: (0, 0)),
                  pl.BlockSpec((None, d, tn), lambda l, j: (l, 0, j)),
                  pl.BlockSpec((None, 1, tn), lambda l, j: (l, 0, j))],
        out_specs=pl.BlockSpec((None, 8, tn), lambda l, j: (l, 0, j)),
        compiler_params=_cparams(("parallel", "parallel")),
        name="modulation",
    )(cc, w_mod, b_mod.reshape(depth, 1, n))


def _normmod_kernel(x_ref, g_ref, sc_ref, sh_ref, o_ref):
    x = x_ref[...]
    y = x * lax.rsqrt(jnp.mean(x * x, axis=-1, keepdims=True) + EPS) * g_ref[...]
    o_ref[...] = (y * (1.0 + sc_ref[0]) + sh_ref[0]).astype(BF16)


def _normmod(xs, gain, sc_t, sh_t):
    m, d = xs.shape
    nt = m // ROW_TILE
    return pl.pallas_call(
        _normmod_kernel,
        out_shape=jax.ShapeDtypeStruct((m, d), BF16),
        grid=(nt,),
        in_specs=[pl.BlockSpec((ROW_TILE, d), lambda i: (i, 0)),
                  pl.BlockSpec((1, d), lambda i: (0, 0)),
                  pl.BlockSpec((1, 1, d), lambda i: (i, 0, 0)),
                  pl.BlockSpec((1, 1, d), lambda i: (i, 0, 0))],
        out_specs=pl.BlockSpec((ROW_TILE, d), lambda i: (i, 0)),
        compiler_params=_cparams(("parallel",)),
        name="normmod",
    )(xs, gain.reshape(1, d), sc_t, sh_t)


def _normmod_first_kernel(x_ref, c_ref, g_ref, sc_ref, sh_ref, o_ref, xs_ref, *, tiles_per_batch):
    x = jnp.where(pl.program_id(1) == tiles_per_batch - 1, c_ref[...], x_ref[...])
    xs_ref[...] = x
    y = x * lax.rsqrt(jnp.mean(x * x, axis=-1, keepdims=True) + EPS) * g_ref[...]
    o_ref[...] = (y * (1.0 + sc_ref[0]) + sh_ref[0]).astype(BF16)


def _normmod_first(x, ctx, gain, sc_t, sh_t):
    batch, t_lat, d = x.shape
    tpb = t_lat // ROW_TILE + 1
    m = batch * tpb * ROW_TILE
    tab_spec = pl.BlockSpec((1, 1, d), lambda b, r: (b * tpb + r, 0, 0))
    row_spec = pl.BlockSpec((ROW_TILE, d), lambda b, r: (b * tpb + r, 0))
    return pl.pallas_call(
        functools.partial(_normmod_first_kernel, tiles_per_batch=tpb),
        out_shape=(jax.ShapeDtypeStruct((m, d), BF16), jax.ShapeDtypeStruct((m, d), F32)),
        grid=(batch, tpb),
        in_specs=[pl.BlockSpec((None, ROW_TILE, d), lambda b, r: (b, jnp.minimum(r, tpb - 2), 0)),
                  pl.BlockSpec((None, CTX_LEN, d), lambda b, r: (b, 0, 0)),
                  pl.BlockSpec((1, d), lambda b, r: (0, 0)),
                  tab_spec, tab_spec],
        out_specs=(row_spec, row_spec),
        compiler_params=_cparams(("parallel", "parallel")),
        name="normmod_first",
    )(x, ctx, gain.reshape(1, d), sc_t, sh_t)


IN_UNIT = 1024
W_KV_ROW, W_MGATE_ROW, W_LOCAL_ROW, W_MERGE_ROW = 1024, 7680, 7696, 13840
N_IN = W_MERGE_ROW + N_BRANCH * D_MODEL


def _pick_tile(n, candidates):
    for t in candidates:
        if n % t == 0:
            return t
    raise ValueError(f"no tile for {n}")


def _inproj_kernel(h_ref, w_ref, o_ref, wt_s):
    @pl.when(pl.program_id(1) == 0)
    def _():
        wt_s[...] = w_ref[...].T.astype(BF16)

    o_ref[...] = jnp.dot(h_ref[...], wt_s[...], preferred_element_type=F32).astype(o_ref.dtype)


def _inproj(h, w_t, layer, n_units, unit_row, name):
    m, k = h.shape
    tm = _pick_tile(m, (1024, 512, 256))
    base = layer * N_IN
    return pl.pallas_call(
        _inproj_kernel,
        out_shape=jax.ShapeDtypeStruct((m, n_units * IN_UNIT), BF16),
        grid=(n_units, m // tm),
        in_specs=[pl.BlockSpec((tm, k), lambda j, i: (i, 0)),
                  pl.BlockSpec((pl.Element(IN_UNIT), pl.Element(k)),
                               lambda j, i: (pl.multiple_of(base + unit_row(j), 8), 0))],
        out_specs=pl.BlockSpec((tm, IN_UNIT), lambda j, i: (i, j)),
        scratch_shapes=[pltpu.VMEM((k, IN_UNIT), BF16)],
        compiler_params=_cparams(("parallel", "arbitrary")),
        name=name,
    )(h, w_t)


def _inproj_tail_kernel(h_ref, wkv_ref, wg_ref, bias_ref,
                        kv_out, g_out, gt_out, pre_out, pret_out, suf_out, suft_out, wt_s):
    kvw = wkv_ref.shape[0]
    L = ROW_TILE

    @pl.when(pl.program_id(0) == 0)
    def _():
        wt_s[:, :kvw] = wkv_ref[...].T.astype(BF16)
        wt_s[:, kvw:] = wg_ref[...].T.astype(BF16)

    y = jnp.dot(h_ref[...], wt_s[...], preferred_element_type=F32)
    kv_out[...] = y[:, :kvw]
    g = y[:, kvw:]
    lane = lax.broadcasted_iota(jnp.int32, g.shape, 1)
    gates = jnp.where(lane < 4 * MLSTM_HEADS, g, 0.0) + bias_ref[...]
    log_f = jnp.minimum(gates, 0.0) - jnp.log1p(jnp.exp(-jnp.abs(gates)))
    p1 = log_f.astype(BF16)
    r1 = log_f - p1.astype(F32)
    p2 = r1.astype(BF16)
    p3 = (r1 - p2.astype(F32)).astype(BF16)
    parts = jnp.concatenate([p1, p2, p3], axis=1)
    row = lax.broadcasted_iota(jnp.int32, (L, L), 0)
    col = lax.broadcasted_iota(jnp.int32, (L, L), 1)
    lower = (col <= row).astype(BF16)
    chunks = [slice(c * L, (c + 1) * L) for c in range(g.shape[0] // L)]
    sums = [jnp.dot(lower, parts[rs], preferred_element_type=F32) for rs in chunks]
    pre = [s[:, 0:128] + s[:, 128:256] + s[:, 256:384] for s in sums]
    suf = [p[L - 1:L, :] - p + log_f[rs] for p, rs in zip(pre, chunks)]
    g_out[...] = gates
    for c, rs in enumerate(chunks):
        pre_out[rs, :] = pre[c]
        suf_out[rs, :] = suf[c]
        gt_out[:, rs] = gates[rs].T
        pret_out[:, rs] = pre[c].T
        suft_out[:, rs] = suf[c].T


def _inproj_tail(h, w_t, layer, bias):
    m, k = h.shape
    tm = _pick_tile(m, (1024, 512, 256))
    kvw = 2 * ATT_KV_HEADS * ATT_HEAD_DIM
    row_sd, col_sd = jax.ShapeDtypeStruct((m, 128), F32), jax.ShapeDtypeStruct((128, m), F32)
    row_spec = pl.BlockSpec((tm, 128), lambda i: (i, 0))
    col_spec = pl.BlockSpec((128, tm), lambda i: (0, i))
    return pl.pallas_call(
        _inproj_tail_kernel,
        out_shape=(jax.ShapeDtypeStruct((m, kvw), F32), row_sd, col_sd, row_sd, col_sd, row_sd, col_sd),
        grid=(m // tm,),
        in_specs=[pl.BlockSpec((tm, k), lambda i: (i, 0)),
                  pl.BlockSpec((pl.Element(kvw), pl.Element(k)), lambda i: (layer * N_IN + W_KV_ROW, 0)),
                  pl.BlockSpec((pl.Element(128), pl.Element(k)), lambda i: (layer * N_IN + W_MGATE_ROW, 0)),
                  pl.BlockSpec((1, 128), lambda i: (0, 0))],
        out_specs=(pl.BlockSpec((tm, kvw), lambda i: (i, 0)),
                   row_spec, col_spec, row_spec, col_spec, row_spec, col_spec),
        scratch_shapes=[pltpu.VMEM((k, kvw + 128), BF16)],
        compiler_params=_cparams(("arbitrary",)),
        name="inproj_tail",
    )(h, w_t, w_t, bias)


def _qkvprep_kernel(q_ref, t_ref, cos_ref, sin_ref, qg_ref, kg_ref, q_out, k_out, vt_out, *, tiles_per_batch):
    is_lat = pl.program_id(1) < tiles_per_batch - 1
    cos, sin, kg = cos_ref[...], sin_ref[...], kg_ref[...]
    qg = qg_ref[...] * (ATT_SCALE * LOG2_E)
    hd = ATT_HEAD_DIM
    n_q = ATT_KV_HEADS * ATT_GROUP
    sls = [slice(h * hd, (h + 1) * hd) for h in range(n_q)]
    xs = [q_ref[:, sl].astype(F32) for sl in sls] + [t_ref[:, sl] for sl in sls[:ATT_KV_HEADS]]
    gains = [qg] * n_q + [kg] * ATT_KV_HEADS
    xs = [x * lax.rsqrt(jnp.mean(x * x, axis=-1, keepdims=True) + EPS) * g for x, g in zip(xs, gains)]
    lane = lax.broadcasted_iota(jnp.int32, xs[0].shape, 1)
    low = (lane % 64) < 32
    up = [pltpu.roll(x, 96, 1) for x in xs]
    down = [pltpu.roll(x, 32, 1) for x in xs]
    outs = [jnp.where(is_lat, x * cos + jnp.where(low, a, b) * sin, x).astype(BF16)
            for x, a, b in zip(xs, up, down)]
    for h in range(n_q):
        q_out[:, sls[h]] = outs[h]
    for h in range(ATT_KV_HEADS):
        k_out[:, sls[h]] = outs[n_q + h]
        vt_out[h, 0:hd, :] = t_ref[:, sls[ATT_KV_HEADS + h]].T.astype(BF16)
        vt_out[h, hd:, :] = jnp.ones((VT_ROWS - hd, ROW_TILE), BF16)


def _qkvprep(ya, kv, cos_t, sin_t, q_gain, k_gain, batch, tiles_per_batch):
    m = kv.shape[0]
    lat_tiles = tiles_per_batch - 1
    kvw = ATT_KV_HEADS * ATT_HEAD_DIM
    rope_spec = pl.BlockSpec((ROW_TILE, ATT_HEAD_DIM), lambda b, r: (jnp.minimum(r, lat_tiles - 1), 0))
    gain_spec = pl.BlockSpec((1, ATT_HEAD_DIM), lambda b, r: (0, 0))
    return pl.pallas_call(
        functools.partial(_qkvprep_kernel, tiles_per_batch=tiles_per_batch),
        out_shape=(jax.ShapeDtypeStruct((m, BRANCH_W), BF16),
                   jax.ShapeDtypeStruct((m, kvw), BF16),
                   jax.ShapeDtypeStruct((batch, ATT_KV_HEADS, VT_ROWS, tiles_per_batch * ROW_TILE), BF16)),
        grid=(batch, tiles_per_batch),
        in_specs=[pl.BlockSpec((ROW_TILE, BRANCH_W), lambda b, r: (b * tiles_per_batch + r, A_AQ)),
                  pl.BlockSpec((ROW_TILE, 2 * kvw), lambda b, r: (b * tiles_per_batch + r, 0)),
                  rope_spec, rope_spec, gain_spec, gain_spec],
        out_specs=(pl.BlockSpec((ROW_TILE, BRANCH_W), lambda b, r: (b * tiles_per_batch + r, 0)),
                   pl.BlockSpec((ROW_TILE, kvw), lambda b, r: (b * tiles_per_batch + r, 0)),
                   pl.BlockSpec((None, ATT_KV_HEADS, VT_ROWS, ROW_TILE), lambda b, r: (b, 0, 0, r))),
        compiler_params=_cparams(("parallel", "parallel")),
        name="qkvprep",
    )(ya, kv, cos_t, sin_t, q_gain.reshape(1, ATT_HEAD_DIM), k_gain.reshape(1, ATT_HEAD_DIM))


def _attn_kernel(q_ref, z_ref, k_ref, vt_ref, o_ref, *, n_lat_q, t_lat, kc):
    is_lat = pl.program_id(2) < n_lat_q
    hd = ATT_HEAD_DIM
    q = jnp.concatenate([q_ref[:, g * hd:(g + 1) * hd] for g in range(ATT_GROUP)], axis=0)

    def attend(chunks):
        def scores(c):
            off, n = chunks[c]
            return lax.dot_general(k_ref[off:off + n, :], q, (((1,), (1,)), ((), ())),
                                   preferred_element_type=F32)

        m, o = None, None
        s_next = scores(0)
        for c, (off, n) in enumerate(chunks):
            s = s_next
            if c + 1 < len(chunks):
                s_next = scores(c + 1)
            mc = jnp.max(s, axis=0, keepdims=True)
            m_new = mc if m is None else jnp.maximum(m, mc)
            p = jnp.exp2((s - m_new).astype(BF16))
            oc = jnp.dot(vt_ref[:, off:off + n], p, preferred_element_type=F32)
            o = oc if o is None else jnp.exp2(m - m_new) * o + oc
            m = m_new
        o = o[0:hd] / o[hd:hd + 1]
        for g in range(ATT_GROUP):
            sl = slice(g * ATT_HEAD_DIM, (g + 1) * ATT_HEAD_DIM)
            og = o[:, g * ATT_Q_TILE:(g + 1) * ATT_Q_TILE].T
            o_ref[:, sl] = (og * _silu(z_ref[:, sl].astype(F32))).astype(BF16)

    ctx_chunk = (t_lat, CTX_LEN)

    @pl.when(is_lat)
    def _():
        attend([(off, kc) for off in range(0, t_lat, kc)] + [ctx_chunk])

    @pl.when(jnp.logical_not(is_lat))
    def _():
        attend([ctx_chunk])


def _attention(qb, ya, kb, vt, batch, s_len):
    m = ya.shape[0]
    t_lat = s_len - CTX_LEN
    nq = s_len // ATT_Q_TILE
    n_lat_q = t_lat // ATT_Q_TILE
    gw = ATT_GROUP * ATT_HEAD_DIM
    return pl.pallas_call(
        functools.partial(_attn_kernel, n_lat_q=n_lat_q, t_lat=t_lat,
                          kc=_pick_tile(t_lat, (ATT_KEY_CHUNK, 512, 256))),
        out_shape=jax.ShapeDtypeStruct((m, BRANCH_W), BF16),
        grid=(batch, ATT_KV_HEADS, nq),
        in_specs=[pl.BlockSpec((ATT_Q_TILE, gw), lambda b, h, i: (b * nq + i, h)),
                  pl.BlockSpec((ATT_Q_TILE, gw), lambda b, h, i: (b * nq + i, A_AZ * 2 + h)),
                  pl.BlockSpec((s_len, ATT_HEAD_DIM), lambda b, h, i: (b, h)),
                  pl.BlockSpec((None, None, VT_ROWS, s_len), lambda b, h, i: (b, h, 0, 0))],
        out_specs=pl.BlockSpec((ATT_Q_TILE, gw), lambda b, h, i: (b * nq + i, h)),
        compiler_params=_cparams(("parallel", "parallel", "arbitrary")),
        name="attention",
    )(qb, ya, kb, vt)


def _mlstm_kernel(*refs, reverse, final):
    if final:
        (q_ref, k_ref, v_ref, g_ref, gt_ref, bc_ref, bct_ref, o_ref, z_ref, gain_ref, hprev_ref,
         out_ref, c_s, n_s, m_s) = refs
    else:
        q_ref, k_ref, v_ref, g_ref, gt_ref, bc_ref, bct_ref, out_ref, c_s, n_s, m_s = refs
    L = ROW_TILE
    hd = MLSTM_HEAD_DIM

    @pl.when(pl.program_id(1) == 0)
    def _():
        c_s[...] = jnp.zeros(c_s.shape, F32)
        n_s[...] = jnp.zeros(n_s.shape, F32)
        m_s[...] = jnp.full(m_s.shape, M_INIT, F32)

    gates, gates_t = g_ref[...], gt_ref[...]
    bcum, bcum_t = bc_ref[...], bct_ref[...]
    row = lax.broadcasted_iota(jnp.int32, (L, L), 0)
    col = lax.broadcasted_iota(jnp.int32, (L, L), 1)
    tri = (col >= row) if reverse else (col <= row)
    last = 0 if reverse else L - 1
    heads = range(MLSTM_HEADS)
    hsl = [slice(h * hd, (h + 1) * hd) for h in heads]
    icol = [(2 if reverse else 0) * MLSTM_HEADS + h for h in heads]
    fcol = [(3 if reverse else 1) * MLSTM_HEADS + h for h in heads]
    c_old = [c_s[h] for h in heads]
    n_old = [n_s[h, 0:1, :] for h in heads]
    m_old = [m_s[h, 0:1, 0:1] for h in heads]
    q = [q_ref[:, hsl[h]] for h in heads]
    k = [k_ref[:, hsl[h]].astype(F32) * MLSTM_K_SCALE for h in heads]
    v = [v_ref[:, hsl[h]] for h in heads]
    b_col = [bcum[:, fcol[h]:fcol[h] + 1] for h in heads]
    i_col = [gates[:, icol[h]:icol[h] + 1] for h in heads]

    qk = [lax.dot_general(q[h], k[h].astype(BF16), (((1,), (1,)), ((), ())), preferred_element_type=F32)
          for h in heads]
    qc = [jnp.dot(q[h], c_old[h].astype(BF16), preferred_element_type=F32) for h in heads]

    m_t, dmat, inter = [], [], []
    for h in heads:
        b_row, i_row = bcum_t[fcol[h]:fcol[h] + 1, :], gates_t[icol[h]:icol[h] + 1, :]
        log_d = jnp.where(tri, b_col[h] - b_row + i_row, -jnp.inf)
        log_inter = b_col[h] + m_old[h]
        m_t.append(jnp.maximum(log_inter, jnp.max(log_d, axis=1, keepdims=True)))
        dmat.append(jnp.exp(log_d - m_t[h]))
        inter.append(jnp.exp(log_inter - m_t[h]))

    sm = [qk[h] * dmat[h] for h in heads]
    sv = [jnp.dot(sm[h].astype(BF16), v[h], preferred_element_type=F32) for h in heads]

    kw, decay, m_new = [], [], []
    for h in heads:
        b_last = bcum[last:last + 1, fcol[h]:fcol[h] + 1]
        log_w = b_last - b_col[h] + i_col[h]
        m_new.append(jnp.maximum(b_last + m_old[h], jnp.max(log_w, axis=0, keepdims=True)))
        decay.append(jnp.exp(b_last + m_old[h] - m_new[h]))
        kw.append(k[h] * jnp.exp(log_w - m_new[h]))
    kv = [lax.dot_general(kw[h].astype(BF16), v[h], (((0,), (0,)), ((), ())), preferred_element_type=F32)
          for h in heads]

    for h in heads:
        num = inter[h] * qc[h] + sv[h]
        den = inter[h] * jnp.sum(q[h].astype(F32) * n_old[h], axis=1, keepdims=True) \
            + jnp.sum(sm[h], axis=1, keepdims=True)
        hh = num / jnp.maximum(jnp.abs(den), jnp.exp(-m_t[h]))
        if final:
            ht = hh + hprev_ref[:, hsl[h]]
            hn = ht * lax.rsqrt(jnp.mean(ht * ht, axis=-1, keepdims=True) + EPS)
            o_g, z_g = o_ref[:, hsl[h]].astype(F32), z_ref[:, hsl[h]].astype(F32)
            gate = z_g / ((1.0 + jnp.exp(-o_g)) * (1.0 + jnp.exp(-z_g)))
            out_ref[:, hsl[h]] = (hn * gain_ref[:, hsl[h]] * gate).astype(BF16)
        else:
            out_ref[:, hsl[h]] = hh

    for h in heads:
        c_s[h] = decay[h] * c_old[h] + kv[h]
        n_s[h, 0:1, :] = decay[h] * n_old[h] + jnp.sum(kw[h], axis=0, keepdims=True)
        m_s[h] = jnp.broadcast_to(m_new[h], m_s.shape[1:])


def _mlstm_pass(ya, gate_arrays, gain, hprev, batch, tiles_per_batch, reverse):
    m = ya.shape[0]
    lat = tiles_per_batch - 1
    final = hprev is not None

    def rows(b, c):
        r = jnp.where(c == 0, lat, (lat - c) if reverse else (c - 1))
        return b * tiles_per_batch + r

    def col(cidx):
        return pl.BlockSpec((ROW_TILE, BRANCH_W), lambda b, c: (rows(b, c), cidx))

    gate_spec = pl.BlockSpec((ROW_TILE, 128), lambda b, c: (rows(b, c), 0))
    gate_t_spec = pl.BlockSpec((128, ROW_TILE), lambda b, c: (0, rows(b, c)))
    in_specs = [col(A_MQ), col(A_MK), col(A_MV), gate_spec, gate_t_spec, gate_spec, gate_t_spec]
    args = [ya, ya, ya, *gate_arrays]
    if final:
        in_specs += [col(A_MO), col(A_MZ), pl.BlockSpec((1, BRANCH_W), lambda b, c: (0, 0)),
                     pl.BlockSpec((ROW_TILE, BRANCH_W), lambda b, c: (rows(b, c), 0))]
        args += [ya, ya, gain.reshape(1, BRANCH_W), hprev]
    return pl.pallas_call(
        functools.partial(_mlstm_kernel, reverse=reverse, final=final),
        out_shape=jax.ShapeDtypeStruct((m, BRANCH_W), BF16 if final else F32),
        grid=(batch, tiles_per_batch),
        in_specs=in_specs,
        out_specs=pl.BlockSpec((ROW_TILE, BRANCH_W), lambda b, c: (rows(b, c), 0)),
        scratch_shapes=[pltpu.VMEM((MLSTM_HEADS, MLSTM_HEAD_DIM, MLSTM_HEAD_DIM), F32),
                        pltpu.VMEM((MLSTM_HEADS, 8, MLSTM_HEAD_DIM), F32),
                        pltpu.VMEM((MLSTM_HEADS, 8, 128), F32)],
        compiler_params=_cparams(("parallel", "arbitrary")),
        name="mlstm_bwd" if reverse else "mlstm_fwd",
    )(*args)


def _segment_info(r, lat):
    has_prev = jnp.logical_and(r != 0, r != lat)
    has_next = jnp.logical_and(r != lat - 1, r != lat)
    seg_len = jnp.where(r == lat, CTX_LEN, lat * ROW_TILE)
    t0 = jnp.where(r == lat, 0, r * ROW_TILE)
    return has_prev, has_next, seg_len, t0


def _conv_strip(j, r, lat, cu_ref, cb_ref, cc_ref, cz_ref, cu_p, cc_p, cu_n, cc_n, cw_ref):
    has_prev, has_next, _, _ = _segment_info(r, lat)
    rowi = lax.broadcasted_iota(jnp.int32, (ROW_TILE, 1), 0)
    cs = slice(j * 128, (j + 1) * 128)
    a = cc_ref[:, cs].astype(F32) * cu_ref[:, cs].astype(F32)
    a_prev = jnp.where(has_prev, cc_p[HALO - 1:HALO, cs].astype(F32) * cu_p[HALO - 1:HALO, cs].astype(F32), 0.0)
    a_next = jnp.where(has_next, cc_n[0:1, cs].astype(F32) * cu_n[0:1, cs].astype(F32), 0.0)
    a_m1 = jnp.where(rowi == 0, a_prev, pltpu.roll(a, 1, 0))
    a_p1 = jnp.where(rowi == ROW_TILE - 1, a_next, pltpu.roll(a, ROW_TILE - 1, 0))
    y = cw_ref[0:1, cs] * a_m1 + cw_ref[1:2, cs] * a + cw_ref[2:3, cs] * a_p1
    return (cb_ref[:, cs].astype(F32) * y * _silu(cz_ref[:, cs].astype(F32))).astype(BF16)


def _pool_branch(r, lat, pu_ref, pz_ref, pu_p, pu_n, pw_ref, ps_ref, band_ref, yd_ref):
    has_prev, has_next, seg_len, t0 = _segment_info(r, lat)
    rowi = lax.broadcasted_iota(jnp.int32, (ROW_TILE, 1), 0)
    u = pu_ref[...]
    halo_zero = jnp.zeros((HALO, BRANCH_W), u.dtype)
    ext = jnp.concatenate([u, jnp.where(has_prev, pu_p[...], halo_zero), jnp.where(has_next, pu_n[...], halo_zero),
                           jnp.zeros((POOL_K - ROW_TILE - 2 * HALO, BRANCH_W), u.dtype)], axis=0)
    t = t0 + rowi
    groups = range(len(POOL_WINDOWS))
    gsl = [slice(g * POOL_GROUP, (g + 1) * POOL_GROUP) for g in groups]
    acc = [jnp.dot(band_ref[g], ext[:, gsl[g]], preferred_element_type=F32) for g in groups]
    dev = []
    for g, w in enumerate(POOL_WINDOWS):
        inv_cnt = 1.0 / (jnp.minimum(t + (w - w // 2), seg_len) - jnp.maximum(t - w // 2, 0)).astype(F32)
        dev.append((acc[g] * inv_cnt - u[:, gsl[g]].astype(F32)).astype(BF16))
    pg = [jnp.dot(dev[g], pw_ref[g], preferred_element_type=F32) for g in groups]
    for g in groups:
        gate = _silu(pz_ref[:, gsl[g]].astype(F32)) * ps_ref[:, gsl[g]]
        yd_ref[:, gsl[g]] = (pg[g] * gate).astype(BF16)


def _pool_band():
    t = np.arange(ROW_TILE)[:, None]
    pos = np.concatenate([np.arange(ROW_TILE), np.arange(-HALO, 0), np.arange(ROW_TILE, ROW_TILE + HALO),
                          np.full(POOL_K - ROW_TILE - 2 * HALO, -10 * ROW_TILE)])[None, :]
    return np.stack([(pos >= t - w // 2) & (pos < t + w - w // 2) for w in POOL_WINDOWS]).astype(np.float32)


def _mergeout_kernel(*refs, final, tiles_per_batch):
    (b0, b1, cu_ref, cb_ref, cc_ref, cz_ref, pu_ref, pz_ref, cu_p, cc_p, pu_p, cu_n, cc_n, pu_n,
     cw_ref, pw_ref, ps_ref, band_ref, g_ref, wb_ref, wo_ref, x_ref, gt_ref) = refs[:23]
    if final:
        fg_ref, o_ref, yd_s = refs[23:]
    else:
        o_ref, yd_s = refs[23:]
    r, lat = pl.program_id(1), tiles_per_batch - 1

    def gated(j, br):
        gate = jax.nn.sigmoid(g_ref[:, j * D_MODEL:(j + 1) * D_MODEL].astype(F32))
        return gate * jnp.dot(br, wb_ref[j], preferred_element_type=F32)

    acc = gated(0, b0[...]) + gated(1, b1[...])
    strips = [_conv_strip(j, r, lat, cu_ref, cb_ref, cc_ref, cz_ref, cu_p, cc_p, cu_n, cc_n, cw_ref)
              for j in range(BRANCH_W // 128)]
    _pool_branch(r, lat, pu_ref, pz_ref, pu_p, pu_n, pw_ref, ps_ref, band_ref, yd_s)
    acc = acc + gated(2, jnp.concatenate(strips, axis=1)) + gated(3, yd_s[...])
    y = x_ref[...] + gt_ref[0] * jnp.dot(acc.astype(BF16), wo_ref[...], preferred_element_type=F32)
    if final:
        y = y * lax.rsqrt(jnp.mean(y * y, axis=-1, keepdims=True) + EPS) * fg_ref[...]
    o_ref[...] = y


def _mergeout(y_att, y_mls, yb, conv_w, pool_w, pool_scale, w_branch, w_out, layer, xs, gt_t, final_gain,
              batch, tiles_per_batch):
    m, d = xs.shape
    final = final_gain is not None
    lat = tiles_per_batch - 1
    n_r = lat if final else tiles_per_batch
    resident = pl.Buffered(1)
    hb = ROW_TILE // HALO
    n_halo = m // HALO
    n_win = len(POOL_WINDOWS)

    def src(b, r):
        return b * tiles_per_batch + r

    def col(cidx):
        return pl.BlockSpec((ROW_TILE, BRANCH_W), lambda b, r: (src(b, r), cidx))

    def prev(cidx):
        return pl.BlockSpec((HALO, BRANCH_W), lambda b, r: (jnp.maximum(src(b, r) * hb - 1, 0), cidx))

    def nxt(cidx):
        return pl.BlockSpec((HALO, BRANCH_W), lambda b, r: (jnp.minimum((src(b, r) + 1) * hb, n_halo - 1), cidx))

    in_specs = [col(0), col(0),
                col(B_CU), col(B_CB), col(B_CC), col(B_CZ), col(B_PU), col(B_PZ),
                prev(B_CU), prev(B_CC), prev(B_PU), nxt(B_CU), nxt(B_CC), nxt(B_PU),
                pl.BlockSpec((None, 3, BRANCH_W), lambda b, r: (layer, 0, 0)),
                pl.BlockSpec((None, n_win, POOL_GROUP, POOL_GROUP), lambda b, r: (layer, 0, 0, 0)),
                pl.BlockSpec((None, 1, BRANCH_W), lambda b, r: (layer, 0, 0)),
                pl.BlockSpec((n_win, ROW_TILE, POOL_K), lambda b, r: (0, 0, 0)),
                pl.BlockSpec((ROW_TILE, N_BRANCH * d), lambda b, r: (src(b, r), B_GATE)),
                pl.BlockSpec((None, N_BRANCH, BRANCH_W, d), lambda b, r: (layer, 0, 0, 0), pipeline_mode=resident),
                pl.BlockSpec((None, d, d), lambda b, r: (layer, 0, 0), pipeline_mode=resident),
                pl.BlockSpec((ROW_TILE, d), lambda b, r: (src(b, r), 0)),
                pl.BlockSpec((1, 1, d), lambda b, r: (src(b, r), 0, 0))]
    args = [y_att, y_mls, *([yb] * 12), conv_w, pool_w, pool_scale.reshape(-1, 1, BRANCH_W),
            jnp.asarray(_pool_band(), BF16), yb, w_branch, w_out, xs, gt_t]
    if final:
        in_specs.append(pl.BlockSpec((1, d), lambda b, r: (0, 0)))
        args.append(final_gain.reshape(1, d))
    return pl.pallas_call(
        functools.partial(_mergeout_kernel, final=final, tiles_per_batch=tiles_per_batch),
        out_shape=jax.ShapeDtypeStruct((batch * n_r * ROW_TILE, d), F32),
        grid=(batch, n_r),
        in_specs=in_specs,
        out_specs=pl.BlockSpec((ROW_TILE, d), lambda b, r: (b * n_r + r, 0)),
        scratch_shapes=[pltpu.VMEM((ROW_TILE, BRANCH_W), BF16)],
        compiler_params=_cparams(("parallel", "parallel")),
        name="mergeout_final" if final else "mergeout",
    )(*args)


def _rope_tables(t_lat):
    pos = np.arange(t_lat)
    quarter = ATT_HEAD_DIM // 4
    freq = ROPE_THETA ** (-jnp.arange(quarter, dtype=F32) / quarter)
    a_row = jnp.asarray(pos // GRID_W, F32)[:, None] * freq[None, :]
    a_col = jnp.asarray(pos % GRID_W, F32)[:, None] * freq[None, :]
    cos_t = jnp.concatenate([jnp.cos(a_row), jnp.cos(a_row), jnp.cos(a_col), jnp.cos(a_col)], axis=-1)
    sin_t = jnp.concatenate([-jnp.sin(a_row), jnp.sin(a_row), -jnp.sin(a_col), jnp.sin(a_col)], axis=-1)
    return cos_t, sin_t


def kernel(x, c, ctx, c_ctx, norm_gain, w_mod, b_mod, w_in, q_norm_gain, k_norm_gain, mlstm_gate_bias,
           mlstm_norm_gain, conv_w, pool_w, pool_scale, w_branch, w_out, final_norm_gain):
    batch, t_lat, d = x.shape
    depth = w_in.shape[0]
    assert d == D_MODEL and ctx.shape[1] == CTX_LEN and t_lat % ROW_TILE == 0 and batch < 8
    s_len = t_lat + CTX_LEN
    tiles_per_batch = s_len // ROW_TILE

    cc = jnp.zeros((8, d), F32).at[:batch].set(c).at[batch].set(c_ctx)
    mod = _modulation(cc, w_mod, b_mod)
    tile_row = np.array([b if r < tiles_per_batch - 1 else batch
                         for b in range(batch) for r in range(tiles_per_batch)], np.int32)
    cos_t, sin_t = _rope_tables(t_lat)
    assert w_in.shape[1:] == (d, N_IN)
    w_t = jnp.swapaxes(w_in, 1, 2).reshape(depth * N_IN, d)
    w_branch_b, w_out_b, pool_w_b = w_branch.astype(BF16), w_out.astype(BF16), pool_w.astype(BF16)

    xs, out = None, None
    for l in range(depth):
        mod_t = mod[l][tile_row][:, None, :]
        sh_t, sc_t, gt_t = mod_t[..., :d], mod_t[..., d:2 * d], mod_t[..., 2 * d:]
        bias = jnp.zeros((1, 128), F32).at[0, :4 * MLSTM_HEADS].set(mlstm_gate_bias[l].reshape(-1))

        if l == 0:
            h, xs = _normmod_first(x, ctx, norm_gain[l], sc_t, sh_t)
        else:
            h = _normmod(xs, norm_gain[l], sc_t, sh_t)
        ya = _inproj(h, w_t, l, 7, lambda j: jnp.where(j == 0, 0, 512 + IN_UNIT * j), "inproj_a")
        yb = _inproj(h, w_t, l, 14, lambda j: jnp.where(j < 8, W_MERGE_ROW + IN_UNIT * j,
                                                       W_LOCAL_ROW + IN_UNIT * (j - 8)), "inproj_b")
        kv, g, g_t, pre, pre_t, suf, suf_t = _inproj_tail(h, w_t, l, bias)

        qb, kb, vt = _qkvprep(ya, kv, cos_t, sin_t, q_norm_gain[l], k_norm_gain[l], batch, tiles_per_batch)
        y_att = _attention(qb, ya, kb, vt, batch, s_len)
        h_f = _mlstm_pass(ya, (g, g_t, pre, pre_t), None, None, batch, tiles_per_batch, reverse=False)
        y_mls = _mlstm_pass(ya, (g, g_t, suf, suf_t), mlstm_norm_gain[l], h_f, batch, tiles_per_batch,
                            reverse=True)
        last = l == depth - 1
        res = _mergeout(y_att, y_mls, yb, conv_w, pool_w_b, pool_scale, w_branch_b, w_out_b, l, xs, gt_t,
                        final_norm_gain if last else None, batch, tiles_per_batch)
        if last:
            out = res.reshape(batch, t_lat, d)
        else:
            xs = res
    return out
```

```python
import functools

import numpy as np
import jax
import jax.numpy as jnp
from jax import lax
from jax.experimental import pallas as pl
from jax.experimental.pallas import tpu as pltpu

F32 = jnp.float32
BF16 = jnp.bfloat16

D_MODEL = 2048
BRANCH_W = 1024
GRID_W = 64
CTX_LEN = 256
EPS = 1e-6
ATT_HEAD_DIM = 128
ATT_GROUP = 4
ATT_KV_HEADS = 2
ATT_SCALE = ATT_HEAD_DIM ** -0.5
LOG2_E = 1.4426950408889634
ROPE_THETA = 10000.0
MLSTM_HEADS = 4
MLSTM_HEAD_DIM = 256
MLSTM_K_SCALE = MLSTM_HEAD_DIM ** -0.5
M_INIT = -1e30
POOL_WINDOWS = (2, 4, 8, 16)
POOL_GROUP = 256
N_BRANCH = 4

ROW_TILE = 256
HALO = 16
POOL_K = 384
ATT_Q_TILE = 256
ATT_KEY_CHUNK = 256
VT_ROWS = ATT_HEAD_DIM + 16
V7X_VMEM_LIMIT = 56 * 1024 * 1024

A_AQ, A_AZ, A_MQ, A_MK, A_MV, A_MO, A_MZ = range(7)
B_GATE = 0
B_CU, B_CB, B_CC, B_CZ, B_PU, B_PZ = range(8, 14)


def _cparams(sem, vmem=V7X_VMEM_LIMIT):
    return pltpu.CompilerParams(dimension_semantics=sem, vmem_limit_bytes=vmem)


def _silu(x):
    return x * jax.nn.sigmoid(x)


def _mod_kernel(c_ref, w_ref, b_ref, o_ref):
    a = _silu(c_ref[...]).astype(BF16)
    o_ref[...] = jnp.dot(a, w_ref[...].astype(BF16), preferred_element_type=F32) + b_ref[...]


def _modulation(cc, w_mod, b_mod):
    depth, d, n = w_mod.shape
    tn = 768
    return pl.pallas_call(
        _mod_kernel,
        out_shape=jax.ShapeDtypeStruct((depth, 8, n), F32),
        grid=(depth, n // tn),
        in_specs=[pl.BlockSpec((8, d), lambda l, j: (0, 0)),
                  pl.BlockSpec((None, d, tn), lambda l, j: (l, 0, j)),
                  pl.BlockSpec((None, 1, tn), lambda l, j: (l, 0, j))],
        out_specs=pl.BlockSpec((None, 8, tn), lambda l, j: (l, 0, j)),
        compiler_params=_cparams(("parallel", "parallel")),
        name="modulation",
    )(cc, w_mod, b_mod.reshape(depth, 1, n))


def _normmod_first_kernel(x_ref, c_ref, g_ref, sc_ref, sh_ref, o_ref, xs_ref, *, tiles_per_batch):
    x = jnp.where(pl.program_id(1) == tiles_per_batch - 1, c_ref[...], x_ref[...])
    xs_ref[...] = x
    y = x * lax.rsqrt(jnp.mean(x * x, axis=-1, keepdims=True) + EPS) * g_ref[...]
    o_ref[...] = (y * (1.0 + sc_ref[0]) + sh_ref[0]).astype(BF16)


def _normmod_first(x, ctx, gain, sc_t, sh_t):
    batch, t_lat, d = x.shape
    tpb = t_lat // ROW_TILE + 1
    m = batch * tpb * ROW_TILE
    tab_spec = pl.BlockSpec((1, 1, d), lambda b, r: (b * tpb + r, 0, 0))
    row_spec = pl.BlockSpec((ROW_TILE, d), lambda b, r: (b * tpb + r, 0))
    return pl.pallas_call(
        functools.partial(_normmod_first_kernel, tiles_per_batch=tpb),
        out_shape=(jax.ShapeDtypeStruct((m, d), BF16), jax.ShapeDtypeStruct((m, d), F32)),
        grid=(batch, tpb),
        in_specs=[pl.BlockSpec((None, ROW_TILE, d), lambda b, r: (b, jnp.minimum(r, tpb - 2), 0)),
                  pl.BlockSpec((None, CTX_LEN, d), lambda b, r: (b, 0, 0)),
                  pl.BlockSpec((1, d), lambda b, r: (0, 0)),
                  tab_spec, tab_spec],
        out_specs=(row_spec, row_spec),
        compiler_params=_cparams(("parallel", "parallel")),
        name="normmod_first",
    )(x, ctx, gain.reshape(1, d), sc_t, sh_t)


IN_UNIT = 1024
W_KV_ROW, W_MGATE_ROW, W_LOCAL_ROW, W_MERGE_ROW = 1024, 7680, 7696, 13840
N_IN = W_MERGE_ROW + N_BRANCH * D_MODEL


def _pick_tile(n, candidates):
    for t in candidates:
        if n % t == 0:
            return t
    raise ValueError(f"no tile for {n}")


def _inproj_kernel(h_ref, w_ref, o_ref, wt_s):
    @pl.when(pl.program_id(1) == 0)
    def _():
        wt_s[...] = w_ref[...].T.astype(BF16)

    o_ref[...] = jnp.dot(h_ref[...], wt_s[...], preferred_element_type=F32).astype(o_ref.dtype)


def _inproj(h, w_t, layer, n_units, unit_row, name):
    m, k = h.shape
    tm = _pick_tile(m, (1024, 512, 256))
    base = layer * N_IN
    return pl.pallas_call(
        _inproj_kernel,
        out_shape=jax.ShapeDtypeStruct((m, n_units * IN_UNIT), BF16),
        grid=(n_units, m // tm),
        in_specs=[pl.BlockSpec((tm, k), lambda j, i: (i, 0)),
                  pl.BlockSpec((pl.Element(IN_UNIT), pl.Element(k)),
                               lambda j, i: (pl.multiple_of(base + unit_row(j), 8), 0))],
        out_specs=pl.BlockSpec((tm, IN_UNIT), lambda j, i: (i, j)),
        scratch_shapes=[pltpu.VMEM((k, IN_UNIT), BF16)],
        compiler_params=_cparams(("parallel", "arbitrary")),
        name=name,
    )(h, w_t)


def _inproj_tail_kernel(h_ref, wkv_ref, wg_ref, bias_ref,
                        kv_out, g_out, gt_out, pre_out, pret_out, suf_out, suft_out, wt_s):
    kvw = wkv_ref.shape[0]
    L = ROW_TILE

    @pl.when(pl.program_id(0) == 0)
    def _():
        wt_s[:, :kvw] = wkv_ref[...].T.astype(BF16)
        wt_s[:, kvw:] = wg_ref[...].T.astype(BF16)

    y = jnp.dot(h_ref[...], wt_s[...], preferred_element_type=F32)
    kv_out[...] = y[:, :kvw]
    g = y[:, kvw:]
    lane = lax.broadcasted_iota(jnp.int32, g.shape, 1)
    gates = jnp.where(lane < 4 * MLSTM_HEADS, g, 0.0) + bias_ref[...]
    log_f = jnp.minimum(gates, 0.0) - jnp.log1p(jnp.exp(-jnp.abs(gates)))
    p1 = log_f.astype(BF16)
    r1 = log_f - p1.astype(F32)
    p2 = r1.astype(BF16)
    p3 = (r1 - p2.astype(F32)).astype(BF16)
    parts = jnp.concatenate([p1, p2, p3], axis=1)
    row = lax.broadcasted_iota(jnp.int32, (L, L), 0)
    col = lax.broadcasted_iota(jnp.int32, (L, L), 1)
    lower = (col <= row).astype(BF16)
    chunks = [slice(c * L, (c + 1) * L) for c in range(g.shape[0] // L)]
    sums = [jnp.dot(lower, parts[rs], preferred_element_type=F32) for rs in chunks]
    pre = [s[:, 0:128] + s[:, 128:256] + s[:, 256:384] for s in sums]
    suf = [p[L - 1:L, :] - p + log_f[rs] for p, rs in zip(pre, chunks)]
    g_out[...] = gates
    for c, rs in enumerate(chunks):
        pre_out[rs, :] = pre[c]
        suf_out[rs, :] = suf[c]
        gt_out[:, rs] = gates[rs].T
        pret_out[:, rs] = pre[c].T
        suft_out[:, rs] = suf[c].T


def _inproj_tail(h, w_t, layer, bias):
    m, k = h.shape
    tm = _pick_tile(m, (1024, 512, 256))
    kvw = 2 * ATT_KV_HEADS * ATT_HEAD_DIM
    row_sd, col_sd = jax.ShapeDtypeStruct((m, 128), F32), jax.ShapeDtypeStruct((128, m), F32)
    row_spec = pl.BlockSpec((tm, 128), lambda i: (i, 0))
    col_spec = pl.BlockSpec((128, tm), lambda i: (0, i))
    return pl.pallas_call(
        _inproj_tail_kernel,
        out_shape=(jax.ShapeDtypeStruct((m, kvw), F32), row_sd, col_sd, row_sd, col_sd, row_sd, col_sd),
        grid=(m // tm,),
        in_specs=[pl.BlockSpec((tm, k), lambda i: (i, 0)),
                  pl.BlockSpec((pl.Element(kvw), pl.Element(k)), lambda i: (layer * N_IN + W_KV_ROW, 0)),
                  pl.BlockSpec((pl.Element(128), pl.Element(k)), lambda i: (layer * N_IN + W_MGATE_ROW, 0)),
                  pl.BlockSpec((1, 128), lambda i: (0, 0))],
        out_specs=(pl.BlockSpec((tm, kvw), lambda i: (i, 0)),
                   row_spec, col_spec, row_spec, col_spec, row_spec, col_spec),
        scratch_shapes=[pltpu.VMEM((k, kvw + 128), BF16)],
        compiler_params=_cparams(("arbitrary",)),
        name="inproj_tail",
    )(h, w_t, w_t, bias)


def _qkvprep_kernel(q_ref, t_ref, cos_ref, sin_ref, qg_ref, kg_ref, q_out, k_out, vt_out, *, tiles_per_batch):
    is_lat = pl.program_id(1) < tiles_per_batch - 1
    cos, sin, kg = cos_ref[...], sin_ref[...], kg_ref[...]
    qg = qg_ref[...] * (ATT_SCALE * LOG2_E)
    hd = ATT_HEAD_DIM
    n_q = ATT_KV_HEADS * ATT_GROUP
    sls = [slice(h * hd, (h + 1) * hd) for h in range(n_q)]
    xs = [q_ref[:, sl].astype(F32) for sl in sls] + [t_ref[:, sl] for sl in sls[:ATT_KV_HEADS]]
    gains = [qg] * n_q + [kg] * ATT_KV_HEADS
    xs = [x * lax.rsqrt(jnp.mean(x * x, axis=-1, keepdims=True) + EPS) * g for x, g in zip(xs, gains)]
    lane = lax.broadcasted_iota(jnp.int32, xs[0].shape, 1)
    low = (lane % 64) < 32
    up = [pltpu.roll(x, 96, 1) for x in xs]
    down = [pltpu.roll(x, 32, 1) for x in xs]
    outs = [jnp.where(is_lat, x * cos + jnp.where(low, a, b) * sin, x).astype(BF16)
            for x, a, b in zip(xs, up, down)]
    for h in range(n_q):
        q_out[:, sls[h]] = outs[h]
    for h in range(ATT_KV_HEADS):
        k_out[:, sls[h]] = outs[n_q + h]
        vt_out[h, 0:hd, :] = t_ref[:, sls[ATT_KV_HEADS + h]].T.astype(BF16)
        vt_out[h, hd:, :] = jnp.ones((VT_ROWS - hd, ROW_TILE), BF16)


def _qkvprep(ya, kv, cos_t, sin_t, q_gain, k_gain, batch, tiles_per_batch):
    m = kv.shape[0]
    lat_tiles = tiles_per_batch - 1
    kvw = ATT_KV_HEADS * ATT_HEAD_DIM
    rope_spec = pl.BlockSpec((ROW_TILE, ATT_HEAD_DIM), lambda b, r: (jnp.minimum(r, lat_tiles - 1), 0))
    gain_spec = pl.BlockSpec((1, ATT_HEAD_DIM), lambda b, r: (0, 0))
    return pl.pallas_call(
        functools.partial(_qkvprep_kernel, tiles_per_batch=tiles_per_batch),
        out_shape=(jax.ShapeDtypeStruct((m, BRANCH_W), BF16),
                   jax.ShapeDtypeStruct((m, kvw), BF16),
                   jax.ShapeDtypeStruct((batch, ATT_KV_HEADS, VT_ROWS, tiles_per_batch * ROW_TILE), BF16)),
        grid=(batch, tiles_per_batch),
        in_specs=[pl.BlockSpec((ROW_TILE, BRANCH_W), lambda b, r: (b * tiles_per_batch + r, A_AQ)),
                  pl.BlockSpec((ROW_TILE, 2 * kvw), lambda b, r: (b * tiles_per_batch + r, 0)),
                  rope_spec, rope_spec, gain_spec, gain_spec],
        out_specs=(pl.BlockSpec((ROW_TILE, BRANCH_W), lambda b, r: (b * tiles_per_batch + r, 0)),
                   pl.BlockSpec((ROW_TILE, kvw), lambda b, r: (b * tiles_per_batch + r, 0)),
                   pl.BlockSpec((None, ATT_KV_HEADS, VT_ROWS, ROW_TILE), lambda b, r: (b, 0, 0, r))),
        compiler_params=_cparams(("parallel", "parallel")),
        name="qkvprep",
    )(ya, kv, cos_t, sin_t, q_gain.reshape(1, ATT_HEAD_DIM), k_gain.reshape(1, ATT_HEAD_DIM))


def _attn_kernel(q_ref, z_ref, k_ref, vt_ref, o_ref, *, n_lat_q, t_lat, kc):
    is_lat = pl.program_id(2) < n_lat_q
    hd = ATT_HEAD_DIM
    q = jnp.concatenate([q_ref[:, g * hd:(g + 1) * hd] for g in range(ATT_GROUP)], axis=0)

    def attend(chunks):
        def scores(c):
            off, n = chunks[c]
            return lax.dot_general(k_ref[off:off + n, :], q, (((1,), (1,)), ((), ())),
                                   preferred_element_type=F32)

        m, o = None, None
        s_next = scores(0)
        for c, (off, n) in enumerate(chunks):
            s = s_next
            if c + 1 < len(chunks):
                s_next = scores(c + 1)
            mc = jnp.max(s, axis=0, keepdims=True)
            m_new = mc if m is None else jnp.maximum(m, mc)
            p = jnp.exp2((s - m_new).astype(BF16))
            oc = jnp.dot(vt_ref[:, off:off + n], p, preferred_element_type=F32)
            o = oc if o is None else jnp.exp2(m - m_new) * o + oc
            m = m_new
        o = o[0:hd] / o[hd:hd + 1]
        for g in range(ATT_GROUP):
            sl = slice(g * ATT_HEAD_DIM, (g + 1) * ATT_HEAD_DIM)
            og = o[:, g * ATT_Q_TILE:(g + 1) * ATT_Q_TILE].T
            o_ref[:, sl] = (og * _silu(z_ref[:, sl].astype(F32))).astype(BF16)

    ctx_chunk = (t_lat, CTX_LEN)

    @pl.when(is_lat)
    def _():
        attend([(off, kc) for off in range(0, t_lat, kc)] + [ctx_chunk])

    @pl.when(jnp.logical_not(is_lat))
    def _():
        attend([ctx_chunk])


def _attention(qb, ya, kb, vt, batch, s_len):
    m = ya.shape[0]
    t_lat = s_len - CTX_LEN
    nq = s_len // ATT_Q_TILE
    n_lat_q = t_lat // ATT_Q_TILE
    gw = ATT_GROUP * ATT_HEAD_DIM
    return pl.pallas_call(
        functools.partial(_attn_kernel, n_lat_q=n_lat_q, t_lat=t_lat,
                          kc=_pick_tile(t_lat, (ATT_KEY_CHUNK, 512, 256))),
        out_shape=jax.ShapeDtypeStruct((m, BRANCH_W), BF16),
        grid=(batch, ATT_KV_HEADS, nq),
        in_specs=[pl.BlockSpec((ATT_Q_TILE, gw), lambda b, h, i: (b * nq + i, h)),
                  pl.BlockSpec((ATT_Q_TILE, gw), lambda b, h, i: (b * nq + i, A_AZ * 2 + h)),
                  pl.BlockSpec((s_len, ATT_HEAD_DIM), lambda b, h, i: (b, h)),
                  pl.BlockSpec((None, None, VT_ROWS, s_len), lambda b, h, i: (b, h, 0, 0))],
        out_specs=pl.BlockSpec((ATT_Q_TILE, gw), lambda b, h, i: (b * nq + i, h)),
        compiler_params=_cparams(("parallel", "parallel", "arbitrary")),
        name="attention",
    )(qb, ya, kb, vt)


def _mlstm_kernel(*refs, reverse, final):
    if final:
        (q_ref, k_ref, v_ref, g_ref, gt_ref, bc_ref, bct_ref, o_ref, z_ref, gain_ref, hprev_ref,
         out_ref, c_s, n_s, m_s) = refs
    else:
        q_ref, k_ref, v_ref, g_ref, gt_ref, bc_ref, bct_ref, out_ref, c_s, n_s, m_s = refs
    L = ROW_TILE
    hd = MLSTM_HEAD_DIM

    @pl.when(pl.program_id(1) == 0)
    def _():
        c_s[...] = jnp.zeros(c_s.shape, F32)
        n_s[...] = jnp.zeros(n_s.shape, F32)
        m_s[...] = jnp.full(m_s.shape, M_INIT, F32)

    gates, gates_t = g_ref[...], gt_ref[...]
    bcum, bcum_t = bc_ref[...], bct_ref[...]
    row = lax.broadcasted_iota(jnp.int32, (L, L), 0)
    col = lax.broadcasted_iota(jnp.int32, (L, L), 1)
    tri = (col >= row) if reverse else (col <= row)
    last = 0 if reverse else L - 1
    heads = range(MLSTM_HEADS)
    hsl = [slice(h * hd, (h + 1) * hd) for h in heads]
    icol = [(2 if reverse else 0) * MLSTM_HEADS + h for h in heads]
    fcol = [(3 if reverse else 1) * MLSTM_HEADS + h for h in heads]
    c_old = [c_s[h] for h in heads]
    n_old = [n_s[h, 0:1, :] for h in heads]
    m_old = [m_s[h, 0:1, 0:1] for h in heads]
    q = [q_ref[:, hsl[h]] for h in heads]
    k = [k_ref[:, hsl[h]].astype(F32) * MLSTM_K_SCALE for h in heads]
    v = [v_ref[:, hsl[h]] for h in heads]
    b_col = [bcum[:, fcol[h]:fcol[h] + 1] for h in heads]
    i_col = [gates[:, icol[h]:icol[h] + 1] for h in heads]

    qk = [lax.dot_general(q[h], k[h].astype(BF16), (((1,), (1,)), ((), ())), preferred_element_type=F32)
          for h in heads]
    qc = [jnp.dot(q[h], c_old[h].astype(BF16), preferred_element_type=F32) for h in heads]

    m_t, dmat, inter = [], [], []
    for h in heads:
        b_row, i_row = bcum_t[fcol[h]:fcol[h] + 1, :], gates_t[icol[h]:icol[h] + 1, :]
        log_d = jnp.where(tri, b_col[h] - b_row + i_row, -jnp.inf)
        log_inter = b_col[h] + m_old[h]
        m_t.append(jnp.maximum(log_inter, jnp.max(log_d, axis=1, keepdims=True)))
        dmat.append(jnp.exp(log_d - m_t[h]))
        inter.append(jnp.exp(log_inter - m_t[h]))

    sm = [qk[h] * dmat[h] for h in heads]
    sv = [jnp.dot(sm[h].astype(BF16), v[h], preferred_element_type=F32) for h in heads]

    kw, decay, m_new = [], [], []
    for h in heads:
        b_last = bcum[last:last + 1, fcol[h]:fcol[h] + 1]
        log_w = b_last - b_col[h] + i_col[h]
        m_new.append(jnp.maximum(b_last + m_old[h], jnp.max(log_w, axis=0, keepdims=True)))
        decay.append(jnp.exp(b_last + m_old[h] - m_new[h]))
        kw.append(k[h] * jnp.exp(log_w - m_new[h]))
    kv = [lax.dot_general(kw[h].astype(BF16), v[h], (((0,), (0,)), ((), ())), preferred_element_type=F32)
          for h in heads]

    for h in heads:
        num = inter[h] * qc[h] + sv[h]
        den = inter[h] * jnp.sum(q[h].astype(F32) * n_old[h], axis=1, keepdims=True) \
            + jnp.sum(sm[h], axis=1, keepdims=True)
        hh = num / jnp.maximum(jnp.abs(den), jnp.exp(-m_t[h]))
        if final:
            ht = hh + hprev_ref[:, hsl[h]]
            hn = ht * lax.rsqrt(jnp.mean(ht * ht, axis=-1, keepdims=True) + EPS)
            y = hn * gain_ref[:, hsl[h]] * jax.nn.sigmoid(o_ref[:, hsl[h]].astype(F32)) \
                * _silu(z_ref[:, hsl[h]].astype(F32))
            out_ref[:, hsl[h]] = y.astype(BF16)
        else:
            out_ref[:, hsl[h]] = hh

    for h in heads:
        c_s[h] = decay[h] * c_old[h] + kv[h]
        n_s[h, 0:1, :] = decay[h] * n_old[h] + jnp.sum(kw[h], axis=0, keepdims=True)
        m_s[h] = jnp.broadcast_to(m_new[h], m_s.shape[1:])


def _mlstm_pass(ya, gate_arrays, gain, hprev, batch, tiles_per_batch, reverse):
    m = ya.shape[0]
    lat = tiles_per_batch - 1
    final = hprev is not None

    def rows(b, c):
        r = jnp.where(c == 0, lat, (lat - c) if reverse else (c - 1))
        return b * tiles_per_batch + r

    def col(cidx):
        return pl.BlockSpec((ROW_TILE, BRANCH_W), lambda b, c: (rows(b, c), cidx))

    gate_spec = pl.BlockSpec((ROW_TILE, 128), lambda b, c: (rows(b, c), 0))
    gate_t_spec = pl.BlockSpec((128, ROW_TILE), lambda b, c: (0, rows(b, c)))
    in_specs = [col(A_MQ), col(A_MK), col(A_MV), gate_spec, gate_t_spec, gate_spec, gate_t_spec]
    args = [ya, ya, ya, *gate_arrays]
    if final:
        in_specs += [col(A_MO), col(A_MZ), pl.BlockSpec((1, BRANCH_W), lambda b, c: (0, 0)),
                     pl.BlockSpec((ROW_TILE, BRANCH_W), lambda b, c: (rows(b, c), 0))]
        args += [ya, ya, gain.reshape(1, BRANCH_W), hprev]
    return pl.pallas_call(
        functools.partial(_mlstm_kernel, reverse=reverse, final=final),
        out_shape=jax.ShapeDtypeStruct((m, BRANCH_W), BF16 if final else F32),
        grid=(batch, tiles_per_batch),
        in_specs=in_specs,
        out_specs=pl.BlockSpec((ROW_TILE, BRANCH_W), lambda b, c: (rows(b, c), 0)),
        scratch_shapes=[pltpu.VMEM((MLSTM_HEADS, MLSTM_HEAD_DIM, MLSTM_HEAD_DIM), F32),
                        pltpu.VMEM((MLSTM_HEADS, 8, MLSTM_HEAD_DIM), F32),
                        pltpu.VMEM((MLSTM_HEADS, 8, 128), F32)],
        compiler_params=_cparams(("parallel", "arbitrary")),
        name="mlstm_bwd" if reverse else "mlstm_fwd",
    )(*args)


def _segment_info(r, lat):
    has_prev = jnp.logical_and(r != 0, r != lat)
    has_next = jnp.logical_and(r != lat - 1, r != lat)
    seg_len = jnp.where(r == lat, CTX_LEN, lat * ROW_TILE)
    t0 = jnp.where(r == lat, 0, r * ROW_TILE)
    return has_prev, has_next, seg_len, t0


def _conv_strip(j, r, lat, cu_ref, cb_ref, cc_ref, cz_ref, cu_p, cc_p, cu_n, cc_n, cw_ref):
    has_prev, has_next, _, _ = _segment_info(r, lat)
    rowi = lax.broadcasted_iota(jnp.int32, (ROW_TILE, 1), 0)
    cs = slice(j * 128, (j + 1) * 128)
    a = cc_ref[:, cs].astype(F32) * cu_ref[:, cs].astype(F32)
    a_prev = jnp.where(has_prev, cc_p[HALO - 1:HALO, cs].astype(F32) * cu_p[HALO - 1:HALO, cs].astype(F32), 0.0)
    a_next = jnp.where(has_next, cc_n[0:1, cs].astype(F32) * cu_n[0:1, cs].astype(F32), 0.0)
    a_m1 = jnp.where(rowi == 0, a_prev, pltpu.roll(a, 1, 0))
    a_p1 = jnp.where(rowi == ROW_TILE - 1, a_next, pltpu.roll(a, ROW_TILE - 1, 0))
    y = cw_ref[0:1, cs] * a_m1 + cw_ref[1:2, cs] * a + cw_ref[2:3, cs] * a_p1
    return (cb_ref[:, cs].astype(F32) * y * _silu(cz_ref[:, cs].astype(F32))).astype(BF16)


def _pool_branch(r, lat, pu_ref, pz_ref, pu_p, pu_n, pw_ref, ps_ref, band_ref, yd_ref):
    has_prev, has_next, seg_len, t0 = _segment_info(r, lat)
    rowi = lax.broadcasted_iota(jnp.int32, (ROW_TILE, 1), 0)
    u = pu_ref[...]
    halo_zero = jnp.zeros((HALO, BRANCH_W), u.dtype)
    ext = jnp.concatenate([u, jnp.where(has_prev, pu_p[...], halo_zero), jnp.where(has_next, pu_n[...], halo_zero),
                           jnp.zeros((POOL_K - ROW_TILE - 2 * HALO, BRANCH_W), u.dtype)], axis=0)
    t = t0 + rowi
    groups = range(len(POOL_WINDOWS))
    gsl = [slice(g * POOL_GROUP, (g + 1) * POOL_GROUP) for g in groups]
    acc = [jnp.dot(band_ref[g], ext[:, gsl[g]], preferred_element_type=F32) for g in groups]
    dev = []
    for g, w in enumerate(POOL_WINDOWS):
        inv_cnt = 1.0 / (jnp.minimum(t + (w - w // 2), seg_len) - jnp.maximum(t - w // 2, 0)).astype(F32)
        dev.append((acc[g] * inv_cnt - u[:, gsl[g]].astype(F32)).astype(BF16))
    pg = [jnp.dot(dev[g], pw_ref[g], preferred_element_type=F32) for g in groups]
    for g in groups:
        gate = _silu(pz_ref[:, gsl[g]].astype(F32)) * ps_ref[:, gsl[g]]
        yd_ref[:, gsl[g]] = (pg[g] * gate).astype(BF16)


def _pool_band():
    t = np.arange(ROW_TILE)[:, None]
    pos = np.concatenate([np.arange(ROW_TILE), np.arange(-HALO, 0), np.arange(ROW_TILE, ROW_TILE + HALO),
                          np.full(POOL_K - ROW_TILE - 2 * HALO, -10 * ROW_TILE)])[None, :]
    return np.stack([(pos >= t - w // 2) & (pos < t + w - w // 2) for w in POOL_WINDOWS]).astype(np.float32)


def _mergeout_kernel(*refs, final, tiles_per_batch):
    (b0, b1, cu_ref, cb_ref, cc_ref, cz_ref, pu_ref, pz_ref, cu_p, cc_p, pu_p, cu_n, cc_n, pu_n,
     cw_ref, pw_ref, ps_ref, band_ref, g_ref, wb_ref, wo_ref, x_ref, gt_ref) = refs[:23]
    if final:
        fg_ref, o_ref, yd_s = refs[23:]
    else:
        ng_ref, nsc_ref, nsh_ref, o_ref, h_ref, yd_s = refs[23:]
    r, lat = pl.program_id(1), tiles_per_batch - 1

    def gated(j, br):
        gate = jax.nn.sigmoid(g_ref[:, j * D_MODEL:(j + 1) * D_MODEL].astype(F32))
        return gate * jnp.dot(br, wb_ref[j], preferred_element_type=F32)

    acc = gated(0, b0[...]) + gated(1, b1[...])
    strips = [_conv_strip(j, r, lat, cu_ref, cb_ref, cc_ref, cz_ref, cu_p, cc_p, cu_n, cc_n, cw_ref)
              for j in range(BRANCH_W // 128)]
    _pool_branch(r, lat, pu_ref, pz_ref, pu_p, pu_n, pw_ref, ps_ref, band_ref, yd_s)
    acc = acc + gated(2, jnp.concatenate(strips, axis=1)) + gated(3, yd_s[...])
    y = x_ref[...] + gt_ref[0] * jnp.dot(acc.astype(BF16), wo_ref[...], preferred_element_type=F32)
    yn = y * lax.rsqrt(jnp.mean(y * y, axis=-1, keepdims=True) + EPS)
    if final:
        o_ref[...] = yn * fg_ref[...]
    else:
        o_ref[...] = y
        h_ref[...] = (yn * ng_ref[...] * (1.0 + nsc_ref[0]) + nsh_ref[0]).astype(BF16)


def _mergeout(y_att, y_mls, yb, conv_w, pool_w, pool_scale, w_branch, w_out, layer, xs, gt_t, final_gain,
              next_norm, batch, tiles_per_batch):
    m, d = xs.shape
    final = final_gain is not None
    lat = tiles_per_batch - 1
    n_r = lat if final else tiles_per_batch
    resident = pl.Buffered(1)
    hb = ROW_TILE // HALO
    n_halo = m // HALO
    n_win = len(POOL_WINDOWS)

    def src(b, r):
        return b * tiles_per_batch + r

    def col(cidx):
        return pl.BlockSpec((ROW_TILE, BRANCH_W), lambda b, r: (src(b, r), cidx))

    def prev(cidx):
        return pl.BlockSpec((HALO, BRANCH_W), lambda b, r: (jnp.maximum(src(b, r) * hb - 1, 0), cidx))

    def nxt(cidx):
        return pl.BlockSpec((HALO, BRANCH_W), lambda b, r: (jnp.minimum((src(b, r) + 1) * hb, n_halo - 1), cidx))

    in_specs = [col(0), col(0),
                col(B_CU), col(B_CB), col(B_CC), col(B_CZ), col(B_PU), col(B_PZ),
                prev(B_CU), prev(B_CC), prev(B_PU), nxt(B_CU), nxt(B_CC), nxt(B_PU),
                pl.BlockSpec((None, 3, BRANCH_W), lambda b, r: (layer, 0, 0)),
                pl.BlockSpec((None, n_win, POOL_GROUP, POOL_GROUP), lambda b, r: (layer, 0, 0, 0)),
                pl.BlockSpec((None, 1, BRANCH_W), lambda b, r: (layer, 0, 0)),
                pl.BlockSpec((n_win, ROW_TILE, POOL_K), lambda b, r: (0, 0, 0)),
                pl.BlockSpec((ROW_TILE, N_BRANCH * d), lambda b, r: (src(b, r), B_GATE)),
                pl.BlockSpec((None, N_BRANCH, BRANCH_W, d), lambda b, r: (layer, 0, 0, 0), pipeline_mode=resident),
                pl.BlockSpec((None, d, d), lambda b, r: (layer, 0, 0), pipeline_mode=resident),
                pl.BlockSpec((ROW_TILE, d), lambda b, r: (src(b, r), 0)),
                pl.BlockSpec((1, 1, d), lambda b, r: (src(b, r), 0, 0))]
    args = [y_att, y_mls, *([yb] * 12), conv_w, pool_w, pool_scale.reshape(-1, 1, BRANCH_W),
            jnp.asarray(_pool_band(), BF16), yb, w_branch, w_out, xs, gt_t]
    row_spec = pl.BlockSpec((ROW_TILE, d), lambda b, r: (b * n_r + r, 0))
    out_shape = jax.ShapeDtypeStruct((batch * n_r * ROW_TILE, d), F32)
    if final:
        in_specs.append(pl.BlockSpec((1, d), lambda b, r: (0, 0)))
        args.append(final_gain.reshape(1, d))
        out_specs = row_spec
    else:
        tab_spec = pl.BlockSpec((1, 1, d), lambda b, r: (src(b, r), 0, 0))
        in_specs += [pl.BlockSpec((1, d), lambda b, r: (0, 0)), tab_spec, tab_spec]
        args += [next_norm[0].reshape(1, d), next_norm[1], next_norm[2]]
        out_shape = (out_shape, jax.ShapeDtypeStruct((m, d), BF16))
        out_specs = (row_spec, row_spec)
    return pl.pallas_call(
        functools.partial(_mergeout_kernel, final=final, tiles_per_batch=tiles_per_batch),
        out_shape=out_shape,
        grid=(batch, n_r),
        in_specs=in_specs,
        out_specs=out_specs,
        scratch_shapes=[pltpu.VMEM((ROW_TILE, BRANCH_W), BF16)],
        compiler_params=_cparams(("parallel", "parallel")),
        name="mergeout_final" if final else "mergeout",
    )(*args)


def _rope_tables(t_lat):
    pos = np.arange(t_lat)
    quarter = ATT_HEAD_DIM // 4
    freq = ROPE_THETA ** (-jnp.arange(quarter, dtype=F32) / quarter)
    a_row = jnp.asarray(pos // GRID_W, F32)[:, None] * freq[None, :]
    a_col = jnp.asarray(pos % GRID_W, F32)[:, None] * freq[None, :]
    cos_t = jnp.concatenate([jnp.cos(a_row), jnp.cos(a_row), jnp.cos(a_col), jnp.cos(a_col)], axis=-1)
    sin_t = jnp.concatenate([-jnp.sin(a_row), jnp.sin(a_row), -jnp.sin(a_col), jnp.sin(a_col)], axis=-1)
    return cos_t, sin_t


def kernel(x, c, ctx, c_ctx, norm_gain, w_mod, b_mod, w_in, q_norm_gain, k_norm_gain, mlstm_gate_bias,
           mlstm_norm_gain, conv_w, pool_w, pool_scale, w_branch, w_out, final_norm_gain):
    batch, t_lat, d = x.shape
    depth = w_in.shape[0]
    assert d == D_MODEL and ctx.shape[1] == CTX_LEN and t_lat % ROW_TILE == 0 and batch < 8
    s_len = t_lat + CTX_LEN
    tiles_per_batch = s_len // ROW_TILE

    cc = jnp.zeros((8, d), F32).at[:batch].set(c).at[batch].set(c_ctx)
    mod = _modulation(cc, w_mod, b_mod)
    tile_row = np.array([b if r < tiles_per_batch - 1 else batch
                         for b in range(batch) for r in range(tiles_per_batch)], np.int32)
    cos_t, sin_t = _rope_tables(t_lat)
    assert w_in.shape[1:] == (d, N_IN)
    w_t = jnp.swapaxes(w_in, 1, 2).reshape(depth * N_IN, d)
    w_branch_b, w_out_b, pool_w_b = w_branch.astype(BF16), w_out.astype(BF16), pool_w.astype(BF16)

    tables = []
    for l in range(depth):
        mod_t = mod[l][tile_row][:, None, :]
        tables.append((mod_t[..., :d], mod_t[..., d:2 * d], mod_t[..., 2 * d:]))

    out = None
    h, xs = _normmod_first(x, ctx, norm_gain[0], tables[0][1], tables[0][0])
    for l in range(depth):
        gt_t = tables[l][2]
        bias = jnp.zeros((1, 128), F32).at[0, :4 * MLSTM_HEADS].set(mlstm_gate_bias[l].reshape(-1))
        ya = _inproj(h, w_t, l, 7, lambda j: jnp.where(j == 0, 0, 512 + IN_UNIT * j), "inproj_a")
        yb = _inproj(h, w_t, l, 14, lambda j: jnp.where(j < 8, W_MERGE_ROW + IN_UNIT * j,
                                                       W_LOCAL_ROW + IN_UNIT * (j - 8)), "inproj_b")
        kv, g, g_t, pre, pre_t, suf, suf_t = _inproj_tail(h, w_t, l, bias)

        qb, kb, vt = _qkvprep(ya, kv, cos_t, sin_t, q_norm_gain[l], k_norm_gain[l], batch, tiles_per_batch)
        y_att = _attention(qb, ya, kb, vt, batch, s_len)
        h_f = _mlstm_pass(ya, (g, g_t, pre, pre_t), None, None, batch, tiles_per_batch, reverse=False)
        y_mls = _mlstm_pass(ya, (g, g_t, suf, suf_t), mlstm_norm_gain[l], h_f, batch, tiles_per_batch,
                            reverse=True)
        if l == depth - 1:
            out = _mergeout(y_att, y_mls, yb, conv_w, pool_w_b, pool_scale, w_branch_b, w_out_b, l, xs, gt_t,
                            final_norm_gain, None, batch, tiles_per_batch).reshape(batch, t_lat, d)
        else:
            next_norm = (norm_gain[l + 1], tables[l + 1][1], tables[l + 1][0])
            xs, h = _mergeout(y_att, y_mls, yb, conv_w, pool_w_b, pool_scale, w_branch_b, w_out_b, l, xs, gt_t,
                              None, next_norm, batch, tiles_per_batch)
    return out
```

```python
import functools

import numpy as np
import jax
import jax.numpy as jnp
from jax import lax
from jax.experimental import pallas as pl
from jax.experimental.pallas import tpu as pltpu

F32 = jnp.float32
BF16 = jnp.bfloat16

D_MODEL = 2048
BRANCH_W = 1024
GRID_W = 64
CTX_LEN = 256
EPS = 1e-6
ATT_HEAD_DIM = 128
ATT_GROUP = 4
ATT_KV_HEADS = 2
ATT_SCALE = ATT_HEAD_DIM ** -0.5
LOG2_E = 1.4426950408889634
ROPE_THETA = 10000.0
MLSTM_HEADS = 4
MLSTM_HEAD_DIM = 256
MLSTM_K_SCALE = MLSTM_HEAD_DIM ** -0.5
MLSTM_AUX = 128
M_INIT = -1e30
POOL_WINDOWS = (2, 4, 8, 16)
POOL_GROUP = 256
N_BRANCH = 4

ROW_TILE = 256
HALO = 16
POOL_K = 384
ATT_Q_TILE = 256
ATT_KEY_CHUNK = 256
VT_ROWS = ATT_HEAD_DIM + 16
V7X_VMEM_LIMIT = 56 * 1024 * 1024

A_AQ, A_AZ, A_MQ, A_MK, A_MV, A_MO, A_MZ = range(7)
B_GATE = 0
B_CU, B_CB, B_CC, B_CZ, B_PU, B_PZ = range(8, 14)


def _cparams(sem, vmem=V7X_VMEM_LIMIT):
    return pltpu.CompilerParams(dimension_semantics=sem, vmem_limit_bytes=vmem)


def _silu(x):
    return x * jax.nn.sigmoid(x)


def _mod_kernel(c_ref, w_ref, b_ref, o_ref):
    a = _silu(c_ref[...]).astype(BF16)
    o_ref[...] = jnp.dot(a, w_ref[...].astype(BF16), preferred_element_type=F32) + b_ref[...]


def _modulation(cc, w_mod, b_mod):
    depth, d, n = w_mod.shape
    tn = 768
    return pl.pallas_call(
        _mod_kernel,
        out_shape=jax.ShapeDtypeStruct((depth, 8, n), F32),
        grid=(depth, n // tn),
        in_specs=[pl.BlockSpec((8, d), lambda l, j: (0, 0)),
                  pl.BlockSpec((None, d, tn), lambda l, j: (l, 0, j)),
                  pl.BlockSpec((None, 1, tn), lambda l, j: (l, 0, j))],
        out_specs=pl.BlockSpec((None, 8, tn), lambda l, j: (l, 0, j)),
        compiler_params=_cparams(("parallel", "parallel")),
        name="modulation",
    )(cc, w_mod, b_mod.reshape(depth, 1, n))


def _normmod_first_kernel(x_ref, c_ref, g_ref, sc_ref, sh_ref, o_ref, xs_ref, *, tiles_per_batch):
    x = jnp.where(pl.program_id(1) == tiles_per_batch - 1, c_ref[...], x_ref[...])
    xs_ref[...] = x
    y = x * lax.rsqrt(jnp.mean(x * x, axis=-1, keepdims=True) + EPS) * g_ref[...]
    o_ref[...] = (y * (1.0 + sc_ref[0]) + sh_ref[0]).astype(BF16)


def _normmod_first(x, ctx, gain, sc_t, sh_t):
    batch, t_lat, d = x.shape
    tpb = t_lat // ROW_TILE + 1
    m = batch * tpb * ROW_TILE
    tab_spec = pl.BlockSpec((1, 1, d), lambda b, r: (b * tpb + r, 0, 0))
    row_spec = pl.BlockSpec((ROW_TILE, d), lambda b, r: (b * tpb + r, 0))
    return pl.pallas_call(
        functools.partial(_normmod_first_kernel, tiles_per_batch=tpb),
        out_shape=(jax.ShapeDtypeStruct((m, d), BF16), jax.ShapeDtypeStruct((m, d), F32)),
        grid=(batch, tpb),
        in_specs=[pl.BlockSpec((None, ROW_TILE, d), lambda b, r: (b, jnp.minimum(r, tpb - 2), 0)),
                  pl.BlockSpec((None, CTX_LEN, d), lambda b, r: (b, 0, 0)),
                  pl.BlockSpec((1, d), lambda b, r: (0, 0)),
                  tab_spec, tab_spec],
        out_specs=(row_spec, row_spec),
        compiler_params=_cparams(("parallel", "parallel")),
        name="normmod_first",
    )(x, ctx, gain.reshape(1, d), sc_t, sh_t)


IN_UNIT = 1024
W_KV_ROW, W_MGATE_ROW, W_LOCAL_ROW, W_MERGE_ROW = 1024, 7680, 7696, 13840
N_IN = W_MERGE_ROW + N_BRANCH * D_MODEL


def _pick_tile(n, candidates):
    for t in candidates:
        if n % t == 0:
            return t
    raise ValueError(f"no tile for {n}")


def _inproj_kernel(h_ref, w_ref, o_ref, wt_s):
    @pl.when(pl.program_id(1) == 0)
    def _():
        wt_s[...] = w_ref[...].T.astype(BF16)

    o_ref[...] = jnp.dot(h_ref[...], wt_s[...], preferred_element_type=F32).astype(o_ref.dtype)


def _inproj(h, w_t, layer, n_units, unit_row, name):
    m, k = h.shape
    tm = _pick_tile(m, (1024, 512, 256))
    base = layer * N_IN
    return pl.pallas_call(
        _inproj_kernel,
        out_shape=jax.ShapeDtypeStruct((m, n_units * IN_UNIT), BF16),
        grid=(n_units, m // tm),
        in_specs=[pl.BlockSpec((tm, k), lambda j, i: (i, 0)),
                  pl.BlockSpec((pl.Element(IN_UNIT), pl.Element(k)),
                               lambda j, i: (pl.multiple_of(base + unit_row(j), 8), 0))],
        out_specs=pl.BlockSpec((tm, IN_UNIT), lambda j, i: (i, j)),
        scratch_shapes=[pltpu.VMEM((k, IN_UNIT), BF16)],
        compiler_params=_cparams(("parallel", "arbitrary")),
        name=name,
    )(h, w_t)


def _inproj_tail_kernel(h_ref, wkv_ref, wg_ref, bias_ref,
                        kv_out, g_out, gt_out, pre_out, pret_out, suf_out, suft_out, wt_s):
    kvw = wkv_ref.shape[0]
    L = ROW_TILE

    @pl.when(pl.program_id(0) == 0)
    def _():
        wt_s[:, :kvw] = wkv_ref[...].T.astype(BF16)
        wt_s[:, kvw:] = wg_ref[...].T.astype(BF16)

    y = jnp.dot(h_ref[...], wt_s[...], preferred_element_type=F32)
    kv_out[...] = y[:, :kvw]
    g = y[:, kvw:]
    lane = lax.broadcasted_iota(jnp.int32, g.shape, 1)
    gates = jnp.where(lane < 4 * MLSTM_HEADS, g, 0.0) + bias_ref[...]
    log_f = jnp.minimum(gates, 0.0) - jnp.log1p(jnp.exp(-jnp.abs(gates)))
    p1 = log_f.astype(BF16)
    r1 = log_f - p1.astype(F32)
    p2 = r1.astype(BF16)
    p3 = (r1 - p2.astype(F32)).astype(BF16)
    parts = jnp.concatenate([p1, p2, p3], axis=1)
    row = lax.broadcasted_iota(jnp.int32, (L, L), 0)
    col = lax.broadcasted_iota(jnp.int32, (L, L), 1)
    lower = (col <= row).astype(BF16)
    chunks = [slice(c * L, (c + 1) * L) for c in range(g.shape[0] // L)]
    sums = [jnp.dot(lower, parts[rs], preferred_element_type=F32) for rs in chunks]
    pre = [s[:, 0:128] + s[:, 128:256] + s[:, 256:384] for s in sums]
    suf = [p[L - 1:L, :] - p + log_f[rs] for p, rs in zip(pre, chunks)]
    g_out[...] = gates
    for c, rs in enumerate(chunks):
        pre_out[rs, :] = pre[c]
        suf_out[rs, :] = suf[c]
        gt_out[:, rs] = gates[rs].T
        pret_out[:, rs] = pre[c].T
        suft_out[:, rs] = suf[c].T


def _inproj_tail(h, w_t, layer, bias):
    m, k = h.shape
    tm = _pick_tile(m, (1024, 512, 256))
    kvw = 2 * ATT_KV_HEADS * ATT_HEAD_DIM
    row_sd, col_sd = jax.ShapeDtypeStruct((m, 128), F32), jax.ShapeDtypeStruct((128, m), F32)
    row_spec = pl.BlockSpec((tm, 128), lambda i: (i, 0))
    col_spec = pl.BlockSpec((128, tm), lambda i: (0, i))
    return pl.pallas_call(
        _inproj_tail_kernel,
        out_shape=(jax.ShapeDtypeStruct((m, kvw), F32), row_sd, col_sd, row_sd, col_sd, row_sd, col_sd),
        grid=(m // tm,),
        in_specs=[pl.BlockSpec((tm, k), lambda i: (i, 0)),
                  pl.BlockSpec((pl.Element(kvw), pl.Element(k)), lambda i: (layer * N_IN + W_KV_ROW, 0)),
                  pl.BlockSpec((pl.Element(128), pl.Element(k)), lambda i: (layer * N_IN + W_MGATE_ROW, 0)),
                  pl.BlockSpec((1, 128), lambda i: (0, 0))],
        out_specs=(pl.BlockSpec((tm, kvw), lambda i: (i, 0)),
                   row_spec, col_spec, row_spec, col_spec, row_spec, col_spec),
        scratch_shapes=[pltpu.VMEM((k, kvw + 128), BF16)],
        compiler_params=_cparams(("arbitrary",)),
        name="inproj_tail",
    )(h, w_t, w_t, bias)


def _qkvprep_kernel(q_ref, t_ref, cos_ref, sin_ref, qg_ref, kg_ref, q_out, k_out, vt_out, *, tiles_per_batch):
    is_lat = pl.program_id(1) < tiles_per_batch - 1
    cos, sin, kg = cos_ref[...], sin_ref[...], kg_ref[...]
    qg = qg_ref[...] * (ATT_SCALE * LOG2_E)
    hd = ATT_HEAD_DIM
    n_q = ATT_KV_HEADS * ATT_GROUP
    sls = [slice(h * hd, (h + 1) * hd) for h in range(n_q)]
    xs = [q_ref[:, sl].astype(F32) for sl in sls] + [t_ref[:, sl] for sl in sls[:ATT_KV_HEADS]]
    gains = [qg] * n_q + [kg] * ATT_KV_HEADS
    xs = [x * lax.rsqrt(jnp.mean(x * x, axis=-1, keepdims=True) + EPS) * g for x, g in zip(xs, gains)]
    lane = lax.broadcasted_iota(jnp.int32, xs[0].shape, 1)
    low = (lane % 64) < 32
    up = [pltpu.roll(x, 96, 1) for x in xs]
    down = [pltpu.roll(x, 32, 1) for x in xs]
    outs = [jnp.where(is_lat, x * cos + jnp.where(low, a, b) * sin, x).astype(BF16)
            for x, a, b in zip(xs, up, down)]
    for h in range(n_q):
        q_out[:, sls[h]] = outs[h]
    for h in range(ATT_KV_HEADS):
        k_out[:, sls[h]] = outs[n_q + h]
        vt_out[h, 0:hd, :] = t_ref[:, sls[ATT_KV_HEADS + h]].T.astype(BF16)
        vt_out[h, hd:, :] = jnp.ones((VT_ROWS - hd, ROW_TILE), BF16)


def _qkvprep(ya, kv, cos_t, sin_t, q_gain, k_gain, batch, tiles_per_batch):
    m = kv.shape[0]
    lat_tiles = tiles_per_batch - 1
    kvw = ATT_KV_HEADS * ATT_HEAD_DIM
    rope_spec = pl.BlockSpec((ROW_TILE, ATT_HEAD_DIM), lambda b, r: (jnp.minimum(r, lat_tiles - 1), 0))
    gain_spec = pl.BlockSpec((1, ATT_HEAD_DIM), lambda b, r: (0, 0))
    return pl.pallas_call(
        functools.partial(_qkvprep_kernel, tiles_per_batch=tiles_per_batch),
        out_shape=(jax.ShapeDtypeStruct((m, BRANCH_W), BF16),
                   jax.ShapeDtypeStruct((m, kvw), BF16),
                   jax.ShapeDtypeStruct((batch, ATT_KV_HEADS, VT_ROWS, tiles_per_batch * ROW_TILE), BF16)),
        grid=(batch, tiles_per_batch),
        in_specs=[pl.BlockSpec((ROW_TILE, BRANCH_W), lambda b, r: (b * tiles_per_batch + r, A_AQ)),
                  pl.BlockSpec((ROW_TILE, 2 * kvw), lambda b, r: (b * tiles_per_batch + r, 0)),
                  rope_spec, rope_spec, gain_spec, gain_spec],
        out_specs=(pl.BlockSpec((ROW_TILE, BRANCH_W), lambda b, r: (b * tiles_per_batch + r, 0)),
                   pl.BlockSpec((ROW_TILE, kvw), lambda b, r: (b * tiles_per_batch + r, 0)),
                   pl.BlockSpec((None, ATT_KV_HEADS, VT_ROWS, ROW_TILE), lambda b, r: (b, 0, 0, r))),
        compiler_params=_cparams(("parallel", "parallel")),
        name="qkvprep",
    )(ya, kv, cos_t, sin_t, q_gain.reshape(1, ATT_HEAD_DIM), k_gain.reshape(1, ATT_HEAD_DIM))


def _attn_kernel(q_ref, z_ref, k_ref, vt_ref, o_ref, *, n_lat_q, t_lat, kc):
    is_lat = pl.program_id(2) < n_lat_q
    hd = ATT_HEAD_DIM
    q = jnp.concatenate([q_ref[:, g * hd:(g + 1) * hd] for g in range(ATT_GROUP)], axis=0)

    def attend(chunks):
        def scores(c):
            off, n = chunks[c]
            return lax.dot_general(k_ref[off:off + n, :], q, (((1,), (1,)), ((), ())),
                                   preferred_element_type=F32)

        m, o = None, None
        s_next = scores(0)
        for c, (off, n) in enumerate(chunks):
            s = s_next
            if c + 1 < len(chunks):
                s_next = scores(c + 1)
            mc = jnp.max(s, axis=0, keepdims=True)
            m_new = mc if m is None else jnp.maximum(m, mc)
            p = jnp.exp2((s - m_new).astype(BF16))
            oc = jnp.dot(vt_ref[:, off:off + n], p, preferred_element_type=F32)
            o = oc if o is None else jnp.exp2(m - m_new) * o + oc
            m = m_new
        o = o[0:hd] / o[hd:hd + 1]
        for g in range(ATT_GROUP):
            sl = slice(g * ATT_HEAD_DIM, (g + 1) * ATT_HEAD_DIM)
            og = o[:, g * ATT_Q_TILE:(g + 1) * ATT_Q_TILE].T
            o_ref[:, sl] = (og * _silu(z_ref[:, sl].astype(F32))).astype(BF16)

    ctx_chunk = (t_lat, CTX_LEN)

    @pl.when(is_lat)
    def _():
        attend([(off, kc) for off in range(0, t_lat, kc)] + [ctx_chunk])

    @pl.when(jnp.logical_not(is_lat))
    def _():
        attend([ctx_chunk])


def _attention(qb, ya, kb, vt, batch, s_len):
    m = ya.shape[0]
    t_lat = s_len - CTX_LEN
    nq = s_len // ATT_Q_TILE
    n_lat_q = t_lat // ATT_Q_TILE
    gw = ATT_GROUP * ATT_HEAD_DIM
    return pl.pallas_call(
        functools.partial(_attn_kernel, n_lat_q=n_lat_q, t_lat=t_lat,
                          kc=_pick_tile(t_lat, (ATT_KEY_CHUNK, 512, 256))),
        out_shape=jax.ShapeDtypeStruct((m, BRANCH_W), BF16),
        grid=(batch, ATT_KV_HEADS, nq),
        in_specs=[pl.BlockSpec((ATT_Q_TILE, gw), lambda b, h, i: (b * nq + i, h)),
                  pl.BlockSpec((ATT_Q_TILE, gw), lambda b, h, i: (b * nq + i, A_AZ * 2 + h)),
                  pl.BlockSpec((s_len, ATT_HEAD_DIM), lambda b, h, i: (b, h)),
                  pl.BlockSpec((None, None, VT_ROWS, s_len), lambda b, h, i: (b, h, 0, 0))],
        out_specs=pl.BlockSpec((ATT_Q_TILE, gw), lambda b, h, i: (b * nq + i, h)),
        compiler_params=_cparams(("parallel", "parallel", "arbitrary")),
        name="attention",
    )(qb, ya, kb, vt)


def _mlstm_kernel(*refs, reverse, final):
    if final:
        (q_ref, k_ref, v_ref, g_ref, gt_ref, bc_ref, bct_ref, o_ref, z_ref, gain_ref, hprev_ref,
         out_ref, c_s, m_s) = refs
    else:
        q_ref, k_ref, v_ref, g_ref, gt_ref, bc_ref, bct_ref, out_ref, c_s, m_s = refs
    L = ROW_TILE
    hd = MLSTM_HEAD_DIM
    aux = MLSTM_AUX

    @pl.when(pl.program_id(1) == 0)
    def _():
        c_s[...] = jnp.zeros(c_s.shape, F32)
        m_s[...] = jnp.full(m_s.shape, M_INIT, F32)

    def lanes(x):
        return jnp.concatenate([x] * (hd // aux), axis=1)

    gates, gates_t = g_ref[...], gt_ref[...]
    bcum, bcum_t = bc_ref[...], bct_ref[...]
    row = lax.broadcasted_iota(jnp.int32, (L, L), 0)
    col = lax.broadcasted_iota(jnp.int32, (L, L), 1)
    tri = (col >= row) if reverse else (col <= row)
    last = 0 if reverse else L - 1
    heads = range(MLSTM_HEADS)
    hsl = [slice(h * hd, (h + 1) * hd) for h in heads]
    icol = [(2 if reverse else 0) * MLSTM_HEADS + h for h in heads]
    fcol = [(3 if reverse else 1) * MLSTM_HEADS + h for h in heads]
    c_old = [c_s[h] for h in heads]
    m_old = [m_s[h, 0:1, 0:1] for h in heads]
    q = [q_ref[:, hsl[h]] for h in heads]
    k = [k_ref[:, hsl[h]].astype(F32) * MLSTM_K_SCALE for h in heads]
    ones = jnp.ones((L, aux), v_ref.dtype)
    v = [jnp.concatenate([v_ref[:, hsl[h]], ones], axis=1) for h in heads]
    b_col = [bcum[:, fcol[h]:fcol[h] + 1] for h in heads]
    i_col = [gates[:, icol[h]:icol[h] + 1] for h in heads]

    qk = [lax.dot_general(q[h], k[h].astype(BF16), (((1,), (1,)), ((), ())), preferred_element_type=F32)
          for h in heads]
    qc = [jnp.dot(q[h], c_old[h].astype(BF16), preferred_element_type=F32) for h in heads]

    g_rep, sm = [], []
    for h in heads:
        a_row = gates_t[icol[h]:icol[h] + 1, :] - bcum_t[fcol[h]:fcol[h] + 1, :]
        masked = jnp.where(tri, a_row, -jnp.inf)
        g_t = jnp.maximum(m_old[h], jnp.max(masked, axis=1, keepdims=True))
        g_rep.append(jnp.broadcast_to(g_t, (L, aux)))
        sm.append((qk[h] * jnp.exp(masked - lanes(g_rep[h]))).astype(BF16))
    sv = [jnp.dot(sm[h], v[h], preferred_element_type=F32) for h in heads]

    kw, decay, m_new = [], [], []
    for h in heads:
        b_last = bcum[last:last + 1, fcol[h]:fcol[h] + 1]
        log_w = b_last - b_col[h] + i_col[h]
        m_new.append(jnp.maximum(b_last + m_old[h], jnp.max(log_w, axis=0, keepdims=True)))
        decay.append(jnp.exp(b_last + m_old[h] - m_new[h]))
        kw.append((k[h] * jnp.exp(log_w - m_new[h])).astype(BF16))
    kv = [lax.dot_general(kw[h], v[h], (((0,), (0,)), ((), ())), preferred_element_type=F32) for h in heads]

    for h in heads:
        inter = jnp.exp(m_old[h] - g_rep[h])
        floor = jnp.exp(-(jnp.broadcast_to(b_col[h], (L, aux)) + g_rep[h]))
        den = inter * qc[h][:, hd:] + sv[h][:, hd:]
        inv = 1.0 / jnp.maximum(jnp.abs(den), floor)
        hh = (lanes(inter) * qc[h][:, :hd] + sv[h][:, :hd]) * lanes(inv)
        if final:
            ht = hh + hprev_ref[:, hsl[h]]
            hn = ht * lax.rsqrt(jnp.mean(ht * ht, axis=-1, keepdims=True) + EPS)
            y = hn * gain_ref[:, hsl[h]] * jax.nn.sigmoid(o_ref[:, hsl[h]].astype(F32)) \
                * _silu(z_ref[:, hsl[h]].astype(F32))
            out_ref[:, hsl[h]] = y.astype(BF16)
        else:
            out_ref[:, hsl[h]] = hh

    for h in heads:
        c_s[h] = decay[h] * c_old[h] + kv[h]
        m_s[h] = jnp.broadcast_to(m_new[h], m_s.shape[1:])


def _mlstm_pass(ya, gate_arrays, gain, hprev, batch, tiles_per_batch, reverse):
    m = ya.shape[0]
    lat = tiles_per_batch - 1
    final = hprev is not None

    def rows(b, c):
        r = jnp.where(c == 0, lat, (lat - c) if reverse else (c - 1))
        return b * tiles_per_batch + r

    def col(cidx):
        return pl.BlockSpec((ROW_TILE, BRANCH_W), lambda b, c: (rows(b, c), cidx))

    gate_spec = pl.BlockSpec((ROW_TILE, 128), lambda b, c: (rows(b, c), 0))
    gate_t_spec = pl.BlockSpec((128, ROW_TILE), lambda b, c: (0, rows(b, c)))
    in_specs = [col(A_MQ), col(A_MK), col(A_MV), gate_spec, gate_t_spec, gate_spec, gate_t_spec]
    args = [ya, ya, ya, *gate_arrays]
    if final:
        in_specs += [col(A_MO), col(A_MZ), pl.BlockSpec((1, BRANCH_W), lambda b, c: (0, 0)),
                     pl.BlockSpec((ROW_TILE, BRANCH_W), lambda b, c: (rows(b, c), 0))]
        args += [ya, ya, gain.reshape(1, BRANCH_W), hprev]
    return pl.pallas_call(
        functools.partial(_mlstm_kernel, reverse=reverse, final=final),
        out_shape=jax.ShapeDtypeStruct((m, BRANCH_W), BF16 if final else F32),
        grid=(batch, tiles_per_batch),
        in_specs=in_specs,
        out_specs=pl.BlockSpec((ROW_TILE, BRANCH_W), lambda b, c: (rows(b, c), 0)),
        scratch_shapes=[pltpu.VMEM((MLSTM_HEADS, MLSTM_HEAD_DIM, MLSTM_HEAD_DIM + MLSTM_AUX), F32),
                        pltpu.VMEM((MLSTM_HEADS, 8, 128), F32)],
        compiler_params=_cparams(("parallel", "arbitrary")),
        name="mlstm_bwd" if reverse else "mlstm_fwd",
    )(*args)


def _segment_info(r, lat):
    has_prev = jnp.logical_and(r != 0, r != lat)
    has_next = jnp.logical_and(r != lat - 1, r != lat)
    seg_len = jnp.where(r == lat, CTX_LEN, lat * ROW_TILE)
    t0 = jnp.where(r == lat, 0, r * ROW_TILE)
    return has_prev, has_next, seg_len, t0


def _conv_strip(j, r, lat, cu_ref, cb_ref, cc_ref, cz_ref, cu_p, cc_p, cu_n, cc_n, cw_ref):
    has_prev, has_next, _, _ = _segment_info(r, lat)
    rowi = lax.broadcasted_iota(jnp.int32, (ROW_TILE, 1), 0)
    cs = slice(j * 128, (j + 1) * 128)
    a = cc_ref[:, cs].astype(F32) * cu_ref[:, cs].astype(F32)
    a_prev = jnp.where(has_prev, cc_p[HALO - 1:HALO, cs].astype(F32) * cu_p[HALO - 1:HALO, cs].astype(F32), 0.0)
    a_next = jnp.where(has_next, cc_n[0:1, cs].astype(F32) * cu_n[0:1, cs].astype(F32), 0.0)
    a_m1 = jnp.where(rowi == 0, a_prev, pltpu.roll(a, 1, 0))
    a_p1 = jnp.where(rowi == ROW_TILE - 1, a_next, pltpu.roll(a, ROW_TILE - 1, 0))
    y = cw_ref[0:1, cs] * a_m1 + cw_ref[1:2, cs] * a + cw_ref[2:3, cs] * a_p1
    return (cb_ref[:, cs].astype(F32) * y * _silu(cz_ref[:, cs].astype(F32))).astype(BF16)


def _pool_branch(r, lat, pu_ref, pz_ref, pu_p, pu_n, pw_ref, ps_ref, band_ref, yd_ref):
    has_prev, has_next, seg_len, t0 = _segment_info(r, lat)
    rowi = lax.broadcasted_iota(jnp.int32, (ROW_TILE, 1), 0)
    u = pu_ref[...]
    halo_zero = jnp.zeros((HALO, BRANCH_W), u.dtype)
    ext = jnp.concatenate([u, jnp.where(has_prev, pu_p[...], halo_zero), jnp.where(has_next, pu_n[...], halo_zero),
                           jnp.zeros((POOL_K - ROW_TILE - 2 * HALO, BRANCH_W), u.dtype)], axis=0)
    t = t0 + rowi
    groups = range(len(POOL_WINDOWS))
    gsl = [slice(g * POOL_GROUP, (g + 1) * POOL_GROUP) for g in groups]
    acc = [jnp.dot(band_ref[g], ext[:, gsl[g]], preferred_element_type=F32) for g in groups]
    dev = []
    for g, w in enumerate(POOL_WINDOWS):
        inv_cnt = 1.0 / (jnp.minimum(t + (w - w // 2), seg_len) - jnp.maximum(t - w // 2, 0)).astype(F32)
        dev.append((acc[g] * inv_cnt - u[:, gsl[g]].astype(F32)).astype(BF16))
    pg = [jnp.dot(dev[g], pw_ref[g], preferred_element_type=F32) for g in groups]
    for g in groups:
        gate = _silu(pz_ref[:, gsl[g]].astype(F32)) * ps_ref[:, gsl[g]]
        yd_ref[:, gsl[g]] = (pg[g] * gate).astype(BF16)


def _pool_band():
    t = np.arange(ROW_TILE)[:, None]
    pos = np.concatenate([np.arange(ROW_TILE), np.arange(-HALO, 0), np.arange(ROW_TILE, ROW_TILE + HALO),
                          np.full(POOL_K - ROW_TILE - 2 * HALO, -10 * ROW_TILE)])[None, :]
    return np.stack([(pos >= t - w // 2) & (pos < t + w - w // 2) for w in POOL_WINDOWS]).astype(np.float32)


def _mergeout_kernel(*refs, final, tiles_per_batch):
    (b0, b1, cu_ref, cb_ref, cc_ref, cz_ref, pu_ref, pz_ref, cu_p, cc_p, pu_p, cu_n, cc_n, pu_n,
     cw_ref, pw_ref, ps_ref, band_ref, g_ref, wb_ref, wo_ref, x_ref, gt_ref) = refs[:23]
    if final:
        fg_ref, o_ref, yd_s = refs[23:]
    else:
        ng_ref, nsc_ref, nsh_ref, o_ref, h_ref, yd_s = refs[23:]
    r, lat = pl.program_id(1), tiles_per_batch - 1

    def gated(j, br):
        gate = jax.nn.sigmoid(g_ref[:, j * D_MODEL:(j + 1) * D_MODEL].astype(F32))
        return gate * jnp.dot(br, wb_ref[j], preferred_element_type=F32)

    acc = gated(0, b0[...]) + gated(1, b1[...])
    strips = [_conv_strip(j, r, lat, cu_ref, cb_ref, cc_ref, cz_ref, cu_p, cc_p, cu_n, cc_n, cw_ref)
              for j in range(BRANCH_W // 128)]
    _pool_branch(r, lat, pu_ref, pz_ref, pu_p, pu_n, pw_ref, ps_ref, band_ref, yd_s)
    acc = acc + gated(2, jnp.concatenate(strips, axis=1)) + gated(3, yd_s[...])
    y = x_ref[...] + gt_ref[0] * jnp.dot(acc.astype(BF16), wo_ref[...], preferred_element_type=F32)
    yn = y * lax.rsqrt(jnp.mean(y * y, axis=-1, keepdims=True) + EPS)
    if final:
        o_ref[...] = yn * fg_ref[...]
    else:
        o_ref[...] = y
        h_ref[...] = (yn * ng_ref[...] * (1.0 + nsc_ref[0]) + nsh_ref[0]).astype(BF16)


def _mergeout(y_att, y_mls, yb, conv_w, pool_w, pool_scale, w_branch, w_out, layer, xs, gt_t, final_gain,
              next_norm, batch, tiles_per_batch):
    m, d = xs.shape
    final = final_gain is not None
    lat = tiles_per_batch - 1
    n_r = lat if final else tiles_per_batch
    resident = pl.Buffered(1)
    hb = ROW_TILE // HALO
    n_halo = m // HALO
    n_win = len(POOL_WINDOWS)

    def src(b, r):
        return b * tiles_per_batch + r

    def col(cidx):
        return pl.BlockSpec((ROW_TILE, BRANCH_W), lambda b, r: (src(b, r), cidx))

    def prev(cidx):
        return pl.BlockSpec((HALO, BRANCH_W), lambda b, r: (jnp.maximum(src(b, r) * hb - 1, 0), cidx))

    def nxt(cidx):
        return pl.BlockSpec((HALO, BRANCH_W), lambda b, r: (jnp.minimum((src(b, r) + 1) * hb, n_halo - 1), cidx))

    in_specs = [col(0), col(0),
                col(B_CU), col(B_CB), col(B_CC), col(B_CZ), col(B_PU), col(B_PZ),
                prev(B_CU), prev(B_CC), prev(B_PU), nxt(B_CU), nxt(B_CC), nxt(B_PU),
                pl.BlockSpec((None, 3, BRANCH_W), lambda b, r: (layer, 0, 0)),
                pl.BlockSpec((None, n_win, POOL_GROUP, POOL_GROUP), lambda b, r: (layer, 0, 0, 0)),
                pl.BlockSpec((None, 1, BRANCH_W), lambda b, r: (layer, 0, 0)),
                pl.BlockSpec((n_win, ROW_TILE, POOL_K), lambda b, r: (0, 0, 0)),
                pl.BlockSpec((ROW_TILE, N_BRANCH * d), lambda b, r: (src(b, r), B_GATE)),
                pl.BlockSpec((None, N_BRANCH, BRANCH_W, d), lambda b, r: (layer, 0, 0, 0), pipeline_mode=resident),
                pl.BlockSpec((None, d, d), lambda b, r: (layer, 0, 0), pipeline_mode=resident),
                pl.BlockSpec((ROW_TILE, d), lambda b, r: (src(b, r), 0)),
                pl.BlockSpec((1, 1, d), lambda b, r: (src(b, r), 0, 0))]
    args = [y_att, y_mls, *([yb] * 12), conv_w, pool_w, pool_scale.reshape(-1, 1, BRANCH_W),
            jnp.asarray(_pool_band(), BF16), yb, w_branch, w_out, xs, gt_t]
    row_spec = pl.BlockSpec((ROW_TILE, d), lambda b, r: (b * n_r + r, 0))
    out_shape = jax.ShapeDtypeStruct((batch * n_r * ROW_TILE, d), F32)
    if final:
        in_specs.append(pl.BlockSpec((1, d), lambda b, r: (0, 0)))
        args.append(final_gain.reshape(1, d))
        out_specs = row_spec
    else:
        tab_spec = pl.BlockSpec((1, 1, d), lambda b, r: (src(b, r), 0, 0))
        in_specs += [pl.BlockSpec((1, d), lambda b, r: (0, 0)), tab_spec, tab_spec]
        args += [next_norm[0].reshape(1, d), next_norm[1], next_norm[2]]
        out_shape = (out_shape, jax.ShapeDtypeStruct((m, d), BF16))
        out_specs = (row_spec, row_spec)
    return pl.pallas_call(
        functools.partial(_mergeout_kernel, final=final, tiles_per_batch=tiles_per_batch),
        out_shape=out_shape,
        grid=(batch, n_r),
        in_specs=in_specs,
        out_specs=out_specs,
        scratch_shapes=[pltpu.VMEM((ROW_TILE, BRANCH_W), BF16)],
        compiler_params=_cparams(("parallel", "parallel")),
        name="mergeout_final" if final else "mergeout",
    )(*args)


def _rope_tables(t_lat):
    pos = np.arange(t_lat)
    quarter = ATT_HEAD_DIM // 4
    freq = ROPE_THETA ** (-jnp.arange(quarter, dtype=F32) / quarter)
    a_row = jnp.asarray(pos // GRID_W, F32)[:, None] * freq[None, :]
    a_col = jnp.asarray(pos % GRID_W, F32)[:, None] * freq[None, :]
    cos_t = jnp.concatenate([jnp.cos(a_row), jnp.cos(a_row), jnp.cos(a_col), jnp.cos(a_col)], axis=-1)
    sin_t = jnp.concatenate([-jnp.sin(a_row), jnp.sin(a_row), -jnp.sin(a_col), jnp.sin(a_col)], axis=-1)
    return cos_t, sin_t


def kernel(x, c, ctx, c_ctx, norm_gain, w_mod, b_mod, w_in, q_norm_gain, k_norm_gain, mlstm_gate_bias,
           mlstm_norm_gain, conv_w, pool_w, pool_scale, w_branch, w_out, final_norm_gain):
    batch, t_lat, d = x.shape
    depth = w_in.shape[0]
    assert d == D_MODEL and ctx.shape[1] == CTX_LEN and t_lat % ROW_TILE == 0 and batch < 8
    s_len = t_lat + CTX_LEN
    tiles_per_batch = s_len // ROW_TILE

    cc = jnp.zeros((8, d), F32).at[:batch].set(c).at[batch].set(c_ctx)
    mod = _modulation(cc, w_mod, b_mod)
    tile_row = np.array([b if r < tiles_per_batch - 1 else batch
                         for b in range(batch) for r in range(tiles_per_batch)], np.int32)
    cos_t, sin_t = _rope_tables(t_lat)
    assert w_in.shape[1:] == (d, N_IN)
    w_t = jnp.swapaxes(w_in, 1, 2).reshape(depth * N_IN, d)
    w_branch_b, w_out_b, pool_w_b = w_branch.astype(BF16), w_out.astype(BF16), pool_w.astype(BF16)

    tables = []
    for l in range(depth):
        mod_t = mod[l][tile_row][:, None, :]
        tables.append((mod_t[..., :d], mod_t[..., d:2 * d], mod_t[..., 2 * d:]))

    out = None
    h, xs = _normmod_first(x, ctx, norm_gain[0], tables[0][1], tables[0][0])
    for l in range(depth):
        gt_t = tables[l][2]
        bias = jnp.zeros((1, 128), F32).at[0, :4 * MLSTM_HEADS].set(mlstm_gate_bias[l].reshape(-1))
        ya = _inproj(h, w_t, l, 7, lambda j: jnp.where(j == 0, 0, 512 + IN_UNIT * j), "inproj_a")
        yb = _inproj(h, w_t, l, 14, lambda j: jnp.where(j < 8, W_MERGE_ROW + IN_UNIT * j,
                                                       W_LOCAL_ROW + IN_UNIT * (j - 8)), "inproj_b")
        kv, g, g_t, pre, pre_t, suf, suf_t = _inproj_tail(h, w_t, l, bias)

        qb, kb, vt = _qkvprep(ya, kv, cos_t, sin_t, q_norm_gain[l], k_norm_gain[l], batch, tiles_per_batch)
        y_att = _attention(qb, ya, kb, vt, batch, s_len)
        h_f = _mlstm_pass(ya, (g, g_t, pre, pre_t), None, None, batch, tiles_per_batch, reverse=False)
        y_mls = _mlstm_pass(ya, (g, g_t, suf, suf_t), mlstm_norm_gain[l], h_f, batch, tiles_per_batch,
                            reverse=True)
        if l == depth - 1:
            out = _mergeout(y_att, y_mls, yb, conv_w, pool_w_b, pool_scale, w_branch_b, w_out_b, l, xs, gt_t,
                            final_norm_gain, None, batch, tiles_per_batch).reshape(batch, t_lat, d)
        else:
            next_norm = (norm_gain[l + 1], tables[l + 1][1], tables[l + 1][0])
            xs, h = _mergeout(y_att, y_mls, yb, conv_w, pool_w_b, pool_scale, w_branch_b, w_out_b, l, xs, gt_t,
                              None, next_norm, batch, tiles_per_batch)
    return out
```

```python
import functools

import numpy as np
import jax
import jax.numpy as jnp
from jax import lax
from jax.experimental import pallas as pl
from jax.experimental.pallas import tpu as pltpu

F32 = jnp.float32
BF16 = jnp.bfloat16

D_MODEL = 2048
BRANCH_W = 1024
GRID_W = 64
CTX_LEN = 256
EPS = 1e-6
ATT_HEAD_DIM = 128
ATT_GROUP = 4
ATT_KV_HEADS = 2
ATT_SCALE = ATT_HEAD_DIM ** -0.5
LOG2_E = 1.4426950408889634
ROPE_THETA = 10000.0
MLSTM_HEADS = 4
MLSTM_HEAD_DIM = 256
MLSTM_K_SCALE = MLSTM_HEAD_DIM ** -0.5
MLSTM_AUX = 128
M_INIT = -1e30
POOL_WINDOWS = (2, 4, 8, 16)
POOL_GROUP = 256
N_BRANCH = 4

ROW_TILE = 256
HALO = 16
POOL_K = 384
ATT_Q_TILE = 256
ATT_KEY_CHUNK = 256
VT_ROWS = ATT_HEAD_DIM + 16
V7X_VMEM_LIMIT = 56 * 1024 * 1024

A_AQ, A_AZ, A_MQ, A_MK, A_MV, A_MO, A_MZ = range(7)
B_GATE = 0
B_CU, B_CB, B_CC, B_CZ, B_PU, B_PZ = range(8, 14)


def _cparams(sem, vmem=V7X_VMEM_LIMIT):
    return pltpu.CompilerParams(dimension_semantics=sem, vmem_limit_bytes=vmem)


def _silu(x):
    return x * jax.nn.sigmoid(x)


def _mod_kernel(c_ref, w_ref, b_ref, o_ref):
    a = _silu(c_ref[...]).astype(BF16)
    o_ref[...] = jnp.dot(a, w_ref[...].astype(BF16), preferred_element_type=F32) + b_ref[...]


def _modulation(cc, w_mod, b_mod):
    depth, d, n = w_mod.shape
    tn = 768
    return pl.pallas_call(
        _mod_kernel,
        out_shape=jax.ShapeDtypeStruct((depth, 8, n), F32),
        grid=(depth, n // tn),
        in_specs=[pl.BlockSpec((8, d), lambda l, j: (0, 0)),
                  pl.BlockSpec((None, d, tn), lambda l, j: (l, 0, j)),
                  pl.BlockSpec((None, 1, tn), lambda l, j: (l, 0, j))],
        out_specs=pl.BlockSpec((None, 8, tn), lambda l, j: (l, 0, j)),
        compiler_params=_cparams(("parallel", "parallel")),
        name="modulation",
    )(cc, w_mod, b_mod.reshape(depth, 1, n))


def _normmod_first_kernel(x_ref, c_ref, g_ref, sc_ref, sh_ref, o_ref, xs_ref, *, tiles_per_batch):
    x = jnp.where(pl.program_id(1) == tiles_per_batch - 1, c_ref[...], x_ref[...])
    xs_ref[...] = x
    y = x * lax.rsqrt(jnp.mean(x * x, axis=-1, keepdims=True) + EPS) * g_ref[...]
    o_ref[...] = (y * (1.0 + sc_ref[0]) + sh_ref[0]).astype(BF16)


def _normmod_first(x, ctx, gain, sc_t, sh_t):
    batch, t_lat, d = x.shape
    tpb = t_lat // ROW_TILE + 1
    m = batch * tpb * ROW_TILE
    tab_spec = pl.BlockSpec((1, 1, d), lambda b, r: (b * tpb + r, 0, 0))
    row_spec = pl.BlockSpec((ROW_TILE, d), lambda b, r: (b * tpb + r, 0))
    return pl.pallas_call(
        functools.partial(_normmod_first_kernel, tiles_per_batch=tpb),
        out_shape=(jax.ShapeDtypeStruct((m, d), BF16), jax.ShapeDtypeStruct((m, d), F32)),
        grid=(batch, tpb),
        in_specs=[pl.BlockSpec((None, ROW_TILE, d), lambda b, r: (b, jnp.minimum(r, tpb - 2), 0)),
                  pl.BlockSpec((None, CTX_LEN, d), lambda b, r: (b, 0, 0)),
                  pl.BlockSpec((1, d), lambda b, r: (0, 0)),
                  tab_spec, tab_spec],
        out_specs=(row_spec, row_spec),
        compiler_params=_cparams(("parallel", "parallel")),
        name="normmod_first",
    )(x, ctx, gain.reshape(1, d), sc_t, sh_t)


IN_UNIT = 1024
W_KV_ROW, W_MGATE_ROW, W_LOCAL_ROW, W_MERGE_ROW = 1024, 7680, 7696, 13840
N_IN = W_MERGE_ROW + N_BRANCH * D_MODEL


def _pick_tile(n, candidates):
    for t in candidates:
        if n % t == 0:
            return t
    raise ValueError(f"no tile for {n}")


def _inproj_kernel(h_ref, w_ref, o_ref, wt_s):
    @pl.when(pl.program_id(1) == 0)
    def _():
        wt_s[...] = w_ref[...].T.astype(BF16)

    o_ref[...] = jnp.dot(h_ref[...], wt_s[...], preferred_element_type=F32).astype(o_ref.dtype)


def _inproj(h, w_t, layer, n_units, unit_row, name):
    m, k = h.shape
    tm = _pick_tile(m, (1024, 512, 256))
    base = layer * N_IN
    return pl.pallas_call(
        _inproj_kernel,
        out_shape=jax.ShapeDtypeStruct((m, n_units * IN_UNIT), BF16),
        grid=(n_units, m // tm),
        in_specs=[pl.BlockSpec((tm, k), lambda j, i: (i, 0)),
                  pl.BlockSpec((pl.Element(IN_UNIT), pl.Element(k)),
                               lambda j, i: (pl.multiple_of(base + unit_row(j), 8), 0))],
        out_specs=pl.BlockSpec((tm, IN_UNIT), lambda j, i: (i, j)),
        scratch_shapes=[pltpu.VMEM((k, IN_UNIT), BF16)],
        compiler_params=_cparams(("parallel", "arbitrary")),
        name=name,
    )(h, w_t)


def _inproj_tail_kernel(h_ref, wkv_ref, wg_ref, bias_ref,
                        kv_out, g_out, gt_out, pre_out, pret_out, suf_out, suft_out, wt_s):
    kvw = wkv_ref.shape[0]
    L = ROW_TILE

    @pl.when(pl.program_id(0) == 0)
    def _():
        wt_s[:, :kvw] = wkv_ref[...].T.astype(BF16)
        wt_s[:, kvw:] = wg_ref[...].T.astype(BF16)

    y = jnp.dot(h_ref[...], wt_s[...], preferred_element_type=F32)
    kv_out[...] = y[:, :kvw]
    g = y[:, kvw:]
    lane = lax.broadcasted_iota(jnp.int32, g.shape, 1)
    gates = jnp.where(lane < 4 * MLSTM_HEADS, g, 0.0) + bias_ref[...]
    log_f = jnp.minimum(gates, 0.0) - jnp.log1p(jnp.exp(-jnp.abs(gates)))
    p1 = log_f.astype(BF16)
    r1 = log_f - p1.astype(F32)
    p2 = r1.astype(BF16)
    p3 = (r1 - p2.astype(F32)).astype(BF16)
    parts = jnp.concatenate([p1, p2, p3], axis=1)
    row = lax.broadcasted_iota(jnp.int32, (L, L), 0)
    col = lax.broadcasted_iota(jnp.int32, (L, L), 1)
    lower = (col <= row).astype(BF16)
    chunks = [slice(c * L, (c + 1) * L) for c in range(g.shape[0] // L)]
    sums = [jnp.dot(lower, parts[rs], preferred_element_type=F32) for rs in chunks]
    pre = [s[:, 0:128] + s[:, 128:256] + s[:, 256:384] for s in sums]
    suf = [p[L - 1:L, :] - p + log_f[rs] for p, rs in zip(pre, chunks)]
    g_out[...] = gates
    for c, rs in enumerate(chunks):
        pre_out[rs, :] = pre[c]
        suf_out[rs, :] = suf[c]
        gt_out[:, rs] = gates[rs].T
        pret_out[:, rs] = pre[c].T
        suft_out[:, rs] = suf[c].T


def _inproj_tail(h, w_t, layer, bias):
    m, k = h.shape
    tm = _pick_tile(m, (1024, 512, 256))
    kvw = 2 * ATT_KV_HEADS * ATT_HEAD_DIM
    row_sd, col_sd = jax.ShapeDtypeStruct((m, 128), F32), jax.ShapeDtypeStruct((128, m), F32)
    row_spec = pl.BlockSpec((tm, 128), lambda i: (i, 0))
    col_spec = pl.BlockSpec((128, tm), lambda i: (0, i))
    return pl.pallas_call(
        _inproj_tail_kernel,
        out_shape=(jax.ShapeDtypeStruct((m, kvw), F32), row_sd, col_sd, row_sd, col_sd, row_sd, col_sd),
        grid=(m // tm,),
        in_specs=[pl.BlockSpec((tm, k), lambda i: (i, 0)),
                  pl.BlockSpec((pl.Element(kvw), pl.Element(k)), lambda i: (layer * N_IN + W_KV_ROW, 0)),
                  pl.BlockSpec((pl.Element(128), pl.Element(k)), lambda i: (layer * N_IN + W_MGATE_ROW, 0)),
                  pl.BlockSpec((1, 128), lambda i: (0, 0))],
        out_specs=(pl.BlockSpec((tm, kvw), lambda i: (i, 0)),
                   row_spec, col_spec, row_spec, col_spec, row_spec, col_spec),
        scratch_shapes=[pltpu.VMEM((k, kvw + 128), BF16)],
        compiler_params=_cparams(("arbitrary",)),
        name="inproj_tail",
    )(h, w_t, w_t, bias)


def _qkvprep_kernel(q_ref, t_ref, cos_ref, sin_ref, qg_ref, kg_ref, perm_ref, q_out, k_out, vt_out, *,
                    tiles_per_batch):
    is_lat = pl.program_id(1) < tiles_per_batch - 1
    cos, sin, kg = cos_ref[...], sin_ref[...], kg_ref[...]
    qg = qg_ref[...] * (ATT_SCALE * LOG2_E)
    hd = ATT_HEAD_DIM
    n_q = ATT_KV_HEADS * ATT_GROUP
    sls = [slice(h * hd, (h + 1) * hd) for h in range(n_q)]
    xs = [q_ref[:, sl].astype(F32) for sl in sls] + [t_ref[:, sl] for sl in sls[:ATT_KV_HEADS]]
    gains = [qg] * n_q + [kg] * ATT_KV_HEADS
    xs = [x * lax.rsqrt(jnp.mean(x * x, axis=-1, keepdims=True) + EPS) * g for x, g in zip(xs, gains)]
    x_all = jnp.concatenate(xs, axis=0)
    hi = x_all.astype(BF16)
    lo = (x_all - hi.astype(F32)).astype(BF16)
    partner = jnp.dot(jnp.concatenate([hi, lo], axis=1), perm_ref[...], preferred_element_type=F32)
    rows = xs[0].shape[0]
    outs = [jnp.where(is_lat, x * cos + partner[i * rows:(i + 1) * rows] * sin, x).astype(BF16)
            for i, x in enumerate(xs)]
    for h in range(n_q):
        q_out[:, sls[h]] = outs[h]
    for h in range(ATT_KV_HEADS):
        k_out[:, sls[h]] = outs[n_q + h]
        vt_out[h, 0:hd, :] = t_ref[:, sls[ATT_KV_HEADS + h]].T.astype(BF16)
        vt_out[h, hd:, :] = jnp.ones((VT_ROWS - hd, ROW_TILE), BF16)


def _qkvprep(ya, kv, cos_t, sin_t, q_gain, k_gain, batch, tiles_per_batch):
    m = kv.shape[0]
    lat_tiles = tiles_per_batch - 1
    kvw = ATT_KV_HEADS * ATT_HEAD_DIM
    rope_spec = pl.BlockSpec((ROW_TILE, ATT_HEAD_DIM), lambda b, r: (jnp.minimum(r, lat_tiles - 1), 0))
    gain_spec = pl.BlockSpec((1, ATT_HEAD_DIM), lambda b, r: (0, 0))
    lane = np.arange(ATT_HEAD_DIM)
    swap = (lane[:, None] == (lane[None, :] ^ (ATT_HEAD_DIM // 4))).astype(np.float32)
    return pl.pallas_call(
        functools.partial(_qkvprep_kernel, tiles_per_batch=tiles_per_batch),
        out_shape=(jax.ShapeDtypeStruct((m, BRANCH_W), BF16),
                   jax.ShapeDtypeStruct((m, kvw), BF16),
                   jax.ShapeDtypeStruct((batch, ATT_KV_HEADS, VT_ROWS, tiles_per_batch * ROW_TILE), BF16)),
        grid=(batch, tiles_per_batch),
        in_specs=[pl.BlockSpec((ROW_TILE, BRANCH_W), lambda b, r: (b * tiles_per_batch + r, A_AQ)),
                  pl.BlockSpec((ROW_TILE, 2 * kvw), lambda b, r: (b * tiles_per_batch + r, 0)),
                  rope_spec, rope_spec, gain_spec, gain_spec,
                  pl.BlockSpec((2 * ATT_HEAD_DIM, ATT_HEAD_DIM), lambda b, r: (0, 0))],
        out_specs=(pl.BlockSpec((ROW_TILE, BRANCH_W), lambda b, r: (b * tiles_per_batch + r, 0)),
                   pl.BlockSpec((ROW_TILE, kvw), lambda b, r: (b * tiles_per_batch + r, 0)),
                   pl.BlockSpec((None, ATT_KV_HEADS, VT_ROWS, ROW_TILE), lambda b, r: (b, 0, 0, r))),
        compiler_params=_cparams(("parallel", "parallel")),
        name="qkvprep",
    )(ya, kv, cos_t, sin_t, q_gain.reshape(1, ATT_HEAD_DIM), k_gain.reshape(1, ATT_HEAD_DIM),
      jnp.asarray(np.tile(swap, (2, 1)), BF16))


def _attn_kernel(q_ref, z_ref, k_ref, vt_ref, o_ref, *, n_lat_q, t_lat, kc):
    is_lat = pl.program_id(2) < n_lat_q
    hd = ATT_HEAD_DIM
    q = jnp.concatenate([q_ref[:, g * hd:(g + 1) * hd] for g in range(ATT_GROUP)], axis=0)

    def attend(chunks):
        def scores(c):
            off, n = chunks[c]
            return lax.dot_general(k_ref[off:off + n, :], q, (((1,), (1,)), ((), ())),
                                   preferred_element_type=F32)

        m, o = None, None
        s_next = scores(0)
        for c, (off, n) in enumerate(chunks):
            s = s_next
            if c + 1 < len(chunks):
                s_next = scores(c + 1)
            mc = jnp.max(s, axis=0, keepdims=True)
            m_new = mc if m is None else jnp.maximum(m, mc)
            p = jnp.exp2((s - m_new).astype(BF16))
            oc = jnp.dot(vt_ref[:, off:off + n], p, preferred_element_type=F32)
            o = oc if o is None else jnp.exp2(m - m_new) * o + oc
            m = m_new
        o = o[0:hd] / o[hd:hd + 1]
        for g in range(ATT_GROUP):
            sl = slice(g * ATT_HEAD_DIM, (g + 1) * ATT_HEAD_DIM)
            og = o[:, g * ATT_Q_TILE:(g + 1) * ATT_Q_TILE].T
            o_ref[:, sl] = (og * _silu(z_ref[:, sl].astype(F32))).astype(BF16)

    ctx_chunk = (t_lat, CTX_LEN)

    @pl.when(is_lat)
    def _():
        attend([(off, kc) for off in range(0, t_lat, kc)] + [ctx_chunk])

    @pl.when(jnp.logical_not(is_lat))
    def _():
        attend([ctx_chunk])


def _attention(qb, ya, kb, vt, batch, s_len):
    m = ya.shape[0]
    t_lat = s_len - CTX_LEN
    nq = s_len // ATT_Q_TILE
    n_lat_q = t_lat // ATT_Q_TILE
    gw = ATT_GROUP * ATT_HEAD_DIM
    return pl.pallas_call(
        functools.partial(_attn_kernel, n_lat_q=n_lat_q, t_lat=t_lat,
                          kc=_pick_tile(t_lat, (ATT_KEY_CHUNK, 512, 256))),
        out_shape=jax.ShapeDtypeStruct((m, BRANCH_W), BF16),
        grid=(batch, ATT_KV_HEADS, nq),
        in_specs=[pl.BlockSpec((ATT_Q_TILE, gw), lambda b, h, i: (b * nq + i, h)),
                  pl.BlockSpec((ATT_Q_TILE, gw), lambda b, h, i: (b * nq + i, A_AZ * 2 + h)),
                  pl.BlockSpec((s_len, ATT_HEAD_DIM), lambda b, h, i: (b, h)),
                  pl.BlockSpec((None, None, VT_ROWS, s_len), lambda b, h, i: (b, h, 0, 0))],
        out_specs=pl.BlockSpec((ATT_Q_TILE, gw), lambda b, h, i: (b * nq + i, h)),
        compiler_params=_cparams(("parallel", "parallel", "arbitrary")),
        name="attention",
    )(qb, ya, kb, vt)


def _mlstm_kernel(*refs, reverse, final):
    if final:
        (q_ref, k_ref, v_ref, g_ref, gt_ref, bc_ref, bct_ref, o_ref, z_ref, gain_ref, hprev_ref,
         out_ref, c_s, m_s) = refs
    else:
        q_ref, k_ref, v_ref, g_ref, gt_ref, bc_ref, bct_ref, out_ref, c_s, m_s = refs
    L = ROW_TILE
    hd = MLSTM_HEAD_DIM
    aux = MLSTM_AUX

    @pl.when(pl.program_id(1) == 0)
    def _():
        c_s[...] = jnp.zeros(c_s.shape, F32)
        m_s[...] = jnp.full(m_s.shape, M_INIT, F32)

    def lanes(x):
        return jnp.concatenate([x] * (hd // aux), axis=1)

    gates, gates_t = g_ref[...], gt_ref[...]
    bcum, bcum_t = bc_ref[...], bct_ref[...]
    row = lax.broadcasted_iota(jnp.int32, (L, L), 0)
    col = lax.broadcasted_iota(jnp.int32, (L, L), 1)
    tri = (col >= row) if reverse else (col <= row)
    last = 0 if reverse else L - 1
    heads = range(MLSTM_HEADS)
    hsl = [slice(h * hd, (h + 1) * hd) for h in heads]
    icol = [(2 if reverse else 0) * MLSTM_HEADS + h for h in heads]
    fcol = [(3 if reverse else 1) * MLSTM_HEADS + h for h in heads]
    c_old = [c_s[h] for h in heads]
    m_old = [m_s[h, 0:1, 0:1] for h in heads]
    q = [q_ref[:, hsl[h]] for h in heads]
    k = [k_ref[:, hsl[h]].astype(F32) * MLSTM_K_SCALE for h in heads]
    ones = jnp.ones((L, aux), v_ref.dtype)
    v = [jnp.concatenate([v_ref[:, hsl[h]], ones], axis=1) for h in heads]
    b_col = [bcum[:, fcol[h]:fcol[h] + 1] for h in heads]
    i_col = [gates[:, icol[h]:icol[h] + 1] for h in heads]

    qk = [lax.dot_general(q[h], k[h].astype(BF16), (((1,), (1,)), ((), ())), preferred_element_type=F32)
          for h in heads]
    qc = [jnp.dot(q[h], c_old[h].astype(BF16), preferred_element_type=F32) for h in heads]

    g_rep, sm = [], []
    for h in heads:
        a_row = gates_t[icol[h]:icol[h] + 1, :] - bcum_t[fcol[h]:fcol[h] + 1, :]
        masked = jnp.where(tri, a_row, -jnp.inf)
        g_t = jnp.maximum(m_old[h], jnp.max(masked, axis=1, keepdims=True))
        g_rep.append(jnp.broadcast_to(g_t, (L, aux)))
        sm.append((qk[h] * jnp.exp(masked - lanes(g_rep[h]))).astype(BF16))
    sv = [jnp.dot(sm[h], v[h], preferred_element_type=F32) for h in heads]

    kw, decay, m_new = [], [], []
    for h in heads:
        b_last = bcum[last:last + 1, fcol[h]:fcol[h] + 1]
        log_w = b_last - b_col[h] + i_col[h]
        m_new.append(jnp.maximum(b_last + m_old[h], jnp.max(log_w, axis=0, keepdims=True)))
        decay.append(jnp.exp(b_last + m_old[h] - m_new[h]))
        kw.append((k[h] * jnp.exp(log_w - m_new[h])).astype(BF16))
    kv = [lax.dot_general(kw[h], v[h], (((0,), (0,)), ((), ())), preferred_element_type=F32) for h in heads]

    for h in heads:
        inter = jnp.exp(m_old[h] - g_rep[h])
        floor = jnp.exp(-(jnp.broadcast_to(b_col[h], (L, aux)) + g_rep[h]))
        den = inter * qc[h][:, hd:] + sv[h][:, hd:]
        inv = 1.0 / jnp.maximum(jnp.abs(den), floor)
        hh = (lanes(inter) * qc[h][:, :hd] + sv[h][:, :hd]) * lanes(inv)
        if final:
            ht = hh + hprev_ref[:, hsl[h]]
            hn = ht * lax.rsqrt(jnp.mean(ht * ht, axis=-1, keepdims=True) + EPS)
            y = hn * gain_ref[:, hsl[h]] * jax.nn.sigmoid(o_ref[:, hsl[h]].astype(F32)) \
                * _silu(z_ref[:, hsl[h]].astype(F32))
            out_ref[:, hsl[h]] = y.astype(BF16)
        else:
            out_ref[:, hsl[h]] = hh

    for h in heads:
        c_s[h] = decay[h] * c_old[h] + kv[h]
        m_s[h] = jnp.broadcast_to(m_new[h], m_s.shape[1:])


def _mlstm_pass(ya, gate_arrays, gain, hprev, batch, tiles_per_batch, reverse):
    m = ya.shape[0]
    lat = tiles_per_batch - 1
    final = hprev is not None

    def rows(b, c):
        r = jnp.where(c == 0, lat, (lat - c) if reverse else (c - 1))
        return b * tiles_per_batch + r

    def col(cidx):
        return pl.BlockSpec((ROW_TILE, BRANCH_W), lambda b, c: (rows(b, c), cidx))

    gate_spec = pl.BlockSpec((ROW_TILE, 128), lambda b, c: (rows(b, c), 0))
    gate_t_spec = pl.BlockSpec((128, ROW_TILE), lambda b, c: (0, rows(b, c)))
    in_specs = [col(A_MQ), col(A_MK), col(A_MV), gate_spec, gate_t_spec, gate_spec, gate_t_spec]
    args = [ya, ya, ya, *gate_arrays]
    if final:
        in_specs += [col(A_MO), col(A_MZ), pl.BlockSpec((1, BRANCH_W), lambda b, c: (0, 0)),
                     pl.BlockSpec((ROW_TILE, BRANCH_W), lambda b, c: (rows(b, c), 0))]
        args += [ya, ya, gain.reshape(1, BRANCH_W), hprev]
    return pl.pallas_call(
        functools.partial(_mlstm_kernel, reverse=reverse, final=final),
        out_shape=jax.ShapeDtypeStruct((m, BRANCH_W), BF16 if final else F32),
        grid=(batch, tiles_per_batch),
        in_specs=in_specs,
        out_specs=pl.BlockSpec((ROW_TILE, BRANCH_W), lambda b, c: (rows(b, c), 0)),
        scratch_shapes=[pltpu.VMEM((MLSTM_HEADS, MLSTM_HEAD_DIM, MLSTM_HEAD_DIM + MLSTM_AUX), F32),
                        pltpu.VMEM((MLSTM_HEADS, 8, 128), F32)],
        compiler_params=_cparams(("parallel", "arbitrary")),
        name="mlstm_bwd" if reverse else "mlstm_fwd",
    )(*args)


def _segment_info(r, lat):
    has_prev = jnp.logical_and(r != 0, r != lat)
    has_next = jnp.logical_and(r != lat - 1, r != lat)
    seg_len = jnp.where(r == lat, CTX_LEN, lat * ROW_TILE)
    t0 = jnp.where(r == lat, 0, r * ROW_TILE)
    return has_prev, has_next, seg_len, t0


def _conv_strip(j, r, lat, cu_ref, cb_ref, cc_ref, cz_ref, cu_p, cc_p, cu_n, cc_n, cw_ref):
    has_prev, has_next, _, _ = _segment_info(r, lat)
    rowi = lax.broadcasted_iota(jnp.int32, (ROW_TILE, 1), 0)
    cs = slice(j * 128, (j + 1) * 128)
    a = cc_ref[:, cs].astype(F32) * cu_ref[:, cs].astype(F32)
    a_prev = jnp.where(has_prev, cc_p[HALO - 1:HALO, cs].astype(F32) * cu_p[HALO - 1:HALO, cs].astype(F32), 0.0)
    a_next = jnp.where(has_next, cc_n[0:1, cs].astype(F32) * cu_n[0:1, cs].astype(F32), 0.0)
    a_m1 = jnp.where(rowi == 0, a_prev, pltpu.roll(a, 1, 0))
    a_p1 = jnp.where(rowi == ROW_TILE - 1, a_next, pltpu.roll(a, ROW_TILE - 1, 0))
    y = cw_ref[0:1, cs] * a_m1 + cw_ref[1:2, cs] * a + cw_ref[2:3, cs] * a_p1
    return (cb_ref[:, cs].astype(F32) * y * _silu(cz_ref[:, cs].astype(F32))).astype(BF16)


def _pool_branch(r, lat, pu_ref, pz_ref, pu_p, pu_n, pw_ref, ps_ref, band_ref, yd_ref):
    has_prev, has_next, seg_len, t0 = _segment_info(r, lat)
    rowi = lax.broadcasted_iota(jnp.int32, (ROW_TILE, 1), 0)
    u = pu_ref[...]
    halo_zero = jnp.zeros((HALO, BRANCH_W), u.dtype)
    ext = jnp.concatenate([u, jnp.where(has_prev, pu_p[...], halo_zero), jnp.where(has_next, pu_n[...], halo_zero),
                           jnp.zeros((POOL_K - ROW_TILE - 2 * HALO, BRANCH_W), u.dtype)], axis=0)
    t = t0 + rowi
    groups = range(len(POOL_WINDOWS))
    gsl = [slice(g * POOL_GROUP, (g + 1) * POOL_GROUP) for g in groups]
    acc = [jnp.dot(band_ref[g], ext[:, gsl[g]], preferred_element_type=F32) for g in groups]
    dev = []
    for g, w in enumerate(POOL_WINDOWS):
        inv_cnt = 1.0 / (jnp.minimum(t + (w - w // 2), seg_len) - jnp.maximum(t - w // 2, 0)).astype(F32)
        dev.append((acc[g] * inv_cnt - u[:, gsl[g]].astype(F32)).astype(BF16))
    pg = [jnp.dot(dev[g], pw_ref[g], preferred_element_type=F32) for g in groups]
    for g in groups:
        gate = _silu(pz_ref[:, gsl[g]].astype(F32)) * ps_ref[:, gsl[g]]
        yd_ref[:, gsl[g]] = (pg[g] * gate).astype(BF16)


def _pool_band():
    t = np.arange(ROW_TILE)[:, None]
    pos = np.concatenate([np.arange(ROW_TILE), np.arange(-HALO, 0), np.arange(ROW_TILE, ROW_TILE + HALO),
                          np.full(POOL_K - ROW_TILE - 2 * HALO, -10 * ROW_TILE)])[None, :]
    return np.stack([(pos >= t - w // 2) & (pos < t + w - w // 2) for w in POOL_WINDOWS]).astype(np.float32)


def _mergeout_kernel(*refs, final, tiles_per_batch):
    (b0, b1, cu_ref, cb_ref, cc_ref, cz_ref, pu_ref, pz_ref, cu_p, cc_p, pu_p, cu_n, cc_n, pu_n,
     cw_ref, pw_ref, ps_ref, band_ref, g_ref, wb_ref, wo_ref, x_ref, gt_ref) = refs[:23]
    if final:
        fg_ref, o_ref, yd_s = refs[23:]
    else:
        ng_ref, nsc_ref, nsh_ref, o_ref, h_ref, yd_s = refs[23:]
    r, lat = pl.program_id(1), tiles_per_batch - 1

    def gated(j, br):
        gate = jax.nn.sigmoid(g_ref[:, j * D_MODEL:(j + 1) * D_MODEL].astype(F32))
        return gate * jnp.dot(br, wb_ref[j], preferred_element_type=F32)

    acc = gated(0, b0[...]) + gated(1, b1[...])
    strips = [_conv_strip(j, r, lat, cu_ref, cb_ref, cc_ref, cz_ref, cu_p, cc_p, cu_n, cc_n, cw_ref)
              for j in range(BRANCH_W // 128)]
    _pool_branch(r, lat, pu_ref, pz_ref, pu_p, pu_n, pw_ref, ps_ref, band_ref, yd_s)
    acc = acc + gated(2, jnp.concatenate(strips, axis=1)) + gated(3, yd_s[...])
    y = x_ref[...] + gt_ref[0] * jnp.dot(acc.astype(BF16), wo_ref[...], preferred_element_type=F32)
    yn = y * lax.rsqrt(jnp.mean(y * y, axis=-1, keepdims=True) + EPS)
    if final:
        o_ref[...] = yn * fg_ref[...]
    else:
        o_ref[...] = y
        h_ref[...] = (yn * ng_ref[...] * (1.0 + nsc_ref[0]) + nsh_ref[0]).astype(BF16)


def _mergeout(y_att, y_mls, yb, conv_w, pool_w, pool_scale, w_branch, w_out, layer, xs, gt_t, final_gain,
              next_norm, batch, tiles_per_batch):
    m, d = xs.shape
    final = final_gain is not None
    lat = tiles_per_batch - 1
    n_r = lat if final else tiles_per_batch
    resident = pl.Buffered(1)
    hb = ROW_TILE // HALO
    n_halo = m // HALO
    n_win = len(POOL_WINDOWS)

    def src(b, r):
        return b * tiles_per_batch + r

    def col(cidx):
        return pl.BlockSpec((ROW_TILE, BRANCH_W), lambda b, r: (src(b, r), cidx))

    def prev(cidx):
        return pl.BlockSpec((HALO, BRANCH_W), lambda b, r: (jnp.maximum(src(b, r) * hb - 1, 0), cidx))

    def nxt(cidx):
        return pl.BlockSpec((HALO, BRANCH_W), lambda b, r: (jnp.minimum((src(b, r) + 1) * hb, n_halo - 1), cidx))

    in_specs = [col(0), col(0),
                col(B_CU), col(B_CB), col(B_CC), col(B_CZ), col(B_PU), col(B_PZ),
                prev(B_CU), prev(B_CC), prev(B_PU), nxt(B_CU), nxt(B_CC), nxt(B_PU),
                pl.BlockSpec((None, 3, BRANCH_W), lambda b, r: (layer, 0, 0)),
                pl.BlockSpec((None, n_win, POOL_GROUP, POOL_GROUP), lambda b, r: (layer, 0, 0, 0)),
                pl.BlockSpec((None, 1, BRANCH_W), lambda b, r: (layer, 0, 0)),
                pl.BlockSpec((n_win, ROW_TILE, POOL_K), lambda b, r: (0, 0, 0)),
                pl.BlockSpec((ROW_TILE, N_BRANCH * d), lambda b, r: (src(b, r), B_GATE)),
                pl.BlockSpec((None, N_BRANCH, BRANCH_W, d), lambda b, r: (layer, 0, 0, 0), pipeline_mode=resident),
                pl.BlockSpec((None, d, d), lambda b, r: (layer, 0, 0), pipeline_mode=resident),
                pl.BlockSpec((ROW_TILE, d), lambda b, r: (src(b, r), 0)),
                pl.BlockSpec((1, 1, d), lambda b, r: (src(b, r), 0, 0))]
    args = [y_att, y_mls, *([yb] * 12), conv_w, pool_w, pool_scale.reshape(-1, 1, BRANCH_W),
            jnp.asarray(_pool_band(), BF16), yb, w_branch, w_out, xs, gt_t]
    row_spec = pl.BlockSpec((ROW_TILE, d), lambda b, r: (b * n_r + r, 0))
    out_shape = jax.ShapeDtypeStruct((batch * n_r * ROW_TILE, d), F32)
    if final:
        in_specs.append(pl.BlockSpec((1, d), lambda b, r: (0, 0)))
        args.append(final_gain.reshape(1, d))
        out_specs = row_spec
    else:
        tab_spec = pl.BlockSpec((1, 1, d), lambda b, r: (src(b, r), 0, 0))
        in_specs += [pl.BlockSpec((1, d), lambda b, r: (0, 0)), tab_spec, tab_spec]
        args += [next_norm[0].reshape(1, d), next_norm[1], next_norm[2]]
        out_shape = (out_shape, jax.ShapeDtypeStruct((m, d), BF16))
        out_specs = (row_spec, row_spec)
    return pl.pallas_call(
        functools.partial(_mergeout_kernel, final=final, tiles_per_batch=tiles_per_batch),
        out_shape=out_shape,
        grid=(batch, n_r),
        in_specs=in_specs,
        out_specs=out_specs,
        scratch_shapes=[pltpu.VMEM((ROW_TILE, BRANCH_W), BF16)],
        compiler_params=_cparams(("parallel", "parallel")),
        name="mergeout_final" if final else "mergeout",
    )(*args)


def _rope_tables(t_lat):
    pos = np.arange(t_lat)
    quarter = ATT_HEAD_DIM // 4
    freq = ROPE_THETA ** (-jnp.arange(quarter, dtype=F32) / quarter)
    a_row = jnp.asarray(pos // GRID_W, F32)[:, None] * freq[None, :]
    a_col = jnp.asarray(pos % GRID_W, F32)[:, None] * freq[None, :]
    cos_t = jnp.concatenate([jnp.cos(a_row), jnp.cos(a_row), jnp.cos(a_col), jnp.cos(a_col)], axis=-1)
    sin_t = jnp.concatenate([-jnp.sin(a_row), jnp.sin(a_row), -jnp.sin(a_col), jnp.sin(a_col)], axis=-1)
    return cos_t, sin_t


def kernel(x, c, ctx, c_ctx, norm_gain, w_mod, b_mod, w_in, q_norm_gain, k_norm_gain, mlstm_gate_bias,
           mlstm_norm_gain, conv_w, pool_w, pool_scale, w_branch, w_out, final_norm_gain):
    batch, t_lat, d = x.shape
    depth = w_in.shape[0]
    assert d == D_MODEL and ctx.shape[1] == CTX_LEN and t_lat % ROW_TILE == 0 and batch < 8
    s_len = t_lat + CTX_LEN
    tiles_per_batch = s_len // ROW_TILE

    cc = jnp.zeros((8, d), F32).at[:batch].set(c).at[batch].set(c_ctx)
    mod = _modulation(cc, w_mod, b_mod)
    tile_row = np.array([b if r < tiles_per_batch - 1 else batch
                         for b in range(batch) for r in range(tiles_per_batch)], np.int32)
    cos_t, sin_t = _rope_tables(t_lat)
    assert w_in.shape[1:] == (d, N_IN)
    w_t = jnp.swapaxes(w_in, 1, 2).reshape(depth * N_IN, d)
    w_branch_b, w_out_b, pool_w_b = w_branch.astype(BF16), w_out.astype(BF16), pool_w.astype(BF16)

    tables = []
    for l in range(depth):
        mod_t = mod[l][tile_row][:, None, :]
        tables.append((mod_t[..., :d], mod_t[..., d:2 * d], mod_t[..., 2 * d:]))

    out = None
    h, xs = _normmod_first(x, ctx, norm_gain[0], tables[0][1], tables[0][0])
    for l in range(depth):
        gt_t = tables[l][2]
        bias = jnp.zeros((1, 128), F32).at[0, :4 * MLSTM_HEADS].set(mlstm_gate_bias[l].reshape(-1))
        ya = _inproj(h, w_t, l, 7, lambda j: jnp.where(j == 0, 0, 512 + IN_UNIT * j), "inproj_a")
        yb = _inproj(h, w_t, l, 14, lambda j: jnp.where(j < 8, W_MERGE_ROW + IN_UNIT * j,
                                                       W_LOCAL_ROW + IN_UNIT * (j - 8)), "inproj_b")
        kv, g, g_t, pre, pre_t, suf, suf_t = _inproj_tail(h, w_t, l, bias)

        qb, kb, vt = _qkvprep(ya, kv, cos_t, sin_t, q_norm_gain[l], k_norm_gain[l], batch, tiles_per_batch)
        y_att = _attention(qb, ya, kb, vt, batch, s_len)
        h_f = _mlstm_pass(ya, (g, g_t, pre, pre_t), None, None, batch, tiles_per_batch, reverse=False)
        y_mls = _mlstm_pass(ya, (g, g_t, suf, suf_t), mlstm_norm_gain[l], h_f, batch, tiles_per_batch,
                            reverse=True)
        if l == depth - 1:
            out = _mergeout(y_att, y_mls, yb, conv_w, pool_w_b, pool_scale, w_branch_b, w_out_b, l, xs, gt_t,
                            final_norm_gain, None, batch, tiles_per_batch).reshape(batch, t_lat, d)
        else:
            next_norm = (norm_gain[l + 1], tables[l + 1][1], tables[l + 1][0])
            xs, h = _mergeout(y_att, y_mls, yb, conv_w, pool_w_b, pool_scale, w_branch_b, w_out_b, l, xs, gt_t,
                              None, next_norm, batch, tiles_per_batch)
    return out
```

```python
import functools

import numpy as np
import jax
import jax.numpy as jnp
from jax import lax
from jax.experimental import pallas as pl
from jax.experimental.pallas import tpu as pltpu

F32 = jnp.float32
BF16 = jnp.bfloat16

D_MODEL = 2048
BRANCH_W = 1024
GRID_W = 64
CTX_LEN = 256
EPS = 1e-6
ATT_HEAD_DIM = 128
ATT_GROUP = 4
ATT_KV_HEADS = 2
ATT_SCALE = ATT_HEAD_DIM ** -0.5
LOG2_E = 1.4426950408889634
ROPE_THETA = 10000.0
MLSTM_HEADS = 4
MLSTM_HEAD_DIM = 256
MLSTM_K_SCALE = MLSTM_HEAD_DIM ** -0.5
MLSTM_AUX = 128
M_INIT = -1e30
POOL_WINDOWS = (2, 4, 8, 16)
POOL_GROUP = 256
N_BRANCH = 4

ROW_TILE = 256
HALO = 16
POOL_K = 384
ATT_Q_TILE = 256
ATT_KEY_CHUNK = 256
VT_ROWS = ATT_HEAD_DIM + 16
V7X_VMEM_LIMIT = 56 * 1024 * 1024
V7X_LANES = 128
V7X_SUBLANES = 8
GATE_LANES = V7X_LANES

A_AQ, A_AZ, A_MQ, A_MK, A_MV, A_MO, A_MZ = range(7)
B_GATE = 0
B_CU, B_CB, B_CC, B_CZ, B_PU, B_PZ = range(8, 14)


def _cparams(sem, vmem=V7X_VMEM_LIMIT):
    return pltpu.CompilerParams(dimension_semantics=sem, vmem_limit_bytes=vmem)


def _silu(x):
    return x * jax.nn.sigmoid(x)


def _mod_kernel(c_ref, w_ref, b_ref, o_ref):
    a = _silu(c_ref[...]).astype(BF16)
    o_ref[...] = jnp.dot(a, w_ref[...].astype(BF16), preferred_element_type=F32) + b_ref[...]


def _modulation(cc, w_mod, b_mod):
    depth, d, n = w_mod.shape
    tn = 768
    return pl.pallas_call(
        _mod_kernel,
        out_shape=jax.ShapeDtypeStruct((depth, 8, n), F32),
        grid=(depth, n // tn),
        in_specs=[pl.BlockSpec((8, d), lambda l, j: (0, 0)),
                  pl.BlockSpec((None, d, tn), lambda l, j: (l, 0, j)),
                  pl.BlockSpec((None, 1, tn), lambda l, j: (l, 0, j))],
        out_specs=pl.BlockSpec((None, 8, tn), lambda l, j: (l, 0, j)),
        compiler_params=_cparams(("parallel", "parallel")),
        name="modulation",
    )(cc, w_mod, b_mod.reshape(depth, 1, n))


def _normmod_first_kernel(x_ref, c_ref, g_ref, sc_ref, sh_ref, o_ref, xs_ref, *, tiles_per_batch):
    x = jnp.where(pl.program_id(1) == tiles_per_batch - 1, c_ref[...], x_ref[...])
    xs_ref[...] = x
    y = x * lax.rsqrt(jnp.mean(x * x, axis=-1, keepdims=True) + EPS) * g_ref[...]
    o_ref[...] = (y * (1.0 + sc_ref[0]) + sh_ref[0]).astype(BF16)


def _normmod_first(x, ctx, gain, sc_t, sh_t):
    batch, t_lat, d = x.shape
    tpb = t_lat // ROW_TILE + 1
    m = batch * tpb * ROW_TILE
    tab_spec = pl.BlockSpec((1, 1, d), lambda b, r: (b * tpb + r, 0, 0))
    row_spec = pl.BlockSpec((ROW_TILE, d), lambda b, r: (b * tpb + r, 0))
    return pl.pallas_call(
        functools.partial(_normmod_first_kernel, tiles_per_batch=tpb),
        out_shape=(jax.ShapeDtypeStruct((m, d), BF16), jax.ShapeDtypeStruct((m, d), F32)),
        grid=(batch, tpb),
        in_specs=[pl.BlockSpec((None, ROW_TILE, d), lambda b, r: (b, jnp.minimum(r, tpb - 2), 0)),
                  pl.BlockSpec((None, CTX_LEN, d), lambda b, r: (b, 0, 0)),
                  pl.BlockSpec((1, d), lambda b, r: (0, 0)),
                  tab_spec, tab_spec],
        out_specs=(row_spec, row_spec),
        compiler_params=_cparams(("parallel", "parallel")),
        name="normmod_first",
    )(x, ctx, gain.reshape(1, d), sc_t, sh_t)


IN_UNIT = 1024
W_KV_ROW, W_MGATE_ROW, W_LOCAL_ROW, W_MERGE_ROW = 1024, 7680, 7696, 13840
N_IN = W_MERGE_ROW + N_BRANCH * D_MODEL


def _pick_tile(n, candidates):
    for t in candidates:
        if n % t == 0:
            return t
    raise ValueError(f"no tile for {n}")


def _inproj_kernel(h_ref, w_ref, o_ref, wt_s):
    @pl.when(pl.program_id(1) == 0)
    def _():
        wt_s[...] = w_ref[...].T.astype(BF16)

    o_ref[...] = jnp.dot(h_ref[...], wt_s[...], preferred_element_type=F32).astype(o_ref.dtype)


def _inproj(h, w_t, layer, n_units, unit_row, name):
    m, k = h.shape
    tm = _pick_tile(m, (1024, 512, 256))
    base = layer * N_IN
    return pl.pallas_call(
        _inproj_kernel,
        out_shape=jax.ShapeDtypeStruct((m, n_units * IN_UNIT), BF16),
        grid=(n_units, m // tm),
        in_specs=[pl.BlockSpec((tm, k), lambda j, i: (i, 0)),
                  pl.BlockSpec((pl.Element(IN_UNIT), pl.Element(k)),
                               lambda j, i: (pl.multiple_of(base + unit_row(j), 8), 0))],
        out_specs=pl.BlockSpec((tm, IN_UNIT), lambda j, i: (i, j)),
        scratch_shapes=[pltpu.VMEM((k, IN_UNIT), BF16)],
        compiler_params=_cparams(("parallel", "arbitrary")),
        name=name,
    )(h, w_t)


def _inproj_tail_kernel(h_ref, wkv_ref, wg_ref, bias_ref,
                        kv_out, g_out, gt_out, pre_out, pret_out, suf_out, suft_out, wt_s):
    kvw = wkv_ref.shape[0]
    L = ROW_TILE

    @pl.when(pl.program_id(0) == 0)
    def _():
        wt_s[:, :kvw] = wkv_ref[...].T.astype(BF16)
        wt_s[:, kvw:] = wg_ref[...].T.astype(BF16)

    y = jnp.dot(h_ref[...], wt_s[...], preferred_element_type=F32)
    kv_out[...] = y[:, :kvw]
    g = y[:, kvw:]
    lane = lax.broadcasted_iota(jnp.int32, g.shape, 1)
    gates = jnp.where(lane < 4 * MLSTM_HEADS, g, 0.0) + bias_ref[...]
    log_f = jnp.minimum(gates, 0.0) - jnp.log1p(jnp.exp(-jnp.abs(gates)))
    p1 = log_f.astype(BF16)
    r1 = log_f - p1.astype(F32)
    p2 = r1.astype(BF16)
    p3 = (r1 - p2.astype(F32)).astype(BF16)
    parts = jnp.concatenate([p1, p2, p3], axis=1)
    row = lax.broadcasted_iota(jnp.int32, (L, L), 0)
    col = lax.broadcasted_iota(jnp.int32, (L, L), 1)
    lower = (col <= row).astype(BF16)
    chunks = [slice(c * L, (c + 1) * L) for c in range(g.shape[0] // L)]
    sums = [jnp.dot(lower, parts[rs], preferred_element_type=F32) for rs in chunks]
    gl = GATE_LANES
    pre = [s[:, 0:gl] + s[:, gl:2 * gl] + s[:, 2 * gl:3 * gl] for s in sums]
    suf = [p[L - 1:L, :] - p + log_f[rs] for p, rs in zip(pre, chunks)]
    g_out[...] = gates
    for c, rs in enumerate(chunks):
        pre_out[rs, :] = pre[c]
        suf_out[rs, :] = suf[c]
        gt_out[:, rs] = gates[rs].T
        pret_out[:, rs] = pre[c].T
        suft_out[:, rs] = suf[c].T


def _inproj_tail(h, w_t, layer, bias):
    m, k = h.shape
    tm = _pick_tile(m, (1024, 512, 256))
    kvw = 2 * ATT_KV_HEADS * ATT_HEAD_DIM
    gl = GATE_LANES
    row_sd, col_sd = jax.ShapeDtypeStruct((m, gl), F32), jax.ShapeDtypeStruct((gl, m), F32)
    row_spec = pl.BlockSpec((tm, gl), lambda i: (i, 0))
    col_spec = pl.BlockSpec((gl, tm), lambda i: (0, i))
    return pl.pallas_call(
        _inproj_tail_kernel,
        out_shape=(jax.ShapeDtypeStruct((m, kvw), F32), row_sd, col_sd, row_sd, col_sd, row_sd, col_sd),
        grid=(m // tm,),
        in_specs=[pl.BlockSpec((tm, k), lambda i: (i, 0)),
                  pl.BlockSpec((pl.Element(kvw), pl.Element(k)), lambda i: (layer * N_IN + W_KV_ROW, 0)),
                  pl.BlockSpec((pl.Element(gl), pl.Element(k)), lambda i: (layer * N_IN + W_MGATE_ROW, 0)),
                  pl.BlockSpec((1, gl), lambda i: (0, 0))],
        out_specs=(pl.BlockSpec((tm, kvw), lambda i: (i, 0)),
                   row_spec, col_spec, row_spec, col_spec, row_spec, col_spec),
        scratch_shapes=[pltpu.VMEM((k, kvw + gl), BF16)],
        compiler_params=_cparams(("arbitrary",)),
        name="inproj_tail",
    )(h, w_t, w_t, bias)


def _qkvprep_kernel(q_ref, t_ref, cos_ref, sin_ref, qg_ref, kg_ref, perm_ref, q_out, k_out, vt_out, *,
                    tiles_per_batch):
    is_lat = pl.program_id(1) < tiles_per_batch - 1
    cos, sin, kg = cos_ref[...], sin_ref[...], kg_ref[...]
    qg = qg_ref[...] * (ATT_SCALE * LOG2_E)
    hd = ATT_HEAD_DIM
    n_q = ATT_KV_HEADS * ATT_GROUP
    sls = [slice(h * hd, (h + 1) * hd) for h in range(n_q)]
    xs = [q_ref[:, sl].astype(F32) for sl in sls] + [t_ref[:, sl] for sl in sls[:ATT_KV_HEADS]]
    gains = [qg] * n_q + [kg] * ATT_KV_HEADS
    xs = [x * lax.rsqrt(jnp.mean(x * x, axis=-1, keepdims=True) + EPS) * g for x, g in zip(xs, gains)]
    x_all = jnp.concatenate(xs, axis=0)
    hi = x_all.astype(BF16)
    lo = (x_all - hi.astype(F32)).astype(BF16)
    partner = jnp.dot(jnp.concatenate([hi, lo], axis=1), perm_ref[...], preferred_element_type=F32)
    rows = xs[0].shape[0]
    outs = [jnp.where(is_lat, x * cos + partner[i * rows:(i + 1) * rows] * sin, x).astype(BF16)
            for i, x in enumerate(xs)]
    for h in range(n_q):
        q_out[:, sls[h]] = outs[h]
    for h in range(ATT_KV_HEADS):
        k_out[:, sls[h]] = outs[n_q + h]
        vt_out[h, 0:hd, :] = t_ref[:, sls[ATT_KV_HEADS + h]].T.astype(BF16)
        vt_out[h, hd:, :] = jnp.ones((VT_ROWS - hd, ROW_TILE), BF16)


def _qkvprep(ya, kv, cos_t, sin_t, q_gain, k_gain, batch, tiles_per_batch):
    m = kv.shape[0]
    lat_tiles = tiles_per_batch - 1
    kvw = ATT_KV_HEADS * ATT_HEAD_DIM
    rope_spec = pl.BlockSpec((ROW_TILE, ATT_HEAD_DIM), lambda b, r: (jnp.minimum(r, lat_tiles - 1), 0))
    gain_spec = pl.BlockSpec((1, ATT_HEAD_DIM), lambda b, r: (0, 0))
    lane = np.arange(ATT_HEAD_DIM)
    swap = (lane[:, None] == (lane[None, :] ^ (ATT_HEAD_DIM // 4))).astype(np.float32)
    return pl.pallas_call(
        functools.partial(_qkvprep_kernel, tiles_per_batch=tiles_per_batch),
        out_shape=(jax.ShapeDtypeStruct((m, BRANCH_W), BF16),
                   jax.ShapeDtypeStruct((m, kvw), BF16),
                   jax.ShapeDtypeStruct((batch, ATT_KV_HEADS, VT_ROWS, tiles_per_batch * ROW_TILE), BF16)),
        grid=(batch, tiles_per_batch),
        in_specs=[pl.BlockSpec((ROW_TILE, BRANCH_W), lambda b, r: (b * tiles_per_batch + r, A_AQ)),
                  pl.BlockSpec((ROW_TILE, 2 * kvw), lambda b, r: (b * tiles_per_batch + r, 0)),
                  rope_spec, rope_spec, gain_spec, gain_spec,
                  pl.BlockSpec((2 * ATT_HEAD_DIM, ATT_HEAD_DIM), lambda b, r: (0, 0))],
        out_specs=(pl.BlockSpec((ROW_TILE, BRANCH_W), lambda b, r: (b * tiles_per_batch + r, 0)),
                   pl.BlockSpec((ROW_TILE, kvw), lambda b, r: (b * tiles_per_batch + r, 0)),
                   pl.BlockSpec((None, ATT_KV_HEADS, VT_ROWS, ROW_TILE), lambda b, r: (b, 0, 0, r))),
        compiler_params=_cparams(("parallel", "parallel")),
        name="qkvprep",
    )(ya, kv, cos_t, sin_t, q_gain.reshape(1, ATT_HEAD_DIM), k_gain.reshape(1, ATT_HEAD_DIM),
      jnp.asarray(np.tile(swap, (2, 1)), BF16))


def _attn_kernel(q_ref, z_ref, k_ref, vt_ref, o_ref, *, n_lat_q, t_lat, kc):
    is_lat = pl.program_id(2) < n_lat_q
    hd = ATT_HEAD_DIM
    q = jnp.concatenate([q_ref[:, g * hd:(g + 1) * hd] for g in range(ATT_GROUP)], axis=0)

    def attend(chunks):
        def scores(c):
            off, n = chunks[c]
            return lax.dot_general(k_ref[off:off + n, :], q, (((1,), (1,)), ((), ())),
                                   preferred_element_type=F32)

        m, o = None, None
        s_next = scores(0)
        for c, (off, n) in enumerate(chunks):
            s = s_next
            if c + 1 < len(chunks):
                s_next = scores(c + 1)
            mc = jnp.max(s, axis=0, keepdims=True)
            m_new = mc if m is None else jnp.maximum(m, mc)
            p = jnp.exp2((s - m_new).astype(BF16))
            oc = jnp.dot(vt_ref[:, off:off + n], p, preferred_element_type=F32)
            o = oc if o is None else jnp.exp2(m - m_new) * o + oc
            m = m_new
        o = o[0:hd] / o[hd:hd + 1]
        for g in range(ATT_GROUP):
            sl = slice(g * ATT_HEAD_DIM, (g + 1) * ATT_HEAD_DIM)
            og = o[:, g * ATT_Q_TILE:(g + 1) * ATT_Q_TILE].T
            o_ref[:, sl] = (og * _silu(z_ref[:, sl].astype(F32))).astype(BF16)

    ctx_chunk = (t_lat, CTX_LEN)

    @pl.when(is_lat)
    def _():
        attend([(off, kc) for off in range(0, t_lat, kc)] + [ctx_chunk])

    @pl.when(jnp.logical_not(is_lat))
    def _():
        attend([ctx_chunk])


def _attention(qb, ya, kb, vt, batch, s_len):
    m = ya.shape[0]
    t_lat = s_len - CTX_LEN
    nq = s_len // ATT_Q_TILE
    n_lat_q = t_lat // ATT_Q_TILE
    gw = ATT_GROUP * ATT_HEAD_DIM
    return pl.pallas_call(
        functools.partial(_attn_kernel, n_lat_q=n_lat_q, t_lat=t_lat,
                          kc=_pick_tile(t_lat, (ATT_KEY_CHUNK, 512, 256))),
        out_shape=jax.ShapeDtypeStruct((m, BRANCH_W), BF16),
        grid=(batch, ATT_KV_HEADS, nq),
        in_specs=[pl.BlockSpec((ATT_Q_TILE, gw), lambda b, h, i: (b * nq + i, h)),
                  pl.BlockSpec((ATT_Q_TILE, gw), lambda b, h, i: (b * nq + i, A_AZ * 2 + h)),
                  pl.BlockSpec((s_len, ATT_HEAD_DIM), lambda b, h, i: (b, h)),
                  pl.BlockSpec((None, None, VT_ROWS, s_len), lambda b, h, i: (b, h, 0, 0))],
        out_specs=pl.BlockSpec((ATT_Q_TILE, gw), lambda b, h, i: (b * nq + i, h)),
        compiler_params=_cparams(("parallel", "parallel", "arbitrary")),
        name="attention",
    )(qb, ya, kb, vt)


def _mlstm_kernel(*refs, reverse, final):
    if final:
        (q_ref, k_ref, v_ref, g_ref, gt_ref, bc_ref, bct_ref, o_ref, z_ref, gain_ref, hprev_ref,
         out_ref, c_s, m_s) = refs
    else:
        q_ref, k_ref, v_ref, g_ref, gt_ref, bc_ref, bct_ref, out_ref, c_s, m_s = refs
    L = ROW_TILE
    hd = MLSTM_HEAD_DIM
    aux = MLSTM_AUX

    @pl.when(pl.program_id(1) == 0)
    def _():
        c_s[...] = jnp.zeros(c_s.shape, F32)
        m_s[...] = jnp.full(m_s.shape, M_INIT, F32)

    def lanes(x):
        return jnp.concatenate([x] * (hd // aux), axis=1)

    gates, gates_t = g_ref[...], gt_ref[...]
    bcum, bcum_t = bc_ref[...], bct_ref[...]
    row = lax.broadcasted_iota(jnp.int32, (L, L), 0)
    col = lax.broadcasted_iota(jnp.int32, (L, L), 1)
    tri = (col >= row) if reverse else (col <= row)
    last = 0 if reverse else L - 1
    heads = range(MLSTM_HEADS)
    hsl = [slice(h * hd, (h + 1) * hd) for h in heads]
    icol = [(2 if reverse else 0) * MLSTM_HEADS + h for h in heads]
    fcol = [(3 if reverse else 1) * MLSTM_HEADS + h for h in heads]
    c_old = [c_s[h] for h in heads]
    m_old = [m_s[h, 0:1, 0:1] for h in heads]
    q = [q_ref[:, hsl[h]] for h in heads]
    k = [k_ref[:, hsl[h]].astype(F32) * MLSTM_K_SCALE for h in heads]
    ones = jnp.ones((L, aux), v_ref.dtype)
    v = [jnp.concatenate([v_ref[:, hsl[h]], ones], axis=1) for h in heads]
    b_col = [bcum[:, fcol[h]:fcol[h] + 1] for h in heads]
    i_col = [gates[:, icol[h]:icol[h] + 1] for h in heads]

    qk = [lax.dot_general(q[h], k[h].astype(BF16), (((1,), (1,)), ((), ())), preferred_element_type=F32)
          for h in heads]
    qc = [jnp.dot(q[h], c_old[h].astype(BF16), preferred_element_type=F32) for h in heads]

    g_rep, sm = [], []
    for h in heads:
        a_row = gates_t[icol[h]:icol[h] + 1, :] - bcum_t[fcol[h]:fcol[h] + 1, :]
        masked = jnp.where(tri, a_row, -jnp.inf)
        g_t = jnp.maximum(m_old[h], jnp.max(masked, axis=1, keepdims=True))
        g_rep.append(jnp.broadcast_to(g_t, (L, aux)))
        sm.append((qk[h] * jnp.exp(masked - lanes(g_rep[h]))).astype(BF16))
    sv = [jnp.dot(sm[h], v[h], preferred_element_type=F32) for h in heads]

    kw, decay, m_new = [], [], []
    for h in heads:
        b_last = bcum[last:last + 1, fcol[h]:fcol[h] + 1]
        log_w = b_last - b_col[h] + i_col[h]
        m_new.append(jnp.maximum(b_last + m_old[h], jnp.max(log_w, axis=0, keepdims=True)))
        decay.append(jnp.exp(b_last + m_old[h] - m_new[h]))
        kw.append((k[h] * jnp.exp(log_w - m_new[h])).astype(BF16))
    kv = [lax.dot_general(kw[h], v[h], (((0,), (0,)), ((), ())), preferred_element_type=F32) for h in heads]

    for h in heads:
        inter = jnp.exp(m_old[h] - g_rep[h])
        floor = jnp.exp(-(jnp.broadcast_to(b_col[h], (L, aux)) + g_rep[h]))
        den = inter * qc[h][:, hd:] + sv[h][:, hd:]
        inv = 1.0 / jnp.maximum(jnp.abs(den), floor)
        hh = (lanes(inter) * qc[h][:, :hd] + sv[h][:, :hd]) * lanes(inv)
        if final:
            ht = hh + hprev_ref[:, hsl[h]]
            hn = ht * lax.rsqrt(jnp.mean(ht * ht, axis=-1, keepdims=True) + EPS)
            y = hn * gain_ref[:, hsl[h]] * jax.nn.sigmoid(o_ref[:, hsl[h]].astype(F32)) \
                * _silu(z_ref[:, hsl[h]].astype(F32))
            out_ref[:, hsl[h]] = y.astype(BF16)
        else:
            out_ref[:, hsl[h]] = hh

    for h in heads:
        c_s[h] = decay[h] * c_old[h] + kv[h]
        m_s[h] = jnp.broadcast_to(m_new[h], m_s.shape[1:])


def _mlstm_pass(ya, gate_arrays, gain, hprev, batch, tiles_per_batch, reverse):
    m = ya.shape[0]
    lat = tiles_per_batch - 1
    final = hprev is not None

    def rows(b, c):
        r = jnp.where(c == 0, lat, (lat - c) if reverse else (c - 1))
        return b * tiles_per_batch + r

    def col(cidx):
        return pl.BlockSpec((ROW_TILE, BRANCH_W), lambda b, c: (rows(b, c), cidx))

    gate_spec = pl.BlockSpec((ROW_TILE, GATE_LANES), lambda b, c: (rows(b, c), 0))
    gate_t_spec = pl.BlockSpec((GATE_LANES, ROW_TILE), lambda b, c: (0, rows(b, c)))
    in_specs = [col(A_MQ), col(A_MK), col(A_MV), gate_spec, gate_t_spec, gate_spec, gate_t_spec]
    args = [ya, ya, ya, *gate_arrays]
    if final:
        in_specs += [col(A_MO), col(A_MZ), pl.BlockSpec((1, BRANCH_W), lambda b, c: (0, 0)),
                     pl.BlockSpec((ROW_TILE, BRANCH_W), lambda b, c: (rows(b, c), 0))]
        args += [ya, ya, gain.reshape(1, BRANCH_W), hprev]
    return pl.pallas_call(
        functools.partial(_mlstm_kernel, reverse=reverse, final=final),
        out_shape=jax.ShapeDtypeStruct((m, BRANCH_W), BF16 if final else F32),
        grid=(batch, tiles_per_batch),
        in_specs=in_specs,
        out_specs=pl.BlockSpec((ROW_TILE, BRANCH_W), lambda b, c: (rows(b, c), 0)),
        scratch_shapes=[pltpu.VMEM((MLSTM_HEADS, MLSTM_HEAD_DIM, MLSTM_HEAD_DIM + MLSTM_AUX), F32),
                        pltpu.VMEM((MLSTM_HEADS, V7X_SUBLANES, V7X_LANES), F32)],
        compiler_params=_cparams(("parallel", "arbitrary")),
        name="mlstm_bwd" if reverse else "mlstm_fwd",
    )(*args)


def _segment_info(r, lat):
    has_prev = jnp.logical_and(r != 0, r != lat)
    has_next = jnp.logical_and(r != lat - 1, r != lat)
    seg_len = jnp.where(r == lat, CTX_LEN, lat * ROW_TILE)
    t0 = jnp.where(r == lat, 0, r * ROW_TILE)
    return has_prev, has_next, seg_len, t0


def _conv_strip(j, r, lat, cu_ref, cb_ref, cc_ref, cz_ref, cu_p, cc_p, cu_n, cc_n, cw_ref):
    has_prev, has_next, _, _ = _segment_info(r, lat)
    rowi = lax.broadcasted_iota(jnp.int32, (ROW_TILE, 1), 0)
    cs = slice(j * V7X_LANES, (j + 1) * V7X_LANES)
    a = cc_ref[:, cs].astype(F32) * cu_ref[:, cs].astype(F32)
    a_prev = jnp.where(has_prev, cc_p[HALO - 1:HALO, cs].astype(F32) * cu_p[HALO - 1:HALO, cs].astype(F32), 0.0)
    a_next = jnp.where(has_next, cc_n[0:1, cs].astype(F32) * cu_n[0:1, cs].astype(F32), 0.0)
    a_m1 = jnp.where(rowi == 0, a_prev, pltpu.roll(a, 1, 0))
    a_p1 = jnp.where(rowi == ROW_TILE - 1, a_next, pltpu.roll(a, ROW_TILE - 1, 0))
    y = cw_ref[0:1, cs] * a_m1 + cw_ref[1:2, cs] * a + cw_ref[2:3, cs] * a_p1
    return (cb_ref[:, cs].astype(F32) * y * _silu(cz_ref[:, cs].astype(F32))).astype(BF16)


def _pool_branch(r, lat, pu_ref, pz_ref, pu_p, pu_n, pw_ref, ps_ref, band_ref, yd_ref):
    has_prev, has_next, seg_len, t0 = _segment_info(r, lat)
    rowi = lax.broadcasted_iota(jnp.int32, (ROW_TILE, 1), 0)
    u = pu_ref[...]
    halo_zero = jnp.zeros((HALO, BRANCH_W), u.dtype)
    ext = jnp.concatenate([u, jnp.where(has_prev, pu_p[...], halo_zero), jnp.where(has_next, pu_n[...], halo_zero),
                           jnp.zeros((POOL_K - ROW_TILE - 2 * HALO, BRANCH_W), u.dtype)], axis=0)
    t = t0 + rowi
    groups = range(len(POOL_WINDOWS))
    gsl = [slice(g * POOL_GROUP, (g + 1) * POOL_GROUP) for g in groups]
    acc = [jnp.dot(band_ref[g], ext[:, gsl[g]], preferred_element_type=F32) for g in groups]
    dev = []
    for g, w in enumerate(POOL_WINDOWS):
        inv_cnt = 1.0 / (jnp.minimum(t + (w - w // 2), seg_len) - jnp.maximum(t - w // 2, 0)).astype(F32)
        dev.append((acc[g] * inv_cnt - u[:, gsl[g]].astype(F32)).astype(BF16))
    pg = [jnp.dot(dev[g], pw_ref[g], preferred_element_type=F32) for g in groups]
    for g in groups:
        gate = _silu(pz_ref[:, gsl[g]].astype(F32)) * ps_ref[:, gsl[g]]
        yd_ref[:, gsl[g]] = (pg[g] * gate).astype(BF16)


def _pool_band():
    t = np.arange(ROW_TILE)[:, None]
    pos = np.concatenate([np.arange(ROW_TILE), np.arange(-HALO, 0), np.arange(ROW_TILE, ROW_TILE + HALO),
                          np.full(POOL_K - ROW_TILE - 2 * HALO, -10 * ROW_TILE)])[None, :]
    return np.stack([(pos >= t - w // 2) & (pos < t + w - w // 2) for w in POOL_WINDOWS]).astype(np.float32)


def _mergeout_kernel(*refs, final, tiles_per_batch):
    (b0, b1, cu_ref, cb_ref, cc_ref, cz_ref, pu_ref, pz_ref, cu_p, cc_p, pu_p, cu_n, cc_n, pu_n,
     cw_ref, pw_ref, ps_ref, band_ref, g_ref, wb_ref, wo_ref, x_ref, gt_ref) = refs[:23]
    if final:
        fg_ref, o_ref, yd_s = refs[23:]
    else:
        ng_ref, nsc_ref, nsh_ref, o_ref, h_ref, yd_s = refs[23:]
    r, lat = pl.program_id(1), tiles_per_batch - 1

    def gated(j, br):
        gate = jax.nn.sigmoid(g_ref[:, j * D_MODEL:(j + 1) * D_MODEL].astype(F32))
        return gate * jnp.dot(br, wb_ref[j], preferred_element_type=F32)

    acc = gated(0, b0[...]) + gated(1, b1[...])
    strips = [_conv_strip(j, r, lat, cu_ref, cb_ref, cc_ref, cz_ref, cu_p, cc_p, cu_n, cc_n, cw_ref)
              for j in range(BRANCH_W // V7X_LANES)]
    _pool_branch(r, lat, pu_ref, pz_ref, pu_p, pu_n, pw_ref, ps_ref, band_ref, yd_s)
    acc = acc + gated(2, jnp.concatenate(strips, axis=1)) + gated(3, yd_s[...])
    y = x_ref[...] + gt_ref[0] * jnp.dot(acc.astype(BF16), wo_ref[...], preferred_element_type=F32)
    yn = y * lax.rsqrt(jnp.mean(y * y, axis=-1, keepdims=True) + EPS)
    if final:
        o_ref[...] = yn * fg_ref[...]
    else:
        o_ref[...] = y
        h_ref[...] = (yn * ng_ref[...] * (1.0 + nsc_ref[0]) + nsh_ref[0]).astype(BF16)


def _mergeout(y_att, y_mls, yb, conv_w, pool_w, pool_scale, w_branch, w_out, layer, xs, gt_t, final_gain,
              next_norm, batch, tiles_per_batch):
    m, d = xs.shape
    final = final_gain is not None
    lat = tiles_per_batch - 1
    n_r = lat if final else tiles_per_batch
    resident = pl.Buffered(1)
    hb = ROW_TILE // HALO
    n_halo = m // HALO
    n_win = len(POOL_WINDOWS)

    def src(b, r):
        return b * tiles_per_batch + r

    def col(cidx):
        return pl.BlockSpec((ROW_TILE, BRANCH_W), lambda b, r: (src(b, r), cidx))

    def prev(cidx):
        return pl.BlockSpec((HALO, BRANCH_W), lambda b, r: (jnp.maximum(src(b, r) * hb - 1, 0), cidx))

    def nxt(cidx):
        return pl.BlockSpec((HALO, BRANCH_W), lambda b, r: (jnp.minimum((src(b, r) + 1) * hb, n_halo - 1), cidx))

    in_specs = [col(0), col(0),
                col(B_CU), col(B_CB), col(B_CC), col(B_CZ), col(B_PU), col(B_PZ),
                prev(B_CU), prev(B_CC), prev(B_PU), nxt(B_CU), nxt(B_CC), nxt(B_PU),
                pl.BlockSpec((None, 3, BRANCH_W), lambda b, r: (layer, 0, 0)),
                pl.BlockSpec((None, n_win, POOL_GROUP, POOL_GROUP), lambda b, r: (layer, 0, 0, 0)),
                pl.BlockSpec((None, 1, BRANCH_W), lambda b, r: (layer, 0, 0)),
                pl.BlockSpec((n_win, ROW_TILE, POOL_K), lambda b, r: (0, 0, 0)),
                pl.BlockSpec((ROW_TILE, N_BRANCH * d), lambda b, r: (src(b, r), B_GATE)),
                pl.BlockSpec((None, N_BRANCH, BRANCH_W, d), lambda b, r: (layer, 0, 0, 0), pipeline_mode=resident),
                pl.BlockSpec((None, d, d), lambda b, r: (layer, 0, 0), pipeline_mode=resident),
                pl.BlockSpec((ROW_TILE, d), lambda b, r: (src(b, r), 0)),
                pl.BlockSpec((1, 1, d), lambda b, r: (src(b, r), 0, 0))]
    args = [y_att, y_mls, *([yb] * 12), conv_w, pool_w, pool_scale.reshape(-1, 1, BRANCH_W),
            jnp.asarray(_pool_band(), BF16), yb, w_branch, w_out, xs, gt_t]
    row_spec = pl.BlockSpec((ROW_TILE, d), lambda b, r: (b * n_r + r, 0))
    out_shape = jax.ShapeDtypeStruct((batch * n_r * ROW_TILE, d), F32)
    if final:
        in_specs.append(pl.BlockSpec((1, d), lambda b, r: (0, 0)))
        args.append(final_gain.reshape(1, d))
        out_specs = row_spec
    else:
        tab_spec = pl.BlockSpec((1, 1, d), lambda b, r: (src(b, r), 0, 0))
        in_specs += [pl.BlockSpec((1, d), lambda b, r: (0, 0)), tab_spec, tab_spec]
        args += [next_norm[0].reshape(1, d), next_norm[1], next_norm[2]]
        out_shape = (out_shape, jax.ShapeDtypeStruct((m, d), BF16))
        out_specs = (row_spec, row_spec)
    return pl.pallas_call(
        functools.partial(_mergeout_kernel, final=final, tiles_per_batch=tiles_per_batch),
        out_shape=out_shape,
        grid=(batch, n_r),
        in_specs=in_specs,
        out_specs=out_specs,
        scratch_shapes=[pltpu.VMEM((ROW_TILE, BRANCH_W), BF16)],
        compiler_params=_cparams(("parallel", "parallel")),
        name="mergeout_final" if final else "mergeout",
    )(*args)


def _rope_tables(t_lat):
    pos = np.arange(t_lat)
    quarter = ATT_HEAD_DIM // 4
    freq = ROPE_THETA ** (-jnp.arange(quarter, dtype=F32) / quarter)
    a_row = jnp.asarray(pos // GRID_W, F32)[:, None] * freq[None, :]
    a_col = jnp.asarray(pos % GRID_W, F32)[:, None] * freq[None, :]
    cos_t = jnp.concatenate([jnp.cos(a_row), jnp.cos(a_row), jnp.cos(a_col), jnp.cos(a_col)], axis=-1)
    sin_t = jnp.concatenate([-jnp.sin(a_row), jnp.sin(a_row), -jnp.sin(a_col), jnp.sin(a_col)], axis=-1)
    return cos_t, sin_t


def kernel(x, c, ctx, c_ctx, norm_gain, w_mod, b_mod, w_in, q_norm_gain, k_norm_gain, mlstm_gate_bias,
           mlstm_norm_gain, conv_w, pool_w, pool_scale, w_branch, w_out, final_norm_gain):
    batch, t_lat, d = x.shape
    depth = w_in.shape[0]
    assert d == D_MODEL and ctx.shape[1] == CTX_LEN and t_lat % ROW_TILE == 0 and batch < 8
    s_len = t_lat + CTX_LEN
    tiles_per_batch = s_len // ROW_TILE

    cc = jnp.zeros((8, d), F32).at[:batch].set(c).at[batch].set(c_ctx)
    mod = _modulation(cc, w_mod, b_mod)
    tile_row = np.array([b if r < tiles_per_batch - 1 else batch
                         for b in range(batch) for r in range(tiles_per_batch)], np.int32)
    cos_t, sin_t = _rope_tables(t_lat)
    assert w_in.shape[1:] == (d, N_IN)
    w_t = jnp.swapaxes(w_in, 1, 2).reshape(depth * N_IN, d)
    w_branch_b, w_out_b, pool_w_b = w_branch.astype(BF16), w_out.astype(BF16), pool_w.astype(BF16)

    tables = []
    for l in range(depth):
        mod_t = mod[l][tile_row][:, None, :]
        tables.append((mod_t[..., :d], mod_t[..., d:2 * d], mod_t[..., 2 * d:]))

    out = None
    h, xs = _normmod_first(x, ctx, norm_gain[0], tables[0][1], tables[0][0])
    for l in range(depth):
        gt_t = tables[l][2]
        bias = jnp.zeros((1, GATE_LANES), F32).at[0, :4 * MLSTM_HEADS].set(mlstm_gate_bias[l].reshape(-1))
        ya = _inproj(h, w_t, l, 7, lambda j: jnp.where(j == 0, 0, 512 + IN_UNIT * j), "inproj_a")
        yb = _inproj(h, w_t, l, 14, lambda j: jnp.where(j < 8, W_MERGE_ROW + IN_UNIT * j,
                                                       W_LOCAL_ROW + IN_UNIT * (j - 8)), "inproj_b")
        kv, g, g_t, pre, pre_t, suf, suf_t = _inproj_tail(h, w_t, l, bias)

        qb, kb, vt = _qkvprep(ya, kv, cos_t, sin_t, q_norm_gain[l], k_norm_gain[l], batch, tiles_per_batch)
        y_att = _attention(qb, ya, kb, vt, batch, s_len)
        h_f = _mlstm_pass(ya, (g, g_t, pre, pre_t), None, None, batch, tiles_per_batch, reverse=False)
        y_mls = _mlstm_pass(ya, (g, g_t, suf, suf_t), mlstm_norm_gain[l], h_f, batch, tiles_per_batch,
                            reverse=True)
        if l == depth - 1:
            out = _mergeout(y_att, y_mls, yb, conv_w, pool_w_b, pool_scale, w_branch_b, w_out_b, l, xs, gt_t,
                            final_norm_gain, None, batch, tiles_per_batch).reshape(batch, t_lat, d)
        else:
            next_norm = (norm_gain[l + 1], tables[l + 1][1], tables[l + 1][0])
            xs, h = _mergeout(y_att, y_mls, yb, conv_w, pool_w_b, pool_scale, w_branch_b, w_out_b, l, xs, gt_t,
                              None, next_norm, batch, tiles_per_batch)
    return out
```

```python
import functools

import numpy as np
import jax
import jax.numpy as jnp
from jax import lax
from jax.experimental import pallas as pl
from jax.experimental.pallas import tpu as pltpu

F32 = jnp.float32
BF16 = jnp.bfloat16

D_MODEL = 2048
BRANCH_W = 1024
GRID_W = 64
CTX_LEN = 256
EPS = 1e-6
ATT_HEAD_DIM = 128
ATT_GROUP = 4
ATT_KV_HEADS = 2
ATT_SCALE = ATT_HEAD_DIM ** -0.5
LOG2_E = 1.4426950408889634
ROPE_THETA = 10000.0
MLSTM_HEADS = 4
MLSTM_HEAD_DIM = 256
MLSTM_K_SCALE = MLSTM_HEAD_DIM ** -0.5
MLSTM_AUX = 128
M_INIT = -1e30
POOL_WINDOWS = (2, 4, 8, 16)
POOL_GROUP = 256
N_BRANCH = 4

ROW_TILE = 256
HALO = 16
POOL_K = 384
ATT_Q_TILE = 256
ATT_KEY_CHUNK = 256
VT_ROWS = ATT_HEAD_DIM + 16
V7X_VMEM_LIMIT = 56 * 1024 * 1024
V7X_LANES = 128
V7X_SUBLANES = 8
GATE_LANES = V7X_LANES

A_AQ, A_AZ, A_MQ, A_MK, A_MV, A_MO, A_MZ = range(7)
B_GATE = 0
B_CU, B_CB, B_CC, B_CZ, B_PU, B_PZ = range(8, 14)


def _cparams(sem, vmem=V7X_VMEM_LIMIT):
    return pltpu.CompilerParams(dimension_semantics=sem, vmem_limit_bytes=vmem)


def _silu(x):
    return x * jax.nn.sigmoid(x)


def _mod_kernel(c_ref, w_ref, b_ref, o_ref):
    a = _silu(c_ref[...]).astype(BF16)
    o_ref[...] = jnp.dot(a, w_ref[...].astype(BF16), preferred_element_type=F32) + b_ref[...]


def _modulation(cc, w_mod, b_mod):
    depth, d, n = w_mod.shape
    tn = 768
    return pl.pallas_call(
        _mod_kernel,
        out_shape=jax.ShapeDtypeStruct((depth, 8, n), F32),
        grid=(depth, n // tn),
        in_specs=[pl.BlockSpec((8, d), lambda l, j: (0, 0)),
                  pl.BlockSpec((None, d, tn), lambda l, j: (l, 0, j)),
                  pl.BlockSpec((None, 1, tn), lambda l, j: (l, 0, j))],
        out_specs=pl.BlockSpec((None, 8, tn), lambda l, j: (l, 0, j)),
        compiler_params=_cparams(("parallel", "parallel")),
        name="modulation",
    )(cc, w_mod, b_mod.reshape(depth, 1, n))


def _normmod_first_kernel(x_ref, c_ref, g_ref, sc_ref, sh_ref, o_ref, xs_ref, *, tiles_per_batch):
    x = jnp.where(pl.program_id(1) == tiles_per_batch - 1, c_ref[...], x_ref[...])
    xs_ref[...] = x
    y = x * lax.rsqrt(jnp.mean(x * x, axis=-1, keepdims=True) + EPS) * g_ref[...]
    o_ref[...] = (y * (1.0 + sc_ref[0]) + sh_ref[0]).astype(BF16)


def _normmod_first(x, ctx, gain, sc_t, sh_t):
    batch, t_lat, d = x.shape
    tpb = t_lat // ROW_TILE + 1
    m = batch * tpb * ROW_TILE
    tab_spec = pl.BlockSpec((1, 1, d), lambda b, r: (b * tpb + r, 0, 0))
    row_spec = pl.BlockSpec((ROW_TILE, d), lambda b, r: (b * tpb + r, 0))
    return pl.pallas_call(
        functools.partial(_normmod_first_kernel, tiles_per_batch=tpb),
        out_shape=(jax.ShapeDtypeStruct((m, d), BF16), jax.ShapeDtypeStruct((m, d), F32)),
        grid=(batch, tpb),
        in_specs=[pl.BlockSpec((None, ROW_TILE, d), lambda b, r: (b, jnp.minimum(r, tpb - 2), 0)),
                  pl.BlockSpec((None, CTX_LEN, d), lambda b, r: (b, 0, 0)),
                  pl.BlockSpec((1, d), lambda b, r: (0, 0)),
                  tab_spec, tab_spec],
        out_specs=(row_spec, row_spec),
        compiler_params=_cparams(("parallel", "parallel")),
        name="normmod_first",
    )(x, ctx, gain.reshape(1, d), sc_t, sh_t)


IN_UNIT = 1024
W_KV_ROW, W_MGATE_ROW, W_LOCAL_ROW, W_MERGE_ROW = 1024, 7680, 7696, 13840
N_IN = W_MERGE_ROW + N_BRANCH * D_MODEL


def _pick_tile(n, candidates):
    for t in candidates:
        if n % t == 0:
            return t
    raise ValueError(f"no tile for {n}")


def _inproj_kernel(h_ref, w_ref, o_ref, wt_s, *, n_pieces):
    j, i = pl.program_id(0), pl.program_id(1)

    @pl.when(jnp.logical_and(j == 0, i == 0))
    def _():
        wt_s[0] = w_ref[...].T.astype(BF16)

    slot = j % 2
    o_ref[...] = jnp.dot(h_ref[...], wt_s[slot], preferred_element_type=F32).astype(o_ref.dtype)
    piece = IN_UNIT // n_pieces
    rows = pl.ds(pl.multiple_of(((i + n_pieces - 1) % n_pieces) * piece, piece), piece)
    wt_s[1 - slot, :, rows] = w_ref[rows, :].T.astype(BF16)


def _inproj(h, w_t, layer, n_units, unit_row, name):
    m, k = h.shape
    tm = _pick_tile(m, (1024, 512, 256))
    n_tiles = m // tm
    n_pieces = max(p for p in (1, 2, 4, 8) if p < n_tiles)
    base = layer * N_IN

    def w_row(j, i):
        return pl.multiple_of(base + unit_row(jnp.minimum(j + jnp.minimum(i, 1), n_units - 1)), 8)

    return pl.pallas_call(
        functools.partial(_inproj_kernel, n_pieces=n_pieces),
        out_shape=jax.ShapeDtypeStruct((m, n_units * IN_UNIT), BF16),
        grid=(n_units, n_tiles),
        in_specs=[pl.BlockSpec((tm, k), lambda j, i: (i, 0)),
                  pl.BlockSpec((pl.Element(IN_UNIT), pl.Element(k)), lambda j, i: (w_row(j, i), 0))],
        out_specs=pl.BlockSpec((tm, IN_UNIT), lambda j, i: (i, j)),
        scratch_shapes=[pltpu.VMEM((2, k, IN_UNIT), BF16)],
        compiler_params=_cparams(("arbitrary", "arbitrary")),
        name=name,
    )(h, w_t)


def _inproj_tail_kernel(h_ref, wkv_ref, wg_ref, bias_ref,
                        kv_out, g_out, gt_out, pre_out, pret_out, suf_out, suft_out, wt_s):
    kvw = wkv_ref.shape[0]
    L = ROW_TILE

    @pl.when(pl.program_id(0) == 0)
    def _():
        wt_s[:, :kvw] = wkv_ref[...].T.astype(BF16)
        wt_s[:, kvw:] = wg_ref[...].T.astype(BF16)

    y = jnp.dot(h_ref[...], wt_s[...], preferred_element_type=F32)
    kv_out[...] = y[:, :kvw]
    g = y[:, kvw:]
    lane = lax.broadcasted_iota(jnp.int32, g.shape, 1)
    gates = jnp.where(lane < 4 * MLSTM_HEADS, g, 0.0) + bias_ref[...]
    log_f = jnp.minimum(gates, 0.0) - jnp.log1p(jnp.exp(-jnp.abs(gates)))
    p1 = log_f.astype(BF16)
    r1 = log_f - p1.astype(F32)
    p2 = r1.astype(BF16)
    p3 = (r1 - p2.astype(F32)).astype(BF16)
    parts = jnp.concatenate([p1, p2, p3], axis=1)
    row = lax.broadcasted_iota(jnp.int32, (L, L), 0)
    col = lax.broadcasted_iota(jnp.int32, (L, L), 1)
    lower = (col <= row).astype(BF16)
    chunks = [slice(c * L, (c + 1) * L) for c in range(g.shape[0] // L)]
    sums = [jnp.dot(lower, parts[rs], preferred_element_type=F32) for rs in chunks]
    gl = GATE_LANES
    pre = [s[:, 0:gl] + s[:, gl:2 * gl] + s[:, 2 * gl:3 * gl] for s in sums]
    suf = [p[L - 1:L, :] - p + log_f[rs] for p, rs in zip(pre, chunks)]
    g_out[...] = gates
    for c, rs in enumerate(chunks):
        pre_out[rs, :] = pre[c]
        suf_out[rs, :] = suf[c]
        gt_out[:, rs] = gates[rs].T
        pret_out[:, rs] = pre[c].T
        suft_out[:, rs] = suf[c].T


def _inproj_tail(h, w_t, layer, bias):
    m, k = h.shape
    tm = _pick_tile(m, (1024, 512, 256))
    kvw = 2 * ATT_KV_HEADS * ATT_HEAD_DIM
    gl = GATE_LANES
    row_sd, col_sd = jax.ShapeDtypeStruct((m, gl), F32), jax.ShapeDtypeStruct((gl, m), F32)
    row_spec = pl.BlockSpec((tm, gl), lambda i: (i, 0))
    col_spec = pl.BlockSpec((gl, tm), lambda i: (0, i))
    return pl.pallas_call(
        _inproj_tail_kernel,
        out_shape=(jax.ShapeDtypeStruct((m, kvw), F32), row_sd, col_sd, row_sd, col_sd, row_sd, col_sd),
        grid=(m // tm,),
        in_specs=[pl.BlockSpec((tm, k), lambda i: (i, 0)),
                  pl.BlockSpec((pl.Element(kvw), pl.Element(k)), lambda i: (layer * N_IN + W_KV_ROW, 0)),
                  pl.BlockSpec((pl.Element(gl), pl.Element(k)), lambda i: (layer * N_IN + W_MGATE_ROW, 0)),
                  pl.BlockSpec((1, gl), lambda i: (0, 0))],
        out_specs=(pl.BlockSpec((tm, kvw), lambda i: (i, 0)),
                   row_spec, col_spec, row_spec, col_spec, row_spec, col_spec),
        scratch_shapes=[pltpu.VMEM((k, kvw + gl), BF16)],
        compiler_params=_cparams(("arbitrary",)),
        name="inproj_tail",
    )(h, w_t, w_t, bias)


def _qkvprep_kernel(q_ref, t_ref, cos_ref, sin_ref, qg_ref, kg_ref, perm_ref, q_out, k_out, vt_out, *,
                    tiles_per_batch):
    is_lat = pl.program_id(1) < tiles_per_batch - 1
    cos, sin, kg = cos_ref[...], sin_ref[...], kg_ref[...]
    qg = qg_ref[...] * (ATT_SCALE * LOG2_E)
    hd = ATT_HEAD_DIM
    n_q = ATT_KV_HEADS * ATT_GROUP
    sls = [slice(h * hd, (h + 1) * hd) for h in range(n_q)]
    xs = [q_ref[:, sl].astype(F32) for sl in sls] + [t_ref[:, sl] for sl in sls[:ATT_KV_HEADS]]
    gains = [qg] * n_q + [kg] * ATT_KV_HEADS
    xs = [x * lax.rsqrt(jnp.mean(x * x, axis=-1, keepdims=True) + EPS) * g for x, g in zip(xs, gains)]
    x_all = jnp.concatenate(xs, axis=0)
    hi = x_all.astype(BF16)
    lo = (x_all - hi.astype(F32)).astype(BF16)
    partner = jnp.dot(jnp.concatenate([hi, lo], axis=1), perm_ref[...], preferred_element_type=F32)
    rows = xs[0].shape[0]
    outs = [jnp.where(is_lat, x * cos + partner[i * rows:(i + 1) * rows] * sin, x).astype(BF16)
            for i, x in enumerate(xs)]
    for h in range(n_q):
        q_out[:, sls[h]] = outs[h]
    for h in range(ATT_KV_HEADS):
        k_out[:, sls[h]] = outs[n_q + h]
        vt_out[h, 0:hd, :] = t_ref[:, sls[ATT_KV_HEADS + h]].T.astype(BF16)
        vt_out[h, hd:, :] = jnp.ones((VT_ROWS - hd, ROW_TILE), BF16)


def _qkvprep(ya, kv, cos_t, sin_t, q_gain, k_gain, batch, tiles_per_batch):
    m = kv.shape[0]
    lat_tiles = tiles_per_batch - 1
    kvw = ATT_KV_HEADS * ATT_HEAD_DIM
    rope_spec = pl.BlockSpec((ROW_TILE, ATT_HEAD_DIM), lambda b, r: (jnp.minimum(r, lat_tiles - 1), 0))
    gain_spec = pl.BlockSpec((1, ATT_HEAD_DIM), lambda b, r: (0, 0))
    lane = np.arange(ATT_HEAD_DIM)
    swap = (lane[:, None] == (lane[None, :] ^ (ATT_HEAD_DIM // 4))).astype(np.float32)
    return pl.pallas_call(
        functools.partial(_qkvprep_kernel, tiles_per_batch=tiles_per_batch),
        out_shape=(jax.ShapeDtypeStruct((m, BRANCH_W), BF16),
                   jax.ShapeDtypeStruct((m, kvw), BF16),
                   jax.ShapeDtypeStruct((batch, ATT_KV_HEADS, VT_ROWS, tiles_per_batch * ROW_TILE), BF16)),
        grid=(batch, tiles_per_batch),
        in_specs=[pl.BlockSpec((ROW_TILE, BRANCH_W), lambda b, r: (b * tiles_per_batch + r, A_AQ)),
                  pl.BlockSpec((ROW_TILE, 2 * kvw), lambda b, r: (b * tiles_per_batch + r, 0)),
                  rope_spec, rope_spec, gain_spec, gain_spec,
                  pl.BlockSpec((2 * ATT_HEAD_DIM, ATT_HEAD_DIM), lambda b, r: (0, 0))],
        out_specs=(pl.BlockSpec((ROW_TILE, BRANCH_W), lambda b, r: (b * tiles_per_batch + r, 0)),
                   pl.BlockSpec((ROW_TILE, kvw), lambda b, r: (b * tiles_per_batch + r, 0)),
                   pl.BlockSpec((None, ATT_KV_HEADS, VT_ROWS, ROW_TILE), lambda b, r: (b, 0, 0, r))),
        compiler_params=_cparams(("parallel", "parallel")),
        name="qkvprep",
    )(ya, kv, cos_t, sin_t, q_gain.reshape(1, ATT_HEAD_DIM), k_gain.reshape(1, ATT_HEAD_DIM),
      jnp.asarray(np.tile(swap, (2, 1)), BF16))


def _attn_kernel(q_ref, z_ref, k_ref, vt_ref, o_ref, *, n_lat_q, t_lat, kc):
    is_lat = pl.program_id(2) < n_lat_q
    hd = ATT_HEAD_DIM
    q = jnp.concatenate([q_ref[:, g * hd:(g + 1) * hd] for g in range(ATT_GROUP)], axis=0)

    def attend(chunks):
        def scores(c):
            off, n = chunks[c]
            return lax.dot_general(k_ref[off:off + n, :], q, (((1,), (1,)), ((), ())),
                                   preferred_element_type=F32)

        m, o = None, None
        s_next = scores(0)
        for c, (off, n) in enumerate(chunks):
            s = s_next
            if c + 1 < len(chunks):
                s_next = scores(c + 1)
            mc = jnp.max(s, axis=0, keepdims=True)
            m_new = mc if m is None else jnp.maximum(m, mc)
            p = jnp.exp2((s - m_new).astype(BF16))
            oc = jnp.dot(vt_ref[:, off:off + n], p, preferred_element_type=F32)
            o = oc if o is None else jnp.exp2(m - m_new) * o + oc
            m = m_new
        o = o[0:hd] / o[hd:hd + 1]
        for g in range(ATT_GROUP):
            sl = slice(g * ATT_HEAD_DIM, (g + 1) * ATT_HEAD_DIM)
            og = o[:, g * ATT_Q_TILE:(g + 1) * ATT_Q_TILE].T
            o_ref[:, sl] = (og * _silu(z_ref[:, sl].astype(F32))).astype(BF16)

    ctx_chunk = (t_lat, CTX_LEN)

    @pl.when(is_lat)
    def _():
        attend([(off, kc) for off in range(0, t_lat, kc)] + [ctx_chunk])

    @pl.when(jnp.logical_not(is_lat))
    def _():
        attend([ctx_chunk])


def _attention(qb, ya, kb, vt, batch, s_len):
    m = ya.shape[0]
    t_lat = s_len - CTX_LEN
    nq = s_len // ATT_Q_TILE
    n_lat_q = t_lat // ATT_Q_TILE
    gw = ATT_GROUP * ATT_HEAD_DIM
    return pl.pallas_call(
        functools.partial(_attn_kernel, n_lat_q=n_lat_q, t_lat=t_lat,
                          kc=_pick_tile(t_lat, (ATT_KEY_CHUNK, 512, 256))),
        out_shape=jax.ShapeDtypeStruct((m, BRANCH_W), BF16),
        grid=(batch, ATT_KV_HEADS, nq),
        in_specs=[pl.BlockSpec((ATT_Q_TILE, gw), lambda b, h, i: (b * nq + i, h)),
                  pl.BlockSpec((ATT_Q_TILE, gw), lambda b, h, i: (b * nq + i, A_AZ * 2 + h)),
                  pl.BlockSpec((s_len, ATT_HEAD_DIM), lambda b, h, i: (b, h)),
                  pl.BlockSpec((None, None, VT_ROWS, s_len), lambda b, h, i: (b, h, 0, 0))],
        out_specs=pl.BlockSpec((ATT_Q_TILE, gw), lambda b, h, i: (b * nq + i, h)),
        compiler_params=_cparams(("parallel", "parallel", "arbitrary")),
        name="attention",
    )(qb, ya, kb, vt)


def _mlstm_kernel(*refs, reverse, final):
    if final:
        (q_ref, k_ref, v_ref, g_ref, gt_ref, bc_ref, bct_ref, o_ref, z_ref, gain_ref, hprev_ref,
         out_ref, c_s, m_s) = refs
    else:
        q_ref, k_ref, v_ref, g_ref, gt_ref, bc_ref, bct_ref, out_ref, c_s, m_s = refs
    L = ROW_TILE
    hd = MLSTM_HEAD_DIM
    aux = MLSTM_AUX

    @pl.when(pl.program_id(1) == 0)
    def _():
        c_s[...] = jnp.zeros(c_s.shape, F32)
        m_s[...] = jnp.full(m_s.shape, M_INIT, F32)

    def lanes(x):
        return jnp.concatenate([x] * (hd // aux), axis=1)

    gates, gates_t = g_ref[...], gt_ref[...]
    bcum, bcum_t = bc_ref[...], bct_ref[...]
    row = lax.broadcasted_iota(jnp.int32, (L, L), 0)
    col = lax.broadcasted_iota(jnp.int32, (L, L), 1)
    tri = (col >= row) if reverse else (col <= row)
    last = 0 if reverse else L - 1
    heads = range(MLSTM_HEADS)
    hsl = [slice(h * hd, (h + 1) * hd) for h in heads]
    icol = [(2 if reverse else 0) * MLSTM_HEADS + h for h in heads]
    fcol = [(3 if reverse else 1) * MLSTM_HEADS + h for h in heads]
    c_old = [c_s[h] for h in heads]
    m_old = [m_s[h, 0:1, 0:1] for h in heads]
    q = [q_ref[:, hsl[h]] for h in heads]
    k = [k_ref[:, hsl[h]].astype(F32) * MLSTM_K_SCALE for h in heads]
    ones = jnp.ones((L, aux), v_ref.dtype)
    v = [jnp.concatenate([v_ref[:, hsl[h]], ones], axis=1) for h in heads]
    b_col = [bcum[:, fcol[h]:fcol[h] + 1] for h in heads]
    i_col = [gates[:, icol[h]:icol[h] + 1] for h in heads]

    qk = [lax.dot_general(q[h], k[h].astype(BF16), (((1,), (1,)), ((), ())), preferred_element_type=F32)
          for h in heads]
    qc = [jnp.dot(q[h], c_old[h].astype(BF16), preferred_element_type=F32) for h in heads]

    g_rep, sm = [], []
    for h in heads:
        a_row = gates_t[icol[h]:icol[h] + 1, :] - bcum_t[fcol[h]:fcol[h] + 1, :]
        masked = jnp.where(tri, a_row, -jnp.inf)
        g_t = jnp.maximum(m_old[h], jnp.max(masked, axis=1, keepdims=True))
        g_rep.append(jnp.broadcast_to(g_t, (L, aux)))
        sm.append((qk[h] * jnp.exp(masked - lanes(g_rep[h]))).astype(BF16))
    sv = [jnp.dot(sm[h], v[h], preferred_element_type=F32) for h in heads]

    kw, decay, m_new = [], [], []
    for h in heads:
        b_last = bcum[last:last + 1, fcol[h]:fcol[h] + 1]
        log_w = b_last - b_col[h] + i_col[h]
        m_new.append(jnp.maximum(b_last + m_old[h], jnp.max(log_w, axis=0, keepdims=True)))
        decay.append(jnp.exp(b_last + m_old[h] - m_new[h]))
        kw.append((k[h] * jnp.exp(log_w - m_new[h])).astype(BF16))
    kv = [lax.dot_general(kw[h], v[h], (((0,), (0,)), ((), ())), preferred_element_type=F32) for h in heads]

    for h in heads:
        inter = jnp.exp(m_old[h] - g_rep[h])
        floor = jnp.exp(-(jnp.broadcast_to(b_col[h], (L, aux)) + g_rep[h]))
        den = inter * qc[h][:, hd:] + sv[h][:, hd:]
        inv = 1.0 / jnp.maximum(jnp.abs(den), floor)
        hh = (lanes(inter) * qc[h][:, :hd] + sv[h][:, :hd]) * lanes(inv)
        if final:
            ht = hh + hprev_ref[:, hsl[h]]
            hn = ht * lax.rsqrt(jnp.mean(ht * ht, axis=-1, keepdims=True) + EPS)
            y = hn * gain_ref[:, hsl[h]] * jax.nn.sigmoid(o_ref[:, hsl[h]].astype(F32)) \
                * _silu(z_ref[:, hsl[h]].astype(F32))
            out_ref[:, hsl[h]] = y.astype(BF16)
        else:
            out_ref[:, hsl[h]] = hh

    for h in heads:
        c_s[h] = decay[h] * c_old[h] + kv[h]
        m_s[h] = jnp.broadcast_to(m_new[h], m_s.shape[1:])


def _mlstm_pass(ya, gate_arrays, gain, hprev, batch, tiles_per_batch, reverse):
    m = ya.shape[0]
    lat = tiles_per_batch - 1
    final = hprev is not None

    def rows(b, c):
        r = jnp.where(c == 0, lat, (lat - c) if reverse else (c - 1))
        return b * tiles_per_batch + r

    def col(cidx):
        return pl.BlockSpec((ROW_TILE, BRANCH_W), lambda b, c: (rows(b, c), cidx))

    gate_spec = pl.BlockSpec((ROW_TILE, GATE_LANES), lambda b, c: (rows(b, c), 0))
    gate_t_spec = pl.BlockSpec((GATE_LANES, ROW_TILE), lambda b, c: (0, rows(b, c)))
    in_specs = [col(A_MQ), col(A_MK), col(A_MV), gate_spec, gate_t_spec, gate_spec, gate_t_spec]
    args = [ya, ya, ya, *gate_arrays]
    if final:
        in_specs += [col(A_MO), col(A_MZ), pl.BlockSpec((1, BRANCH_W), lambda b, c: (0, 0)),
                     pl.BlockSpec((ROW_TILE, BRANCH_W), lambda b, c: (rows(b, c), 0))]
        args += [ya, ya, gain.reshape(1, BRANCH_W), hprev]
    return pl.pallas_call(
        functools.partial(_mlstm_kernel, reverse=reverse, final=final),
        out_shape=jax.ShapeDtypeStruct((m, BRANCH_W), BF16 if final else F32),
        grid=(batch, tiles_per_batch),
        in_specs=in_specs,
        out_specs=pl.BlockSpec((ROW_TILE, BRANCH_W), lambda b, c: (rows(b, c), 0)),
        scratch_shapes=[pltpu.VMEM((MLSTM_HEADS, MLSTM_HEAD_DIM, MLSTM_HEAD_DIM + MLSTM_AUX), F32),
                        pltpu.VMEM((MLSTM_HEADS, V7X_SUBLANES, V7X_LANES), F32)],
        compiler_params=_cparams(("parallel", "arbitrary")),
        name="mlstm_bwd" if reverse else "mlstm_fwd",
    )(*args)


def _segment_info(r, lat):
    has_prev = jnp.logical_and(r != 0, r != lat)
    has_next = jnp.logical_and(r != lat - 1, r != lat)
    seg_len = jnp.where(r == lat, CTX_LEN, lat * ROW_TILE)
    t0 = jnp.where(r == lat, 0, r * ROW_TILE)
    return has_prev, has_next, seg_len, t0


def _conv_strip(j, r, lat, cu_ref, cb_ref, cc_ref, cz_ref, cu_p, cc_p, cu_n, cc_n, cw_ref):
    has_prev, has_next, _, _ = _segment_info(r, lat)
    rowi = lax.broadcasted_iota(jnp.int32, (ROW_TILE, 1), 0)
    cs = slice(j * V7X_LANES, (j + 1) * V7X_LANES)
    a = cc_ref[:, cs].astype(F32) * cu_ref[:, cs].astype(F32)
    a_prev = jnp.where(has_prev, cc_p[HALO - 1:HALO, cs].astype(F32) * cu_p[HALO - 1:HALO, cs].astype(F32), 0.0)
    a_next = jnp.where(has_next, cc_n[0:1, cs].astype(F32) * cu_n[0:1, cs].astype(F32), 0.0)
    a_m1 = jnp.where(rowi == 0, a_prev, pltpu.roll(a, 1, 0))
    a_p1 = jnp.where(rowi == ROW_TILE - 1, a_next, pltpu.roll(a, ROW_TILE - 1, 0))
    y = cw_ref[0:1, cs] * a_m1 + cw_ref[1:2, cs] * a + cw_ref[2:3, cs] * a_p1
    return (cb_ref[:, cs].astype(F32) * y * _silu(cz_ref[:, cs].astype(F32))).astype(BF16)


def _pool_branch(r, lat, pu_ref, pz_ref, pu_p, pu_n, pw_ref, ps_ref, band_ref, yd_ref):
    has_prev, has_next, seg_len, t0 = _segment_info(r, lat)
    rowi = lax.broadcasted_iota(jnp.int32, (ROW_TILE, 1), 0)
    u = pu_ref[...]
    halo_zero = jnp.zeros((HALO, BRANCH_W), u.dtype)
    ext = jnp.concatenate([u, jnp.where(has_prev, pu_p[...], halo_zero), jnp.where(has_next, pu_n[...], halo_zero),
                           jnp.zeros((POOL_K - ROW_TILE - 2 * HALO, BRANCH_W), u.dtype)], axis=0)
    t = t0 + rowi
    groups = range(len(POOL_WINDOWS))
    gsl = [slice(g * POOL_GROUP, (g + 1) * POOL_GROUP) for g in groups]
    acc = [jnp.dot(band_ref[g], ext[:, gsl[g]], preferred_element_type=F32) for g in groups]
    dev = []
    for g, w in enumerate(POOL_WINDOWS):
        inv_cnt = 1.0 / (jnp.minimum(t + (w - w // 2), seg_len) - jnp.maximum(t - w // 2, 0)).astype(F32)
        dev.append((acc[g] * inv_cnt - u[:, gsl[g]].astype(F32)).astype(BF16))
    pg = [jnp.dot(dev[g], pw_ref[g], preferred_element_type=F32) for g in groups]
    for g in groups:
        gate = _silu(pz_ref[:, gsl[g]].astype(F32)) * ps_ref[:, gsl[g]]
        yd_ref[:, gsl[g]] = (pg[g] * gate).astype(BF16)


def _pool_band():
    t = np.arange(ROW_TILE)[:, None]
    pos = np.concatenate([np.arange(ROW_TILE), np.arange(-HALO, 0), np.arange(ROW_TILE, ROW_TILE + HALO),
                          np.full(POOL_K - ROW_TILE - 2 * HALO, -10 * ROW_TILE)])[None, :]
    return np.stack([(pos >= t - w // 2) & (pos < t + w - w // 2) for w in POOL_WINDOWS]).astype(np.float32)


def _mergeout_kernel(*refs, final, tiles_per_batch):
    (b0, b1, cu_ref, cb_ref, cc_ref, cz_ref, pu_ref, pz_ref, cu_p, cc_p, pu_p, cu_n, cc_n, pu_n,
     cw_ref, pw_ref, ps_ref, band_ref, g_ref, wb_ref, wo_ref, x_ref, gt_ref) = refs[:23]
    if final:
        fg_ref, o_ref, yd_s = refs[23:]
    else:
        ng_ref, nsc_ref, nsh_ref, o_ref, h_ref, yd_s = refs[23:]
    r, lat = pl.program_id(1), tiles_per_batch - 1

    def gated(j, br):
        gate = jax.nn.sigmoid(g_ref[:, j * D_MODEL:(j + 1) * D_MODEL].astype(F32))
        return gate * jnp.dot(br, wb_ref[j], preferred_element_type=F32)

    acc = gated(0, b0[...]) + gated(1, b1[...])
    strips = [_conv_strip(j, r, lat, cu_ref, cb_ref, cc_ref, cz_ref, cu_p, cc_p, cu_n, cc_n, cw_ref)
              for j in range(BRANCH_W // V7X_LANES)]
    _pool_branch(r, lat, pu_ref, pz_ref, pu_p, pu_n, pw_ref, ps_ref, band_ref, yd_s)
    acc = acc + gated(2, jnp.concatenate(strips, axis=1)) + gated(3, yd_s[...])
    y = x_ref[...] + gt_ref[0] * jnp.dot(acc.astype(BF16), wo_ref[...], preferred_element_type=F32)
    yn = y * lax.rsqrt(jnp.mean(y * y, axis=-1, keepdims=True) + EPS)
    if final:
        o_ref[...] = yn * fg_ref[...]
    else:
        o_ref[...] = y
        h_ref[...] = (yn * ng_ref[...] * (1.0 + nsc_ref[0]) + nsh_ref[0]).astype(BF16)


def _mergeout(y_att, y_mls, yb, conv_w, pool_w, pool_scale, w_branch, w_out, layer, xs, gt_t, final_gain,
              next_norm, batch, tiles_per_batch):
    m, d = xs.shape
    final = final_gain is not None
    lat = tiles_per_batch - 1
    n_r = lat if final else tiles_per_batch
    resident = pl.Buffered(1)
    hb = ROW_TILE // HALO
    n_halo = m // HALO
    n_win = len(POOL_WINDOWS)

    def src(b, r):
        return b * tiles_per_batch + r

    def col(cidx):
        return pl.BlockSpec((ROW_TILE, BRANCH_W), lambda b, r: (src(b, r), cidx))

    def prev(cidx):
        return pl.BlockSpec((HALO, BRANCH_W), lambda b, r: (jnp.maximum(src(b, r) * hb - 1, 0), cidx))

    def nxt(cidx):
        return pl.BlockSpec((HALO, BRANCH_W), lambda b, r: (jnp.minimum((src(b, r) + 1) * hb, n_halo - 1), cidx))

    in_specs = [col(0), col(0),
                col(B_CU), col(B_CB), col(B_CC), col(B_CZ), col(B_PU), col(B_PZ),
                prev(B_CU), prev(B_CC), prev(B_PU), nxt(B_CU), nxt(B_CC), nxt(B_PU),
                pl.BlockSpec((None, 3, BRANCH_W), lambda b, r: (layer, 0, 0)),
                pl.BlockSpec((None, n_win, POOL_GROUP, POOL_GROUP), lambda b, r: (layer, 0, 0, 0)),
                pl.BlockSpec((None, 1, BRANCH_W), lambda b, r: (layer, 0, 0)),
                pl.BlockSpec((n_win, ROW_TILE, POOL_K), lambda b, r: (0, 0, 0)),
                pl.BlockSpec((ROW_TILE, N_BRANCH * d), lambda b, r: (src(b, r), B_GATE)),
                pl.BlockSpec((None, N_BRANCH, BRANCH_W, d), lambda b, r: (layer, 0, 0, 0), pipeline_mode=resident),
                pl.BlockSpec((None, d, d), lambda b, r: (layer, 0, 0), pipeline_mode=resident),
                pl.BlockSpec((ROW_TILE, d), lambda b, r: (src(b, r), 0)),
                pl.BlockSpec((1, 1, d), lambda b, r: (src(b, r), 0, 0))]
    args = [y_att, y_mls, *([yb] * 12), conv_w, pool_w, pool_scale.reshape(-1, 1, BRANCH_W),
            jnp.asarray(_pool_band(), BF16), yb, w_branch, w_out, xs, gt_t]
    row_spec = pl.BlockSpec((ROW_TILE, d), lambda b, r: (b * n_r + r, 0))
    out_shape = jax.ShapeDtypeStruct((batch * n_r * ROW_TILE, d), F32)
    if final:
        in_specs.append(pl.BlockSpec((1, d), lambda b, r: (0, 0)))
        args.append(final_gain.reshape(1, d))
        out_specs = row_spec
    else:
        tab_spec = pl.BlockSpec((1, 1, d), lambda b, r: (src(b, r), 0, 0))
        in_specs += [pl.BlockSpec((1, d), lambda b, r: (0, 0)), tab_spec, tab_spec]
        args += [next_norm[0].reshape(1, d), next_norm[1], next_norm[2]]
        out_shape = (out_shape, jax.ShapeDtypeStruct((m, d), BF16))
        out_specs = (row_spec, row_spec)
    return pl.pallas_call(
        functools.partial(_mergeout_kernel, final=final, tiles_per_batch=tiles_per_batch),
        out_shape=out_shape,
        grid=(batch, n_r),
        in_specs=in_specs,
        out_specs=out_specs,
        scratch_shapes=[pltpu.VMEM((ROW_TILE, BRANCH_W), BF16)],
        compiler_params=_cparams(("parallel", "parallel")),
        name="mergeout_final" if final else "mergeout",
    )(*args)


def _rope_tables(t_lat):
    pos = np.arange(t_lat)
    quarter = ATT_HEAD_DIM // 4
    freq = ROPE_THETA ** (-jnp.arange(quarter, dtype=F32) / quarter)
    a_row = jnp.asarray(pos // GRID_W, F32)[:, None] * freq[None, :]
    a_col = jnp.asarray(pos % GRID_W, F32)[:, None] * freq[None, :]
    cos_t = jnp.concatenate([jnp.cos(a_row), jnp.cos(a_row), jnp.cos(a_col), jnp.cos(a_col)], axis=-1)
    sin_t = jnp.concatenate([-jnp.sin(a_row), jnp.sin(a_row), -jnp.sin(a_col), jnp.sin(a_col)], axis=-1)
    return cos_t, sin_t


def kernel(x, c, ctx, c_ctx, norm_gain, w_mod, b_mod, w_in, q_norm_gain, k_norm_gain, mlstm_gate_bias,
           mlstm_norm_gain, conv_w, pool_w, pool_scale, w_branch, w_out, final_norm_gain):
    batch, t_lat, d = x.shape
    depth = w_in.shape[0]
    assert d == D_MODEL and ctx.shape[1] == CTX_LEN and t_lat % ROW_TILE == 0 and batch < 8
    s_len = t_lat + CTX_LEN
    tiles_per_batch = s_len // ROW_TILE

    cc = jnp.zeros((8, d), F32).at[:batch].set(c).at[batch].set(c_ctx)
    mod = _modulation(cc, w_mod, b_mod)
    tile_row = np.array([b if r < tiles_per_batch - 1 else batch
                         for b in range(batch) for r in range(tiles_per_batch)], np.int32)
    cos_t, sin_t = _rope_tables(t_lat)
    assert w_in.shape[1:] == (d, N_IN)
    w_t = jnp.swapaxes(w_in, 1, 2).reshape(depth * N_IN, d)
    w_branch_b, w_out_b, pool_w_b = w_branch.astype(BF16), w_out.astype(BF16), pool_w.astype(BF16)

    tables = []
    for l in range(depth):
        mod_t = mod[l][tile_row][:, None, :]
        tables.append((mod_t[..., :d], mod_t[..., d:2 * d], mod_t[..., 2 * d:]))

    out = None
    h, xs = _normmod_first(x, ctx, norm_gain[0], tables[0][1], tables[0][0])
    for l in range(depth):
        gt_t = tables[l][2]
        bias = jnp.zeros((1, GATE_LANES), F32).at[0, :4 * MLSTM_HEADS].set(mlstm_gate_bias[l].reshape(-1))
        ya = _inproj(h, w_t, l, 7, lambda j: jnp.where(j == 0, 0, 512 + IN_UNIT * j), "inproj_a")
        yb = _inproj(h, w_t, l, 14, lambda j: jnp.where(j < 8, W_MERGE_ROW + IN_UNIT * j,
                                                       W_LOCAL_ROW + IN_UNIT * (j - 8)), "inproj_b")
        kv, g, g_t, pre, pre_t, suf, suf_t = _inproj_tail(h, w_t, l, bias)

        qb, kb, vt = _qkvprep(ya, kv, cos_t, sin_t, q_norm_gain[l], k_norm_gain[l], batch, tiles_per_batch)
        y_att = _attention(qb, ya, kb, vt, batch, s_len)
        h_f = _mlstm_pass(ya, (g, g_t, pre, pre_t), None, None, batch, tiles_per_batch, reverse=False)
        y_mls = _mlstm_pass(ya, (g, g_t, suf, suf_t), mlstm_norm_gain[l], h_f, batch, tiles_per_batch,
                            reverse=True)
        if l == depth - 1:
            out = _mergeout(y_att, y_mls, yb, conv_w, pool_w_b, pool_scale, w_branch_b, w_out_b, l, xs, gt_t,
                            final_norm_gain, None, batch, tiles_per_batch).reshape(batch, t_lat, d)
        else:
            next_norm = (norm_gain[l + 1], tables[l + 1][1], tables[l + 1][0])
            xs, h = _mergeout(y_att, y_mls, yb, conv_w, pool_w_b, pool_scale, w_branch_b, w_out_b, l, xs, gt_t,
                              None, next_norm, batch, tiles_per_batch)
    return out
```

```python
import functools

import numpy as np
import jax
import jax.numpy as jnp
from jax import lax
from jax.experimental import pallas as pl
from jax.experimental.pallas import tpu as pltpu

F32 = jnp.float32
BF16 = jnp.bfloat16

D_MODEL = 2048
BRANCH_W = 1024
GRID_W = 64
CTX_LEN = 256
EPS = 1e-6
ATT_HEAD_DIM = 128
ATT_GROUP = 4
ATT_KV_HEADS = 2
ATT_SCALE = ATT_HEAD_DIM ** -0.5
LOG2_E = 1.4426950408889634
ROPE_THETA = 10000.0
MLSTM_HEADS = 4
MLSTM_HEAD_DIM = 256
MLSTM_K_SCALE = MLSTM_HEAD_DIM ** -0.5
MLSTM_AUX = 128
M_INIT = -1e30
POOL_WINDOWS = (2, 4, 8, 16)
POOL_GROUP = 256
N_BRANCH = 4

ROW_TILE = 256
HALO = 16
POOL_K = 384
ATT_Q_TILE = 256
ATT_KEY_CHUNK = 256
VT_ROWS = ATT_HEAD_DIM + 16
V7X_VMEM_LIMIT = 56 * 1024 * 1024
V7X_LANES = 128
V7X_SUBLANES = 8
GATE_LANES = V7X_LANES

A_AQ, A_AZ, A_MQ, A_MK, A_MV, A_MO, A_MZ = range(7)
B_GATE = 0
B_CU, B_CB, B_CC, B_CZ, B_PU, B_PZ = range(8, 14)


def _cparams(sem, vmem=V7X_VMEM_LIMIT):
    return pltpu.CompilerParams(dimension_semantics=sem, vmem_limit_bytes=vmem)


def _silu(x):
    return x * jax.nn.sigmoid(x)


def _mod_kernel(c_ref, w_ref, b_ref, o_ref):
    a = _silu(c_ref[...]).astype(BF16)
    o_ref[...] = jnp.dot(a, w_ref[...].astype(BF16), preferred_element_type=F32) + b_ref[...]


def _modulation(cc, w_mod, b_mod):
    depth, d, n = w_mod.shape
    tn = 768
    return pl.pallas_call(
        _mod_kernel,
        out_shape=jax.ShapeDtypeStruct((depth, 8, n), F32),
        grid=(depth, n // tn),
        in_specs=[pl.BlockSpec((8, d), lambda l, j: (0, 0)),
                  pl.BlockSpec((None, d, tn), lambda l, j: (l, 0, j)),
                  pl.BlockSpec((None, 1, tn), lambda l, j: (l, 0, j))],
        out_specs=pl.BlockSpec((None, 8, tn), lambda l, j: (l, 0, j)),
        compiler_params=_cparams(("parallel", "parallel")),
        name="modulation",
    )(cc, w_mod, b_mod.reshape(depth, 1, n))


def _normmod_first_kernel(x_ref, c_ref, g_ref, sc_ref, sh_ref, o_ref, xs_ref, *, tiles_per_batch):
    x = jnp.where(pl.program_id(1) == tiles_per_batch - 1, c_ref[...], x_ref[...])
    xs_ref[...] = x
    y = x * lax.rsqrt(jnp.mean(x * x, axis=-1, keepdims=True) + EPS) * g_ref[...]
    o_ref[...] = (y * (1.0 + sc_ref[0]) + sh_ref[0]).astype(BF16)


def _normmod_first(x, ctx, gain, sc_t, sh_t):
    batch, t_lat, d = x.shape
    tpb = t_lat // ROW_TILE + 1
    m = batch * tpb * ROW_TILE
    tab_spec = pl.BlockSpec((1, 1, d), lambda b, r: (b * tpb + r, 0, 0))
    row_spec = pl.BlockSpec((ROW_TILE, d), lambda b, r: (b * tpb + r, 0))
    return pl.pallas_call(
        functools.partial(_normmod_first_kernel, tiles_per_batch=tpb),
        out_shape=(jax.ShapeDtypeStruct((m, d), BF16), jax.ShapeDtypeStruct((m, d), F32)),
        grid=(batch, tpb),
        in_specs=[pl.BlockSpec((None, ROW_TILE, d), lambda b, r: (b, jnp.minimum(r, tpb - 2), 0)),
                  pl.BlockSpec((None, CTX_LEN, d), lambda b, r: (b, 0, 0)),
                  pl.BlockSpec((1, d), lambda b, r: (0, 0)),
                  tab_spec, tab_spec],
        out_specs=(row_spec, row_spec),
        compiler_params=_cparams(("parallel", "parallel")),
        name="normmod_first",
    )(x, ctx, gain.reshape(1, d), sc_t, sh_t)


IN_UNIT = 1024
W_KV_ROW, W_MGATE_ROW, W_LOCAL_ROW, W_MERGE_ROW = 1024, 7680, 7696, 13840
N_IN = W_MERGE_ROW + N_BRANCH * D_MODEL


def _pick_tile(n, candidates):
    for t in candidates:
        if n % t == 0:
            return t
    raise ValueError(f"no tile for {n}")


def _inproj_kernel(h_ref, w_ref, o_ref, wt_s):
    @pl.when(pl.program_id(1) == 0)
    def _():
        wt_s[...] = w_ref[...].T.astype(BF16)

    o_ref[...] = jnp.dot(h_ref[...], wt_s[...], preferred_element_type=F32).astype(o_ref.dtype)


def _inproj(h, w_t, layer, n_units, unit_row, name):
    m, k = h.shape
    tm = _pick_tile(m, (2176, 1024, 512, 256))
    base = layer * N_IN
    return pl.pallas_call(
        _inproj_kernel,
        out_shape=jax.ShapeDtypeStruct((m, n_units * IN_UNIT), BF16),
        grid=(n_units, m // tm),
        in_specs=[pl.BlockSpec((tm, k), lambda j, i: (i, 0)),
                  pl.BlockSpec((pl.Element(IN_UNIT), pl.Element(k)),
                               lambda j, i: (pl.multiple_of(base + unit_row(j), 8), 0))],
        out_specs=pl.BlockSpec((tm, IN_UNIT), lambda j, i: (i, j)),
        scratch_shapes=[pltpu.VMEM((k, IN_UNIT), BF16)],
        compiler_params=_cparams(("parallel", "arbitrary"), 62 * 1024 * 1024),
        name=name,
    )(h, w_t)


def _inproj_tail_kernel(h_ref, wkv_ref, wg_ref, bias_ref,
                        kv_out, g_out, gt_out, pre_out, pret_out, suf_out, suft_out, wt_s):
    kvw = wkv_ref.shape[0]
    L = ROW_TILE

    @pl.when(pl.program_id(0) == 0)
    def _():
        wt_s[:, :kvw] = wkv_ref[...].T.astype(BF16)
        wt_s[:, kvw:] = wg_ref[...].T.astype(BF16)

    y = jnp.dot(h_ref[...], wt_s[...], preferred_element_type=F32)
    kv_out[...] = y[:, :kvw]
    g = y[:, kvw:]
    lane = lax.broadcasted_iota(jnp.int32, g.shape, 1)
    gates = jnp.where(lane < 4 * MLSTM_HEADS, g, 0.0) + bias_ref[...]
    log_f = jnp.minimum(gates, 0.0) - jnp.log1p(jnp.exp(-jnp.abs(gates)))
    p1 = log_f.astype(BF16)
    r1 = log_f - p1.astype(F32)
    p2 = r1.astype(BF16)
    p3 = (r1 - p2.astype(F32)).astype(BF16)
    parts = jnp.concatenate([p1, p2, p3], axis=1)
    row = lax.broadcasted_iota(jnp.int32, (L, L), 0)
    col = lax.broadcasted_iota(jnp.int32, (L, L), 1)
    lower = (col <= row).astype(BF16)
    chunks = [slice(c * L, (c + 1) * L) for c in range(g.shape[0] // L)]
    sums = [jnp.dot(lower, parts[rs], preferred_element_type=F32) for rs in chunks]
    gl = GATE_LANES
    pre = [s[:, 0:gl] + s[:, gl:2 * gl] + s[:, 2 * gl:3 * gl] for s in sums]
    suf = [p[L - 1:L, :] - p + log_f[rs] for p, rs in zip(pre, chunks)]
    g_out[...] = gates
    for c, rs in enumerate(chunks):
        pre_out[rs, :] = pre[c]
        suf_out[rs, :] = suf[c]
        gt_out[:, rs] = gates[rs].T
        pret_out[:, rs] = pre[c].T
        suft_out[:, rs] = suf[c].T


def _inproj_tail(h, w_t, layer, bias):
    m, k = h.shape
    tm = _pick_tile(m, (1024, 512, 256))
    kvw = 2 * ATT_KV_HEADS * ATT_HEAD_DIM
    gl = GATE_LANES
    row_sd, col_sd = jax.ShapeDtypeStruct((m, gl), F32), jax.ShapeDtypeStruct((gl, m), F32)
    row_spec = pl.BlockSpec((tm, gl), lambda i: (i, 0))
    col_spec = pl.BlockSpec((gl, tm), lambda i: (0, i))
    return pl.pallas_call(
        _inproj_tail_kernel,
        out_shape=(jax.ShapeDtypeStruct((m, kvw), F32), row_sd, col_sd, row_sd, col_sd, row_sd, col_sd),
        grid=(m // tm,),
        in_specs=[pl.BlockSpec((tm, k), lambda i: (i, 0)),
                  pl.BlockSpec((pl.Element(kvw), pl.Element(k)), lambda i: (layer * N_IN + W_KV_ROW, 0)),
                  pl.BlockSpec((pl.Element(gl), pl.Element(k)), lambda i: (layer * N_IN + W_MGATE_ROW, 0)),
                  pl.BlockSpec((1, gl), lambda i: (0, 0))],
        out_specs=(pl.BlockSpec((tm, kvw), lambda i: (i, 0)),
                   row_spec, col_spec, row_spec, col_spec, row_spec, col_spec),
        scratch_shapes=[pltpu.VMEM((k, kvw + gl), BF16)],
        compiler_params=_cparams(("arbitrary",)),
        name="inproj_tail",
    )(h, w_t, w_t, bias)


def _qkvprep_kernel(q_ref, t_ref, cos_ref, sin_ref, qg_ref, kg_ref, perm_ref, q_out, k_out, vt_out, *,
                    tiles_per_batch):
    is_lat = pl.program_id(1) < tiles_per_batch - 1
    cos, sin, kg = cos_ref[...], sin_ref[...], kg_ref[...]
    qg = qg_ref[...] * (ATT_SCALE * LOG2_E)
    hd = ATT_HEAD_DIM
    n_q = ATT_KV_HEADS * ATT_GROUP
    sls = [slice(h * hd, (h + 1) * hd) for h in range(n_q)]
    xs = [q_ref[:, sl].astype(F32) for sl in sls] + [t_ref[:, sl] for sl in sls[:ATT_KV_HEADS]]
    gains = [qg] * n_q + [kg] * ATT_KV_HEADS
    xs = [x * lax.rsqrt(jnp.mean(x * x, axis=-1, keepdims=True) + EPS) * g for x, g in zip(xs, gains)]
    x_all = jnp.concatenate(xs, axis=0)
    hi = x_all.astype(BF16)
    lo = (x_all - hi.astype(F32)).astype(BF16)
    partner = jnp.dot(jnp.concatenate([hi, lo], axis=1), perm_ref[...], preferred_element_type=F32)
    rows = xs[0].shape[0]
    outs = [jnp.where(is_lat, x * cos + partner[i * rows:(i + 1) * rows] * sin, x).astype(BF16)
            for i, x in enumerate(xs)]
    for h in range(n_q):
        q_out[:, sls[h]] = outs[h]
    for h in range(ATT_KV_HEADS):
        k_out[:, sls[h]] = outs[n_q + h]
        vt_out[h, 0:hd, :] = t_ref[:, sls[ATT_KV_HEADS + h]].T.astype(BF16)
        vt_out[h, hd:, :] = jnp.ones((VT_ROWS - hd, ROW_TILE), BF16)


def _qkvprep(ya, kv, cos_t, sin_t, q_gain, k_gain, batch, tiles_per_batch):
    m = kv.shape[0]
    lat_tiles = tiles_per_batch - 1
    kvw = ATT_KV_HEADS * ATT_HEAD_DIM
    rope_spec = pl.BlockSpec((ROW_TILE, ATT_HEAD_DIM), lambda b, r: (jnp.minimum(r, lat_tiles - 1), 0))
    gain_spec = pl.BlockSpec((1, ATT_HEAD_DIM), lambda b, r: (0, 0))
    lane = np.arange(ATT_HEAD_DIM)
    swap = (lane[:, None] == (lane[None, :] ^ (ATT_HEAD_DIM // 4))).astype(np.float32)
    return pl.pallas_call(
        functools.partial(_qkvprep_kernel, tiles_per_batch=tiles_per_batch),
        out_shape=(jax.ShapeDtypeStruct((m, BRANCH_W), BF16),
                   jax.ShapeDtypeStruct((m, kvw), BF16),
                   jax.ShapeDtypeStruct((batch, ATT_KV_HEADS, VT_ROWS, tiles_per_batch * ROW_TILE), BF16)),
        grid=(batch, tiles_per_batch),
        in_specs=[pl.BlockSpec((ROW_TILE, BRANCH_W), lambda b, r: (b * tiles_per_batch + r, A_AQ)),
                  pl.BlockSpec((ROW_TILE, 2 * kvw), lambda b, r: (b * tiles_per_batch + r, 0)),
                  rope_spec, rope_spec, gain_spec, gain_spec,
                  pl.BlockSpec((2 * ATT_HEAD_DIM, ATT_HEAD_DIM), lambda b, r: (0, 0))],
        out_specs=(pl.BlockSpec((ROW_TILE, BRANCH_W), lambda b, r: (b * tiles_per_batch + r, 0)),
                   pl.BlockSpec((ROW_TILE, kvw), lambda b, r: (b * tiles_per_batch + r, 0)),
                   pl.BlockSpec((None, ATT_KV_HEADS, VT_ROWS, ROW_TILE), lambda b, r: (b, 0, 0, r))),
        compiler_params=_cparams(("parallel", "parallel")),
        name="qkvprep",
    )(ya, kv, cos_t, sin_t, q_gain.reshape(1, ATT_HEAD_DIM), k_gain.reshape(1, ATT_HEAD_DIM),
      jnp.asarray(np.tile(swap, (2, 1)), BF16))


def _attn_kernel(q_ref, z_ref, k_ref, vt_ref, o_ref, *, n_lat_q, t_lat, kc):
    is_lat = pl.program_id(2) < n_lat_q
    hd = ATT_HEAD_DIM
    q = jnp.concatenate([q_ref[:, g * hd:(g + 1) * hd] for g in range(ATT_GROUP)], axis=0)

    def attend(chunks):
        def scores(c):
            off, n = chunks[c]
            return lax.dot_general(k_ref[off:off + n, :], q, (((1,), (1,)), ((), ())),
                                   preferred_element_type=F32)

        m, o = None, None
        s_next = scores(0)
        for c, (off, n) in enumerate(chunks):
            s = s_next
            if c + 1 < len(chunks):
                s_next = scores(c + 1)
            mc = jnp.max(s, axis=0, keepdims=True)
            m_new = mc if m is None else jnp.maximum(m, mc)
            p = jnp.exp2((s - m_new).astype(BF16))
            oc = jnp.dot(vt_ref[:, off:off + n], p, preferred_element_type=F32)
            o = oc if o is None else jnp.exp2(m - m_new) * o + oc
            m = m_new
        o = o[0:hd] / o[hd:hd + 1]
        for g in range(ATT_GROUP):
            sl = slice(g * ATT_HEAD_DIM, (g + 1) * ATT_HEAD_DIM)
            og = o[:, g * ATT_Q_TILE:(g + 1) * ATT_Q_TILE].T
            o_ref[:, sl] = (og * _silu(z_ref[:, sl].astype(F32))).astype(BF16)

    ctx_chunk = (t_lat, CTX_LEN)

    @pl.when(is_lat)
    def _():
        attend([(off, kc) for off in range(0, t_lat, kc)] + [ctx_chunk])

    @pl.when(jnp.logical_not(is_lat))
    def _():
        attend([ctx_chunk])


def _attention(qb, ya, kb, vt, batch, s_len):
    m = ya.shape[0]
    t_lat = s_len - CTX_LEN
    nq = s_len // ATT_Q_TILE
    n_lat_q = t_lat // ATT_Q_TILE
    gw = ATT_GROUP * ATT_HEAD_DIM
    return pl.pallas_call(
        functools.partial(_attn_kernel, n_lat_q=n_lat_q, t_lat=t_lat,
                          kc=_pick_tile(t_lat, (ATT_KEY_CHUNK, 512, 256))),
        out_shape=jax.ShapeDtypeStruct((m, BRANCH_W), BF16),
        grid=(batch, ATT_KV_HEADS, nq),
        in_specs=[pl.BlockSpec((ATT_Q_TILE, gw), lambda b, h, i: (b * nq + i, h)),
                  pl.BlockSpec((ATT_Q_TILE, gw), lambda b, h, i: (b * nq + i, A_AZ * 2 + h)),
                  pl.BlockSpec((s_len, ATT_HEAD_DIM), lambda b, h, i: (b, h)),
                  pl.BlockSpec((None, None, VT_ROWS, s_len), lambda b, h, i: (b, h, 0, 0))],
        out_specs=pl.BlockSpec((ATT_Q_TILE, gw), lambda b, h, i: (b * nq + i, h)),
        compiler_params=_cparams(("parallel", "parallel", "arbitrary")),
        name="attention",
    )(qb, ya, kb, vt)


def _mlstm_kernel(*refs, reverse, final):
    if final:
        (q_ref, k_ref, v_ref, g_ref, gt_ref, bc_ref, bct_ref, o_ref, z_ref, gain_ref, hprev_ref,
         out_ref, c_s, m_s) = refs
    else:
        q_ref, k_ref, v_ref, g_ref, gt_ref, bc_ref, bct_ref, out_ref, c_s, m_s = refs
    L = ROW_TILE
    hd = MLSTM_HEAD_DIM
    aux = MLSTM_AUX

    @pl.when(pl.program_id(1) == 0)
    def _():
        c_s[...] = jnp.zeros(c_s.shape, F32)
        m_s[...] = jnp.full(m_s.shape, M_INIT, F32)

    def lanes(x):
        return jnp.concatenate([x] * (hd // aux), axis=1)

    gates, gates_t = g_ref[...], gt_ref[...]
    bcum, bcum_t = bc_ref[...], bct_ref[...]
    row = lax.broadcasted_iota(jnp.int32, (L, L), 0)
    col = lax.broadcasted_iota(jnp.int32, (L, L), 1)
    tri = (col >= row) if reverse else (col <= row)
    last = 0 if reverse else L - 1
    heads = range(MLSTM_HEADS)
    hsl = [slice(h * hd, (h + 1) * hd) for h in heads]
    icol = [(2 if reverse else 0) * MLSTM_HEADS + h for h in heads]
    fcol = [(3 if reverse else 1) * MLSTM_HEADS + h for h in heads]
    c_old = [c_s[h] for h in heads]
    m_old = [m_s[h, 0:1, 0:1] for h in heads]
    q = [q_ref[:, hsl[h]] for h in heads]
    k = [k_ref[:, hsl[h]].astype(F32) * MLSTM_K_SCALE for h in heads]
    ones = jnp.ones((L, aux), v_ref.dtype)
    v = [jnp.concatenate([v_ref[:, hsl[h]], ones], axis=1) for h in heads]
    b_col = [bcum[:, fcol[h]:fcol[h] + 1] for h in heads]
    i_col = [gates[:, icol[h]:icol[h] + 1] for h in heads]

    qk = [lax.dot_general(q[h], k[h].astype(BF16), (((1,), (1,)), ((), ())), preferred_element_type=F32)
          for h in heads]
    qc = [jnp.dot(q[h], c_old[h].astype(BF16), preferred_element_type=F32) for h in heads]

    g_rep, sm = [], []
    for h in heads:
        a_row = gates_t[icol[h]:icol[h] + 1, :] - bcum_t[fcol[h]:fcol[h] + 1, :]
        masked = jnp.where(tri, a_row, -jnp.inf)
        g_t = jnp.maximum(m_old[h], jnp.max(masked, axis=1, keepdims=True))
        g_rep.append(jnp.broadcast_to(g_t, (L, aux)))
        sm.append((qk[h] * jnp.exp(masked - lanes(g_rep[h]))).astype(BF16))
    sv = [jnp.dot(sm[h], v[h], preferred_element_type=F32) for h in heads]

    kw, decay, m_new = [], [], []
    for h in heads:
        b_last = bcum[last:last + 1, fcol[h]:fcol[h] + 1]
        log_w = b_last - b_col[h] + i_col[h]
        m_new.append(jnp.maximum(b_last + m_old[h], jnp.max(log_w, axis=0, keepdims=True)))
        decay.append(jnp.exp(b_last + m_old[h] - m_new[h]))
        kw.append((k[h] * jnp.exp(log_w - m_new[h])).astype(BF16))
    kv = [lax.dot_general(kw[h], v[h], (((0,), (0,)), ((), ())), preferred_element_type=F32) for h in heads]

    for h in heads:
        inter = jnp.exp(m_old[h] - g_rep[h])
        floor = jnp.exp(-(jnp.broadcast_to(b_col[h], (L, aux)) + g_rep[h]))
        den = inter * qc[h][:, hd:] + sv[h][:, hd:]
        inv = 1.0 / jnp.maximum(jnp.abs(den), floor)
        hh = (lanes(inter) * qc[h][:, :hd] + sv[h][:, :hd]) * lanes(inv)
        if final:
            ht = hh + hprev_ref[:, hsl[h]]
            hn = ht * lax.rsqrt(jnp.mean(ht * ht, axis=-1, keepdims=True) + EPS)
            y = hn * gain_ref[:, hsl[h]] * jax.nn.sigmoid(o_ref[:, hsl[h]].astype(F32)) \
                * _silu(z_ref[:, hsl[h]].astype(F32))
            out_ref[:, hsl[h]] = y.astype(BF16)
        else:
            out_ref[:, hsl[h]] = hh

    for h in heads:
        c_s[h] = decay[h] * c_old[h] + kv[h]
        m_s[h] = jnp.broadcast_to(m_new[h], m_s.shape[1:])


def _mlstm_pass(ya, gate_arrays, gain, hprev, batch, tiles_per_batch, reverse):
    m = ya.shape[0]
    lat = tiles_per_batch - 1
    final = hprev is not None

    def rows(b, c):
        r = jnp.where(c == 0, lat, (lat - c) if reverse else (c - 1))
        return b * tiles_per_batch + r

    def col(cidx):
        return pl.BlockSpec((ROW_TILE, BRANCH_W), lambda b, c: (rows(b, c), cidx))

    gate_spec = pl.BlockSpec((ROW_TILE, GATE_LANES), lambda b, c: (rows(b, c), 0))
    gate_t_spec = pl.BlockSpec((GATE_LANES, ROW_TILE), lambda b, c: (0, rows(b, c)))
    in_specs = [col(A_MQ), col(A_MK), col(A_MV), gate_spec, gate_t_spec, gate_spec, gate_t_spec]
    args = [ya, ya, ya, *gate_arrays]
    if final:
        in_specs += [col(A_MO), col(A_MZ), pl.BlockSpec((1, BRANCH_W), lambda b, c: (0, 0)),
                     pl.BlockSpec((ROW_TILE, BRANCH_W), lambda b, c: (rows(b, c), 0))]
        args += [ya, ya, gain.reshape(1, BRANCH_W), hprev]
    return pl.pallas_call(
        functools.partial(_mlstm_kernel, reverse=reverse, final=final),
        out_shape=jax.ShapeDtypeStruct((m, BRANCH_W), BF16 if final else F32),
        grid=(batch, tiles_per_batch),
        in_specs=in_specs,
        out_specs=pl.BlockSpec((ROW_TILE, BRANCH_W), lambda b, c: (rows(b, c), 0)),
        scratch_shapes=[pltpu.VMEM((MLSTM_HEADS, MLSTM_HEAD_DIM, MLSTM_HEAD_DIM + MLSTM_AUX), F32),
                        pltpu.VMEM((MLSTM_HEADS, V7X_SUBLANES, V7X_LANES), F32)],
        compiler_params=_cparams(("parallel", "arbitrary")),
        name="mlstm_bwd" if reverse else "mlstm_fwd",
    )(*args)


def _segment_info(r, lat):
    has_prev = jnp.logical_and(r != 0, r != lat)
    has_next = jnp.logical_and(r != lat - 1, r != lat)
    seg_len = jnp.where(r == lat, CTX_LEN, lat * ROW_TILE)
    t0 = jnp.where(r == lat, 0, r * ROW_TILE)
    return has_prev, has_next, seg_len, t0


def _conv_strip(j, r, lat, cu_ref, cb_ref, cc_ref, cz_ref, cu_p, cc_p, cu_n, cc_n, cw_ref):
    has_prev, has_next, _, _ = _segment_info(r, lat)
    rowi = lax.broadcasted_iota(jnp.int32, (ROW_TILE, 1), 0)
    cs = slice(j * V7X_LANES, (j + 1) * V7X_LANES)
    a = cc_ref[:, cs].astype(F32) * cu_ref[:, cs].astype(F32)
    a_prev = jnp.where(has_prev, cc_p[HALO - 1:HALO, cs].astype(F32) * cu_p[HALO - 1:HALO, cs].astype(F32), 0.0)
    a_next = jnp.where(has_next, cc_n[0:1, cs].astype(F32) * cu_n[0:1, cs].astype(F32), 0.0)
    a_m1 = jnp.where(rowi == 0, a_prev, pltpu.roll(a, 1, 0))
    a_p1 = jnp.where(rowi == ROW_TILE - 1, a_next, pltpu.roll(a, ROW_TILE - 1, 0))
    y = cw_ref[0:1, cs] * a_m1 + cw_ref[1:2, cs] * a + cw_ref[2:3, cs] * a_p1
    return (cb_ref[:, cs].astype(F32) * y * _silu(cz_ref[:, cs].astype(F32))).astype(BF16)


def _pool_branch(r, lat, pu_ref, pz_ref, pu_p, pu_n, pw_ref, ps_ref, band_ref, yd_ref):
    has_prev, has_next, seg_len, t0 = _segment_info(r, lat)
    rowi = lax.broadcasted_iota(jnp.int32, (ROW_TILE, 1), 0)
    u = pu_ref[...]
    halo_zero = jnp.zeros((HALO, BRANCH_W), u.dtype)
    ext = jnp.concatenate([u, jnp.where(has_prev, pu_p[...], halo_zero), jnp.where(has_next, pu_n[...], halo_zero),
                           jnp.zeros((POOL_K - ROW_TILE - 2 * HALO, BRANCH_W), u.dtype)], axis=0)
    t = t0 + rowi
    groups = range(len(POOL_WINDOWS))
    gsl = [slice(g * POOL_GROUP, (g + 1) * POOL_GROUP) for g in groups]
    acc = [jnp.dot(band_ref[g], ext[:, gsl[g]], preferred_element_type=F32) for g in groups]
    dev = []
    for g, w in enumerate(POOL_WINDOWS):
        inv_cnt = 1.0 / (jnp.minimum(t + (w - w // 2), seg_len) - jnp.maximum(t - w // 2, 0)).astype(F32)
        dev.append((acc[g] * inv_cnt - u[:, gsl[g]].astype(F32)).astype(BF16))
    pg = [jnp.dot(dev[g], pw_ref[g], preferred_element_type=F32) for g in groups]
    for g in groups:
        gate = _silu(pz_ref[:, gsl[g]].astype(F32)) * ps_ref[:, gsl[g]]
        yd_ref[:, gsl[g]] = (pg[g] * gate).astype(BF16)


def _pool_band():
    t = np.arange(ROW_TILE)[:, None]
    pos = np.concatenate([np.arange(ROW_TILE), np.arange(-HALO, 0), np.arange(ROW_TILE, ROW_TILE + HALO),
                          np.full(POOL_K - ROW_TILE - 2 * HALO, -10 * ROW_TILE)])[None, :]
    return np.stack([(pos >= t - w // 2) & (pos < t + w - w // 2) for w in POOL_WINDOWS]).astype(np.float32)


def _mergeout_kernel(*refs, final, tiles_per_batch):
    (b0, b1, cu_ref, cb_ref, cc_ref, cz_ref, pu_ref, pz_ref, cu_p, cc_p, pu_p, cu_n, cc_n, pu_n,
     cw_ref, pw_ref, ps_ref, band_ref, g_ref, wb_ref, wo_ref, x_ref, gt_ref) = refs[:23]
    if final:
        fg_ref, o_ref, yd_s = refs[23:]
    else:
        ng_ref, nsc_ref, nsh_ref, o_ref, h_ref, yd_s = refs[23:]
    r, lat = pl.program_id(1), tiles_per_batch - 1

    def gated(j, br):
        gate = jax.nn.sigmoid(g_ref[:, j * D_MODEL:(j + 1) * D_MODEL].astype(F32))
        return gate * jnp.dot(br, wb_ref[j], preferred_element_type=F32)

    acc = gated(0, b0[...]) + gated(1, b1[...])
    strips = [_conv_strip(j, r, lat, cu_ref, cb_ref, cc_ref, cz_ref, cu_p, cc_p, cu_n, cc_n, cw_ref)
              for j in range(BRANCH_W // V7X_LANES)]
    _pool_branch(r, lat, pu_ref, pz_ref, pu_p, pu_n, pw_ref, ps_ref, band_ref, yd_s)
    acc = acc + gated(2, jnp.concatenate(strips, axis=1)) + gated(3, yd_s[...])
    y = x_ref[...] + gt_ref[0] * jnp.dot(acc.astype(BF16), wo_ref[...], preferred_element_type=F32)
    yn = y * lax.rsqrt(jnp.mean(y * y, axis=-1, keepdims=True) + EPS)
    if final:
        o_ref[...] = yn * fg_ref[...]
    else:
        o_ref[...] = y
        h_ref[...] = (yn * ng_ref[...] * (1.0 + nsc_ref[0]) + nsh_ref[0]).astype(BF16)


def _mergeout(y_att, y_mls, yb, conv_w, pool_w, pool_scale, w_branch, w_out, layer, xs, gt_t, final_gain,
              next_norm, batch, tiles_per_batch):
    m, d = xs.shape
    final = final_gain is not None
    lat = tiles_per_batch - 1
    n_r = lat if final else tiles_per_batch
    resident = pl.Buffered(1)
    hb = ROW_TILE // HALO
    n_halo = m // HALO
    n_win = len(POOL_WINDOWS)

    def src(b, r):
        return b * tiles_per_batch + r

    def col(cidx):
        return pl.BlockSpec((ROW_TILE, BRANCH_W), lambda b, r: (src(b, r), cidx))

    def prev(cidx):
        return pl.BlockSpec((HALO, BRANCH_W), lambda b, r: (jnp.maximum(src(b, r) * hb - 1, 0), cidx))

    def nxt(cidx):
        return pl.BlockSpec((HALO, BRANCH_W), lambda b, r: (jnp.minimum((src(b, r) + 1) * hb, n_halo - 1), cidx))

    in_specs = [col(0), col(0),
                col(B_CU), col(B_CB), col(B_CC), col(B_CZ), col(B_PU), col(B_PZ),
                prev(B_CU), prev(B_CC), prev(B_PU), nxt(B_CU), nxt(B_CC), nxt(B_PU),
                pl.BlockSpec((None, 3, BRANCH_W), lambda b, r: (layer, 0, 0)),
                pl.BlockSpec((None, n_win, POOL_GROUP, POOL_GROUP), lambda b, r: (layer, 0, 0, 0)),
                pl.BlockSpec((None, 1, BRANCH_W), lambda b, r: (layer, 0, 0)),
                pl.BlockSpec((n_win, ROW_TILE, POOL_K), lambda b, r: (0, 0, 0)),
                pl.BlockSpec((ROW_TILE, N_BRANCH * d), lambda b, r: (src(b, r), B_GATE)),
                pl.BlockSpec((None, N_BRANCH, BRANCH_W, d), lambda b, r: (layer, 0, 0, 0), pipeline_mode=resident),
                pl.BlockSpec((None, d, d), lambda b, r: (layer, 0, 0), pipeline_mode=resident),
                pl.BlockSpec((ROW_TILE, d), lambda b, r: (src(b, r), 0)),
                pl.BlockSpec((1, 1, d), lambda b, r: (src(b, r), 0, 0))]
    args = [y_att, y_mls, *([yb] * 12), conv_w, pool_w, pool_scale.reshape(-1, 1, BRANCH_W),
            jnp.asarray(_pool_band(), BF16), yb, w_branch, w_out, xs, gt_t]
    row_spec = pl.BlockSpec((ROW_TILE, d), lambda b, r: (b * n_r + r, 0))
    out_shape = jax.ShapeDtypeStruct((batch * n_r * ROW_TILE, d), F32)
    if final:
        in_specs.append(pl.BlockSpec((1, d), lambda b, r: (0, 0)))
        args.append(final_gain.reshape(1, d))
        out_specs = row_spec
    else:
        tab_spec = pl.BlockSpec((1, 1, d), lambda b, r: (src(b, r), 0, 0))
        in_specs += [pl.BlockSpec((1, d), lambda b, r: (0, 0)), tab_spec, tab_spec]
        args += [next_norm[0].reshape(1, d), next_norm[1], next_norm[2]]
        out_shape = (out_shape, jax.ShapeDtypeStruct((m, d), BF16))
        out_specs = (row_spec, row_spec)
    return pl.pallas_call(
        functools.partial(_mergeout_kernel, final=final, tiles_per_batch=tiles_per_batch),
        out_shape=out_shape,
        grid=(batch, n_r),
        in_specs=in_specs,
        out_specs=out_specs,
        scratch_shapes=[pltpu.VMEM((ROW_TILE, BRANCH_W), BF16)],
        compiler_params=_cparams(("parallel", "parallel")),
        name="mergeout_final" if final else "mergeout",
    )(*args)


def _rope_tables(t_lat):
    pos = np.arange(t_lat)
    quarter = ATT_HEAD_DIM // 4
    freq = ROPE_THETA ** (-jnp.arange(quarter, dtype=F32) / quarter)
    a_row = jnp.asarray(pos // GRID_W, F32)[:, None] * freq[None, :]
    a_col = jnp.asarray(pos % GRID_W, F32)[:, None] * freq[None, :]
    cos_t = jnp.concatenate([jnp.cos(a_row), jnp.cos(a_row), jnp.cos(a_col), jnp.cos(a_col)], axis=-1)
    sin_t = jnp.concatenate([-jnp.sin(a_row), jnp.sin(a_row), -jnp.sin(a_col), jnp.sin(a_col)], axis=-1)
    return cos_t, sin_t


def kernel(x, c, ctx, c_ctx, norm_gain, w_mod, b_mod, w_in, q_norm_gain, k_norm_gain, mlstm_gate_bias,
           mlstm_norm_gain, conv_w, pool_w, pool_scale, w_branch, w_out, final_norm_gain):
    batch, t_lat, d = x.shape
    depth = w_in.shape[0]
    assert d == D_MODEL and ctx.shape[1] == CTX_LEN and t_lat % ROW_TILE == 0 and batch < 8
    s_len = t_lat + CTX_LEN
    tiles_per_batch = s_len // ROW_TILE

    cc = jnp.zeros((8, d), F32).at[:batch].set(c).at[batch].set(c_ctx)
    mod = _modulation(cc, w_mod, b_mod)
    tile_row = np.array([b if r < tiles_per_batch - 1 else batch
                         for b in range(batch) for r in range(tiles_per_batch)], np.int32)
    cos_t, sin_t = _rope_tables(t_lat)
    assert w_in.shape[1:] == (d, N_IN)
    w_t = jnp.swapaxes(w_in, 1, 2).reshape(depth * N_IN, d)
    w_branch_b, w_out_b, pool_w_b = w_branch.astype(BF16), w_out.astype(BF16), pool_w.astype(BF16)

    tables = []
    for l in range(depth):
        mod_t = mod[l][tile_row][:, None, :]
        tables.append((mod_t[..., :d], mod_t[..., d:2 * d], mod_t[..., 2 * d:]))

    out = None
    h, xs = _normmod_first(x, ctx, norm_gain[0], tables[0][1], tables[0][0])
    for l in range(depth):
        gt_t = tables[l][2]
        bias = jnp.zeros((1, GATE_LANES), F32).at[0, :4 * MLSTM_HEADS].set(mlstm_gate_bias[l].reshape(-1))
        ya = _inproj(h, w_t, l, 7, lambda j: jnp.where(j == 0, 0, 512 + IN_UNIT * j), "inproj_a")
        yb = _inproj(h, w_t, l, 14, lambda j: jnp.where(j < 8, W_MERGE_ROW + IN_UNIT * j,
                                                       W_LOCAL_ROW + IN_UNIT * (j - 8)), "inproj_b")
        kv, g, g_t, pre, pre_t, suf, suf_t = _inproj_tail(h, w_t, l, bias)

        qb, kb, vt = _qkvprep(ya, kv, cos_t, sin_t, q_norm_gain[l], k_norm_gain[l], batch, tiles_per_batch)
        y_att = _attention(qb, ya, kb, vt, batch, s_len)
        h_f = _mlstm_pass(ya, (g, g_t, pre, pre_t), None, None, batch, tiles_per_batch, reverse=False)
        y_mls = _mlstm_pass(ya, (g, g_t, suf, suf_t), mlstm_norm_gain[l], h_f, batch, tiles_per_batch,
                            reverse=True)
        if l == depth - 1:
            out = _mergeout(y_att, y_mls, yb, conv_w, pool_w_b, pool_scale, w_branch_b, w_out_b, l, xs, gt_t,
                            final_norm_gain, None, batch, tiles_per_batch).reshape(batch, t_lat, d)
        else:
            next_norm = (norm_gain[l + 1], tables[l + 1][1], tables[l + 1][0])
            xs, h = _mergeout(y_att, y_mls, yb, conv_w, pool_w_b, pool_scale, w_branch_b, w_out_b, l, xs, gt_t,
                              None, next_norm, batch, tiles_per_batch)
    return out
```

```python
import functools

import numpy as np
import jax
import jax.numpy as jnp
from jax import lax
from jax.experimental import pallas as pl
from jax.experimental.pallas import tpu as pltpu

F32 = jnp.float32
BF16 = jnp.bfloat16

D_MODEL = 2048
BRANCH_W = 1024
GRID_W = 64
CTX_LEN = 256
EPS = 1e-6
ATT_HEAD_DIM = 128
ATT_GROUP = 4
ATT_KV_HEADS = 2
ATT_SCALE = ATT_HEAD_DIM ** -0.5
LOG2_E = 1.4426950408889634
ROPE_THETA = 10000.0
MLSTM_HEADS = 4
MLSTM_HEAD_DIM = 256
MLSTM_K_SCALE = MLSTM_HEAD_DIM ** -0.5
MLSTM_AUX = 128
M_INIT = -1e30
POOL_WINDOWS = (2, 4, 8, 16)
POOL_GROUP = 256
N_BRANCH = 4

ROW_TILE = 256
HALO = 16
POOL_K = 384
ATT_Q_TILE = 256
ATT_KEY_CHUNK = 256
VT_ROWS = ATT_HEAD_DIM + 16
V7X_VMEM_LIMIT = 56 * 1024 * 1024
V7X_LANES = 128
V7X_SUBLANES = 8
GATE_LANES = V7X_LANES

A_AQ, A_AZ, A_MQ, A_MK, A_MV, A_MO, A_MZ = range(7)
B_GATE = 0
B_CU, B_CB, B_CC, B_CZ, B_PU, B_PZ = range(8, 14)


def _cparams(sem, vmem=V7X_VMEM_LIMIT):
    return pltpu.CompilerParams(dimension_semantics=sem, vmem_limit_bytes=vmem)


def _silu(x):
    return x * jax.nn.sigmoid(x)


def _mod_kernel(c_ref, w_ref, b_ref, o_ref):
    a = _silu(c_ref[...]).astype(BF16)
    o_ref[...] = jnp.dot(a, w_ref[...].astype(BF16), preferred_element_type=F32) + b_ref[...]


def _modulation(cc, w_mod, b_mod):
    depth, d, n = w_mod.shape
    tn = 768
    return pl.pallas_call(
        _mod_kernel,
        out_shape=jax.ShapeDtypeStruct((depth, 8, n), F32),
        grid=(depth, n // tn),
        in_specs=[pl.BlockSpec((8, d), lambda l, j: (0, 0)),
                  pl.BlockSpec((None, d, tn), lambda l, j: (l, 0, j)),
                  pl.BlockSpec((None, 1, tn), lambda l, j: (l, 0, j))],
        out_specs=pl.BlockSpec((None, 8, tn), lambda l, j: (l, 0, j)),
        compiler_params=_cparams(("parallel", "parallel")),
        name="modulation",
    )(cc, w_mod, b_mod.reshape(depth, 1, n))


def _normmod_first_kernel(x_ref, c_ref, g_ref, sc_ref, sh_ref, o_ref, xs_ref, *, tiles_per_batch):
    x = jnp.where(pl.program_id(1) == tiles_per_batch - 1, c_ref[...], x_ref[...])
    xs_ref[...] = x
    y = x * lax.rsqrt(jnp.mean(x * x, axis=-1, keepdims=True) + EPS) * g_ref[...]
    o_ref[...] = (y * (1.0 + sc_ref[0]) + sh_ref[0]).astype(BF16)


def _normmod_first(x, ctx, gain, sc_t, sh_t):
    batch, t_lat, d = x.shape
    tpb = t_lat // ROW_TILE + 1
    m = batch * tpb * ROW_TILE
    tab_spec = pl.BlockSpec((1, 1, d), lambda b, r: (b * tpb + r, 0, 0))
    row_spec = pl.BlockSpec((ROW_TILE, d), lambda b, r: (b * tpb + r, 0))
    return pl.pallas_call(
        functools.partial(_normmod_first_kernel, tiles_per_batch=tpb),
        out_shape=(jax.ShapeDtypeStruct((m, d), BF16), jax.ShapeDtypeStruct((m, d), F32)),
        grid=(batch, tpb),
        in_specs=[pl.BlockSpec((None, ROW_TILE, d), lambda b, r: (b, jnp.minimum(r, tpb - 2), 0)),
                  pl.BlockSpec((None, CTX_LEN, d), lambda b, r: (b, 0, 0)),
                  pl.BlockSpec((1, d), lambda b, r: (0, 0)),
                  tab_spec, tab_spec],
        out_specs=(row_spec, row_spec),
        compiler_params=_cparams(("parallel", "parallel")),
        name="normmod_first",
    )(x, ctx, gain.reshape(1, d), sc_t, sh_t)


IN_UNIT = 1024
W_KV_ROW, W_MGATE_ROW, W_LOCAL_ROW, W_MERGE_ROW = 1024, 7680, 7696, 13840
N_IN = W_MERGE_ROW + N_BRANCH * D_MODEL


def _pick_tile(n, candidates):
    for t in candidates:
        if n % t == 0:
            return t
    raise ValueError(f"no tile for {n}")


def _inproj_kernel(h_ref, w_ref, o_ref, wt_s):
    @pl.when(pl.program_id(1) == 0)
    def _():
        wt_s[...] = w_ref[...].T.astype(BF16)

    o_ref[...] = jnp.dot(h_ref[...], wt_s[...], preferred_element_type=F32).astype(o_ref.dtype)


def _inproj(h, w_t, layer, n_units, unit_row, name):
    m, k = h.shape
    tm = _pick_tile(m, (2176, 1024, 512, 256))
    base = layer * N_IN
    return pl.pallas_call(
        _inproj_kernel,
        out_shape=jax.ShapeDtypeStruct((m, n_units * IN_UNIT), BF16),
        grid=(n_units, m // tm),
        in_specs=[pl.BlockSpec((tm, k), lambda j, i: (i, 0)),
                  pl.BlockSpec((pl.Element(IN_UNIT), pl.Element(k)),
                               lambda j, i: (pl.multiple_of(base + unit_row(j), 8), 0))],
        out_specs=pl.BlockSpec((tm, IN_UNIT), lambda j, i: (i, j)),
        scratch_shapes=[pltpu.VMEM((k, IN_UNIT), BF16)],
        compiler_params=_cparams(("parallel", "arbitrary"), 62 * 1024 * 1024),
        name=name,
    )(h, w_t)


def _inproj_tail_kernel(h_ref, wkv_ref, wg_ref, bias_ref,
                        kv_out, g_out, gt_out, pre_out, pret_out, suf_out, suft_out, wt_s):
    kvw = wkv_ref.shape[0]
    L = ROW_TILE

    @pl.when(pl.program_id(0) == 0)
    def _():
        wt_s[:, :kvw] = wkv_ref[...].T.astype(BF16)
        wt_s[:, kvw:] = wg_ref[...].T.astype(BF16)

    y = jnp.dot(h_ref[...], wt_s[...], preferred_element_type=F32)
    kv_out[...] = y[:, :kvw]
    g = y[:, kvw:]
    lane = lax.broadcasted_iota(jnp.int32, g.shape, 1)
    gates = jnp.where(lane < 4 * MLSTM_HEADS, g, 0.0) + bias_ref[...]
    log_f = jnp.minimum(gates, 0.0) - jnp.log1p(jnp.exp(-jnp.abs(gates)))
    p1 = log_f.astype(BF16)
    r1 = log_f - p1.astype(F32)
    p2 = r1.astype(BF16)
    p3 = (r1 - p2.astype(F32)).astype(BF16)
    parts = jnp.concatenate([p1, p2, p3], axis=1)
    row = lax.broadcasted_iota(jnp.int32, (L, L), 0)
    col = lax.broadcasted_iota(jnp.int32, (L, L), 1)
    lower = (col <= row).astype(BF16)
    chunks = [slice(c * L, (c + 1) * L) for c in range(g.shape[0] // L)]
    sums = [jnp.dot(lower, parts[rs], preferred_element_type=F32) for rs in chunks]
    gl = GATE_LANES
    pre = [s[:, 0:gl] + s[:, gl:2 * gl] + s[:, 2 * gl:3 * gl] for s in sums]
    suf = [p[L - 1:L, :] - p + log_f[rs] for p, rs in zip(pre, chunks)]
    g_out[...] = gates
    for c, rs in enumerate(chunks):
        pre_out[rs, :] = pre[c]
        suf_out[rs, :] = suf[c]
        gt_out[:, rs] = gates[rs].T
        pret_out[:, rs] = pre[c].T
        suft_out[:, rs] = suf[c].T


def _inproj_tail(h, w_t, layer, bias):
    m, k = h.shape
    tm = _pick_tile(m, (1024, 512, 256))
    kvw = 2 * ATT_KV_HEADS * ATT_HEAD_DIM
    gl = GATE_LANES
    row_sd, col_sd = jax.ShapeDtypeStruct((m, gl), F32), jax.ShapeDtypeStruct((gl, m), F32)
    row_spec = pl.BlockSpec((tm, gl), lambda i: (i, 0))
    col_spec = pl.BlockSpec((gl, tm), lambda i: (0, i))
    return pl.pallas_call(
        _inproj_tail_kernel,
        out_shape=(jax.ShapeDtypeStruct((m, kvw), F32), row_sd, col_sd, row_sd, col_sd, row_sd, col_sd),
        grid=(m // tm,),
        in_specs=[pl.BlockSpec((tm, k), lambda i: (i, 0)),
                  pl.BlockSpec((pl.Element(kvw), pl.Element(k)), lambda i: (layer * N_IN + W_KV_ROW, 0)),
                  pl.BlockSpec((pl.Element(gl), pl.Element(k)), lambda i: (layer * N_IN + W_MGATE_ROW, 0)),
                  pl.BlockSpec((1, gl), lambda i: (0, 0))],
        out_specs=(pl.BlockSpec((tm, kvw), lambda i: (i, 0)),
                   row_spec, col_spec, row_spec, col_spec, row_spec, col_spec),
        scratch_shapes=[pltpu.VMEM((k, kvw + gl), BF16)],
        compiler_params=_cparams(("arbitrary",)),
        name="inproj_tail",
    )(h, w_t, w_t, bias)


def _qkvprep_kernel(q_ref, t_ref, cos_ref, sin_ref, qg_ref, kg_ref, perm_ref, q_out, k_out, vt_out, *,
                    tiles_per_batch):
    is_lat = pl.program_id(1) < tiles_per_batch - 1
    cos, sin, kg = cos_ref[...], sin_ref[...], kg_ref[...]
    qg = qg_ref[...] * (ATT_SCALE * LOG2_E)
    hd = ATT_HEAD_DIM
    n_q = ATT_KV_HEADS * ATT_GROUP
    sls = [slice(h * hd, (h + 1) * hd) for h in range(n_q)]
    xs = [q_ref[:, sl].astype(F32) for sl in sls] + [t_ref[:, sl] for sl in sls[:ATT_KV_HEADS]]
    gains = [qg] * n_q + [kg] * ATT_KV_HEADS
    xs = [x * lax.rsqrt(jnp.mean(x * x, axis=-1, keepdims=True) + EPS) * g for x, g in zip(xs, gains)]
    x_all = jnp.concatenate(xs, axis=0)
    hi = x_all.astype(BF16)
    lo = (x_all - hi.astype(F32)).astype(BF16)
    partner = jnp.dot(jnp.concatenate([hi, lo], axis=1), perm_ref[...], preferred_element_type=F32)
    rows = xs[0].shape[0]
    outs = [jnp.where(is_lat, x * cos + partner[i * rows:(i + 1) * rows] * sin, x).astype(BF16)
            for i, x in enumerate(xs)]
    for h in range(n_q):
        q_out[:, sls[h]] = outs[h]
    for h in range(ATT_KV_HEADS):
        k_out[:, sls[h]] = outs[n_q + h]
        vt_out[h, 0:hd, :] = t_ref[:, sls[ATT_KV_HEADS + h]].T.astype(BF16)
        vt_out[h, hd:, :] = jnp.ones((VT_ROWS - hd, ROW_TILE), BF16)


def _qkvprep(ya, kv, cos_t, sin_t, q_gain, k_gain, batch, tiles_per_batch):
    m = kv.shape[0]
    lat_tiles = tiles_per_batch - 1
    kvw = ATT_KV_HEADS * ATT_HEAD_DIM
    rope_spec = pl.BlockSpec((ROW_TILE, ATT_HEAD_DIM), lambda b, r: (jnp.minimum(r, lat_tiles - 1), 0))
    gain_spec = pl.BlockSpec((1, ATT_HEAD_DIM), lambda b, r: (0, 0))
    lane = np.arange(ATT_HEAD_DIM)
    swap = (lane[:, None] == (lane[None, :] ^ (ATT_HEAD_DIM // 4))).astype(np.float32)
    return pl.pallas_call(
        functools.partial(_qkvprep_kernel, tiles_per_batch=tiles_per_batch),
        out_shape=(jax.ShapeDtypeStruct((m, BRANCH_W), BF16),
                   jax.ShapeDtypeStruct((m, kvw), BF16),
                   jax.ShapeDtypeStruct((batch, ATT_KV_HEADS, VT_ROWS, tiles_per_batch * ROW_TILE), BF16)),
        grid=(batch, tiles_per_batch),
        in_specs=[pl.BlockSpec((ROW_TILE, BRANCH_W), lambda b, r: (b * tiles_per_batch + r, A_AQ)),
                  pl.BlockSpec((ROW_TILE, 2 * kvw), lambda b, r: (b * tiles_per_batch + r, 0)),
                  rope_spec, rope_spec, gain_spec, gain_spec,
                  pl.BlockSpec((2 * ATT_HEAD_DIM, ATT_HEAD_DIM), lambda b, r: (0, 0))],
        out_specs=(pl.BlockSpec((ROW_TILE, BRANCH_W), lambda b, r: (b * tiles_per_batch + r, 0)),
                   pl.BlockSpec((ROW_TILE, kvw), lambda b, r: (b * tiles_per_batch + r, 0)),
                   pl.BlockSpec((None, ATT_KV_HEADS, VT_ROWS, ROW_TILE), lambda b, r: (b, 0, 0, r))),
        compiler_params=_cparams(("parallel", "parallel")),
        name="qkvprep",
    )(ya, kv, cos_t, sin_t, q_gain.reshape(1, ATT_HEAD_DIM), k_gain.reshape(1, ATT_HEAD_DIM),
      jnp.asarray(np.tile(swap, (2, 1)), BF16))


def _attn_kernel(q_ref, z_ref, k_ref, vt_ref, o_ref, *, n_lat_q, t_lat, kc):
    is_lat = pl.program_id(2) < n_lat_q
    hd = ATT_HEAD_DIM
    q = jnp.concatenate([q_ref[:, g * hd:(g + 1) * hd] for g in range(ATT_GROUP)], axis=0)

    def attend(chunks):
        def scores(c):
            off, n = chunks[c]
            return lax.dot_general(k_ref[off:off + n, :], q, (((1,), (1,)), ((), ())),
                                   preferred_element_type=F32)

        m, o = None, None
        s_next = scores(0)
        for c, (off, n) in enumerate(chunks):
            s = s_next
            if c + 1 < len(chunks):
                s_next = scores(c + 1)
            mc = jnp.max(s, axis=0, keepdims=True)
            m_new = mc if m is None else jnp.maximum(m, mc)
            p = jnp.exp2((s - m_new).astype(BF16))
            oc = jnp.dot(vt_ref[:, off:off + n], p, preferred_element_type=F32)
            o = oc if o is None else jnp.exp2(m - m_new) * o + oc
            m = m_new
        o = o[0:hd] / o[hd:hd + 1]
        for g in range(ATT_GROUP):
            sl = slice(g * ATT_HEAD_DIM, (g + 1) * ATT_HEAD_DIM)
            og = o[:, g * ATT_Q_TILE:(g + 1) * ATT_Q_TILE].T
            o_ref[:, sl] = (og * _silu(z_ref[:, sl].astype(F32))).astype(BF16)

    ctx_chunk = (t_lat, CTX_LEN)

    @pl.when(is_lat)
    def _():
        attend([(off, kc) for off in range(0, t_lat, kc)] + [ctx_chunk])

    @pl.when(jnp.logical_not(is_lat))
    def _():
        attend([ctx_chunk])


def _attention(qb, ya, kb, vt, batch, s_len):
    m = ya.shape[0]
    t_lat = s_len - CTX_LEN
    nq = s_len // ATT_Q_TILE
    n_lat_q = t_lat // ATT_Q_TILE
    gw = ATT_GROUP * ATT_HEAD_DIM
    return pl.pallas_call(
        functools.partial(_attn_kernel, n_lat_q=n_lat_q, t_lat=t_lat,
                          kc=_pick_tile(t_lat, (ATT_KEY_CHUNK, 512, 256))),
        out_shape=jax.ShapeDtypeStruct((m, BRANCH_W), BF16),
        grid=(batch, ATT_KV_HEADS, nq),
        in_specs=[pl.BlockSpec((ATT_Q_TILE, gw), lambda b, h, i: (b * nq + i, h)),
                  pl.BlockSpec((ATT_Q_TILE, gw), lambda b, h, i: (b * nq + i, A_AZ * 2 + h)),
                  pl.BlockSpec((s_len, ATT_HEAD_DIM), lambda b, h, i: (b, h)),
                  pl.BlockSpec((None, None, VT_ROWS, s_len), lambda b, h, i: (b, h, 0, 0))],
        out_specs=pl.BlockSpec((ATT_Q_TILE, gw), lambda b, h, i: (b * nq + i, h)),
        compiler_params=_cparams(("parallel", "parallel", "arbitrary")),
        name="attention",
    )(qb, ya, kb, vt)


def _mlstm_kernel(*refs, reverse, final):
    if final:
        (q_ref, k_ref, v_ref, g_ref, gt_ref, bc_ref, bct_ref, o_ref, z_ref, gain_ref, hprev_ref,
         out_ref, c_s, m_s) = refs
    else:
        q_ref, k_ref, v_ref, g_ref, gt_ref, bc_ref, bct_ref, out_ref, c_s, m_s = refs
    L = ROW_TILE
    hd = MLSTM_HEAD_DIM
    aux = MLSTM_AUX

    @pl.when(pl.program_id(1) == 0)
    def _():
        c_s[...] = jnp.zeros(c_s.shape, F32)
        m_s[...] = jnp.full(m_s.shape, M_INIT, F32)

    def lanes(x):
        return jnp.concatenate([x] * (hd // aux), axis=1)

    gates, gates_t = g_ref[...], gt_ref[...]
    bcum, bcum_t = bc_ref[...], bct_ref[...]
    row = lax.broadcasted_iota(jnp.int32, (L, L), 0)
    col = lax.broadcasted_iota(jnp.int32, (L, L), 1)
    tri = (col >= row) if reverse else (col <= row)
    last = 0 if reverse else L - 1
    heads = range(MLSTM_HEADS)
    hsl = [slice(h * hd, (h + 1) * hd) for h in heads]
    icol = [(2 if reverse else 0) * MLSTM_HEADS + h for h in heads]
    fcol = [(3 if reverse else 1) * MLSTM_HEADS + h for h in heads]
    c_old = [c_s[h] for h in heads]
    m_old = [m_s[h, 0:1, 0:1] for h in heads]
    q = [q_ref[:, hsl[h]] for h in heads]
    k = [k_ref[:, hsl[h]].astype(F32) * MLSTM_K_SCALE for h in heads]
    ones = jnp.ones((L, aux), v_ref.dtype)
    v = [jnp.concatenate([v_ref[:, hsl[h]], ones], axis=1) for h in heads]
    b_col = [bcum[:, fcol[h]:fcol[h] + 1] for h in heads]
    i_col = [gates[:, icol[h]:icol[h] + 1] for h in heads]

    qk = [lax.dot_general(q[h], k[h].astype(BF16), (((1,), (1,)), ((), ())), preferred_element_type=F32)
          for h in heads]
    qc = [jnp.dot(q[h], c_old[h].astype(BF16), preferred_element_type=F32) for h in heads]

    g_rep, sm = [], []
    for h in heads:
        a_row = gates_t[icol[h]:icol[h] + 1, :] - bcum_t[fcol[h]:fcol[h] + 1, :]
        masked = jnp.where(tri, a_row, -jnp.inf)
        g_t = jnp.maximum(m_old[h], jnp.max(masked, axis=1, keepdims=True))
        g_rep.append(jnp.broadcast_to(g_t, (L, aux)))
        sm.append((qk[h] * jnp.exp(masked - lanes(g_rep[h]))).astype(BF16))
    sv = [jnp.dot(sm[h], v[h], preferred_element_type=F32) for h in heads]

    kw, decay, m_new = [], [], []
    for h in heads:
        b_last = bcum[last:last + 1, fcol[h]:fcol[h] + 1]
        log_w = b_last - b_col[h] + i_col[h]
        m_new.append(jnp.maximum(b_last + m_old[h], jnp.max(log_w, axis=0, keepdims=True)))
        decay.append(jnp.exp(b_last + m_old[h] - m_new[h]))
        kw.append((k[h] * jnp.exp(log_w - m_new[h])).astype(BF16))
    kv = [lax.dot_general(kw[h], v[h], (((0,), (0,)), ((), ())), preferred_element_type=F32) for h in heads]

    for h in heads:
        inter = jnp.exp(m_old[h] - g_rep[h])
        floor = jnp.exp(-(jnp.broadcast_to(b_col[h], (L, aux)) + g_rep[h]))
        den = inter * qc[h][:, hd:] + sv[h][:, hd:]
        inv = 1.0 / jnp.maximum(jnp.abs(den), floor)
        hh = (lanes(inter) * qc[h][:, :hd] + sv[h][:, :hd]) * lanes(inv)
        if final:
            ht = hh + hprev_ref[:, hsl[h]]
            hn = ht * lax.rsqrt(jnp.mean(ht * ht, axis=-1, keepdims=True) + EPS)
            y = hn * gain_ref[:, hsl[h]] * jax.nn.sigmoid(o_ref[:, hsl[h]].astype(F32)) \
                * _silu(z_ref[:, hsl[h]].astype(F32))
            out_ref[:, hsl[h]] = y.astype(BF16)
        else:
            out_ref[:, hsl[h]] = hh

    for h in heads:
        c_s[h] = decay[h] * c_old[h] + kv[h]
        m_s[h] = jnp.broadcast_to(m_new[h], m_s.shape[1:])


def _mlstm_pass(ya, gate_arrays, gain, hprev, batch, tiles_per_batch, reverse):
    m = ya.shape[0]
    lat = tiles_per_batch - 1
    final = hprev is not None

    def rows(b, c):
        r = jnp.where(c == 0, lat, (lat - c) if reverse else (c - 1))
        return b * tiles_per_batch + r

    def col(cidx):
        return pl.BlockSpec((ROW_TILE, BRANCH_W), lambda b, c: (rows(b, c), cidx))

    gate_spec = pl.BlockSpec((ROW_TILE, GATE_LANES), lambda b, c: (rows(b, c), 0))
    gate_t_spec = pl.BlockSpec((GATE_LANES, ROW_TILE), lambda b, c: (0, rows(b, c)))
    in_specs = [col(A_MQ), col(A_MK), col(A_MV), gate_spec, gate_t_spec, gate_spec, gate_t_spec]
    args = [ya, ya, ya, *gate_arrays]
    if final:
        in_specs += [col(A_MO), col(A_MZ), pl.BlockSpec((1, BRANCH_W), lambda b, c: (0, 0)),
                     pl.BlockSpec((ROW_TILE, BRANCH_W), lambda b, c: (rows(b, c), 0))]
        args += [ya, ya, gain.reshape(1, BRANCH_W), hprev]
    return pl.pallas_call(
        functools.partial(_mlstm_kernel, reverse=reverse, final=final),
        out_shape=jax.ShapeDtypeStruct((m, BRANCH_W), BF16 if final else F32),
        grid=(batch, tiles_per_batch),
        in_specs=in_specs,
        out_specs=pl.BlockSpec((ROW_TILE, BRANCH_W), lambda b, c: (rows(b, c), 0)),
        scratch_shapes=[pltpu.VMEM((MLSTM_HEADS, MLSTM_HEAD_DIM, MLSTM_HEAD_DIM + MLSTM_AUX), F32),
                        pltpu.VMEM((MLSTM_HEADS, V7X_SUBLANES, V7X_LANES), F32)],
        compiler_params=_cparams(("parallel", "arbitrary")),
        name="mlstm_bwd" if reverse else "mlstm_fwd",
    )(*args)


def _segment_info(r, lat):
    has_prev = jnp.logical_and(r != 0, r != lat)
    has_next = jnp.logical_and(r != lat - 1, r != lat)
    seg_len = jnp.where(r == lat, CTX_LEN, lat * ROW_TILE)
    t0 = jnp.where(r == lat, 0, r * ROW_TILE)
    return has_prev, has_next, seg_len, t0


def _conv_strip(j, r, lat, cu_ref, cb_ref, cc_ref, cz_ref, cu_p, cc_p, cu_n, cc_n, cw_ref):
    has_prev, has_next, _, _ = _segment_info(r, lat)
    rowi = lax.broadcasted_iota(jnp.int32, (ROW_TILE, 1), 0)
    cs = slice(j * V7X_LANES, (j + 1) * V7X_LANES)
    a = cc_ref[:, cs].astype(F32) * cu_ref[:, cs].astype(F32)
    a_prev = jnp.where(has_prev, cc_p[HALO - 1:HALO, cs].astype(F32) * cu_p[HALO - 1:HALO, cs].astype(F32), 0.0)
    a_next = jnp.where(has_next, cc_n[0:1, cs].astype(F32) * cu_n[0:1, cs].astype(F32), 0.0)
    a_m1 = jnp.where(rowi == 0, a_prev, pltpu.roll(a, 1, 0))
    a_p1 = jnp.where(rowi == ROW_TILE - 1, a_next, pltpu.roll(a, ROW_TILE - 1, 0))
    y = cw_ref[0:1, cs] * a_m1 + cw_ref[1:2, cs] * a + cw_ref[2:3, cs] * a_p1
    return (cb_ref[:, cs].astype(F32) * y * _silu(cz_ref[:, cs].astype(F32))).astype(BF16)


def _pool_branch(r, lat, pu_ref, pz_ref, pu_p, pu_n, pw_ref, ps_ref, band_ref, yd_ref):
    has_prev, has_next, seg_len, t0 = _segment_info(r, lat)
    rowi = lax.broadcasted_iota(jnp.int32, (ROW_TILE, 1), 0)
    u = pu_ref[...]
    halo_zero = jnp.zeros((HALO, BRANCH_W), u.dtype)
    ext = jnp.concatenate([u, jnp.where(has_prev, pu_p[...], halo_zero), jnp.where(has_next, pu_n[...], halo_zero),
                           jnp.zeros((POOL_K - ROW_TILE - 2 * HALO, BRANCH_W), u.dtype)], axis=0)
    t = t0 + rowi
    groups = range(len(POOL_WINDOWS))
    gsl = [slice(g * POOL_GROUP, (g + 1) * POOL_GROUP) for g in groups]
    acc = [jnp.dot(band_ref[g], ext[:, gsl[g]], preferred_element_type=F32) for g in groups]
    dev = []
    for g, w in enumerate(POOL_WINDOWS):
        inv_cnt = 1.0 / (jnp.minimum(t + (w - w // 2), seg_len) - jnp.maximum(t - w // 2, 0)).astype(F32)
        dev.append((acc[g] * inv_cnt - u[:, gsl[g]].astype(F32)).astype(BF16))
    pg = [jnp.dot(dev[g], pw_ref[g], preferred_element_type=F32) for g in groups]
    for g in groups:
        gate = _silu(pz_ref[:, gsl[g]].astype(F32)) * ps_ref[:, gsl[g]]
        yd_ref[:, gsl[g]] = (pg[g] * gate).astype(BF16)


def _pool_band():
    t = np.arange(ROW_TILE)[:, None]
    pos = np.concatenate([np.arange(ROW_TILE), np.arange(-HALO, 0), np.arange(ROW_TILE, ROW_TILE + HALO),
                          np.full(POOL_K - ROW_TILE - 2 * HALO, -10 * ROW_TILE)])[None, :]
    return np.stack([(pos >= t - w // 2) & (pos < t + w - w // 2) for w in POOL_WINDOWS]).astype(np.float32)


MERGE_COLS = 1024


def _mergeout_kernel(*refs, final, tiles_per_batch):
    (b0, b1, cu_ref, cb_ref, cc_ref, cz_ref, pu_ref, pz_ref, cu_p, cc_p, pu_p, cu_n, cc_n, pu_n,
     cw_ref, pw_ref, ps_ref, band_ref, g_ref, wb_ref, wo_ref, x_ref, gt_ref) = refs[:23]
    if final:
        fg_ref, o_ref, yd_s = refs[23:]
    else:
        ng_ref, nsc_ref, nsh_ref, o_ref, h_ref, yd_s = refs[23:]
    r, lat = pl.program_id(1), tiles_per_batch - 1

    strips = [_conv_strip(j, r, lat, cu_ref, cb_ref, cc_ref, cz_ref, cu_p, cc_p, cu_n, cc_n, cw_ref)
              for j in range(BRANCH_W // V7X_LANES)]
    _pool_branch(r, lat, pu_ref, pz_ref, pu_p, pu_n, pw_ref, ps_ref, band_ref, yd_s)
    branches = (b0[...], b1[...], jnp.concatenate(strips, axis=1), yd_s[...])

    proj = None
    for cols in (slice(c0, c0 + MERGE_COLS) for c0 in range(0, D_MODEL, MERGE_COLS)):
        acc = None
        for j, br in enumerate(branches):
            gate = jax.nn.sigmoid(g_ref[:, j * D_MODEL + cols.start:j * D_MODEL + cols.stop].astype(F32))
            term = gate * jnp.dot(br, wb_ref[j, :, cols], preferred_element_type=F32)
            acc = term if acc is None else acc + term
        part = jnp.dot(acc.astype(BF16), wo_ref[cols, :], preferred_element_type=F32)
        proj = part if proj is None else proj + part
    y = x_ref[...] + gt_ref[0] * proj
    yn = y * lax.rsqrt(jnp.mean(y * y, axis=-1, keepdims=True) + EPS)
    if final:
        o_ref[...] = yn * fg_ref[...]
    else:
        o_ref[...] = y
        h_ref[...] = (yn * ng_ref[...] * (1.0 + nsc_ref[0]) + nsh_ref[0]).astype(BF16)


def _mergeout(y_att, y_mls, yb, conv_w, pool_w, pool_scale, w_branch, w_out, layer, xs, gt_t, final_gain,
              next_norm, batch, tiles_per_batch):
    m, d = xs.shape
    final = final_gain is not None
    lat = tiles_per_batch - 1
    n_r = lat if final else tiles_per_batch
    resident = pl.Buffered(1)
    hb = ROW_TILE // HALO
    n_halo = m // HALO
    n_win = len(POOL_WINDOWS)

    def src(b, r):
        return b * tiles_per_batch + r

    def col(cidx):
        return pl.BlockSpec((ROW_TILE, BRANCH_W), lambda b, r: (src(b, r), cidx))

    def prev(cidx):
        return pl.BlockSpec((HALO, BRANCH_W), lambda b, r: (jnp.maximum(src(b, r) * hb - 1, 0), cidx))

    def nxt(cidx):
        return pl.BlockSpec((HALO, BRANCH_W), lambda b, r: (jnp.minimum((src(b, r) + 1) * hb, n_halo - 1), cidx))

    in_specs = [col(0), col(0),
                col(B_CU), col(B_CB), col(B_CC), col(B_CZ), col(B_PU), col(B_PZ),
                prev(B_CU), prev(B_CC), prev(B_PU), nxt(B_CU), nxt(B_CC), nxt(B_PU),
                pl.BlockSpec((None, 3, BRANCH_W), lambda b, r: (layer, 0, 0)),
                pl.BlockSpec((None, n_win, POOL_GROUP, POOL_GROUP), lambda b, r: (layer, 0, 0, 0)),
                pl.BlockSpec((None, 1, BRANCH_W), lambda b, r: (layer, 0, 0)),
                pl.BlockSpec((n_win, ROW_TILE, POOL_K), lambda b, r: (0, 0, 0)),
                pl.BlockSpec((ROW_TILE, N_BRANCH * d), lambda b, r: (src(b, r), B_GATE)),
                pl.BlockSpec((None, N_BRANCH, BRANCH_W, d), lambda b, r: (layer, 0, 0, 0), pipeline_mode=resident),
                pl.BlockSpec((None, d, d), lambda b, r: (layer, 0, 0), pipeline_mode=resident),
                pl.BlockSpec((ROW_TILE, d), lambda b, r: (src(b, r), 0)),
                pl.BlockSpec((1, 1, d), lambda b, r: (src(b, r), 0, 0))]
    args = [y_att, y_mls, *([yb] * 12), conv_w, pool_w, pool_scale.reshape(-1, 1, BRANCH_W),
            jnp.asarray(_pool_band(), BF16), yb, w_branch, w_out, xs, gt_t]
    row_spec = pl.BlockSpec((ROW_TILE, d), lambda b, r: (b * n_r + r, 0))
    out_shape = jax.ShapeDtypeStruct((batch * n_r * ROW_TILE, d), F32)
    if final:
        in_specs.append(pl.BlockSpec((1, d), lambda b, r: (0, 0)))
        args.append(final_gain.reshape(1, d))
        out_specs = row_spec
    else:
        tab_spec = pl.BlockSpec((1, 1, d), lambda b, r: (src(b, r), 0, 0))
        in_specs += [pl.BlockSpec((1, d), lambda b, r: (0, 0)), tab_spec, tab_spec]
        args += [next_norm[0].reshape(1, d), next_norm[1], next_norm[2]]
        out_shape = (out_shape, jax.ShapeDtypeStruct((m, d), BF16))
        out_specs = (row_spec, row_spec)
    return pl.pallas_call(
        functools.partial(_mergeout_kernel, final=final, tiles_per_batch=tiles_per_batch),
        out_shape=out_shape,
        grid=(batch, n_r),
        in_specs=in_specs,
        out_specs=out_specs,
        scratch_shapes=[pltpu.VMEM((ROW_TILE, BRANCH_W), BF16)],
        compiler_params=_cparams(("parallel", "parallel")),
        name="mergeout_final" if final else "mergeout",
    )(*args)


def _rope_tables(t_lat):
    pos = np.arange(t_lat)
    quarter = ATT_HEAD_DIM // 4
    freq = ROPE_THETA ** (-jnp.arange(quarter, dtype=F32) / quarter)
    a_row = jnp.asarray(pos // GRID_W, F32)[:, None] * freq[None, :]
    a_col = jnp.asarray(pos % GRID_W, F32)[:, None] * freq[None, :]
    cos_t = jnp.concatenate([jnp.cos(a_row), jnp.cos(a_row), jnp.cos(a_col), jnp.cos(a_col)], axis=-1)
    sin_t = jnp.concatenate([-jnp.sin(a_row), jnp.sin(a_row), -jnp.sin(a_col), jnp.sin(a_col)], axis=-1)
    return cos_t, sin_t


def kernel(x, c, ctx, c_ctx, norm_gain, w_mod, b_mod, w_in, q_norm_gain, k_norm_gain, mlstm_gate_bias,
           mlstm_norm_gain, conv_w, pool_w, pool_scale, w_branch, w_out, final_norm_gain):
    batch, t_lat, d = x.shape
    depth = w_in.shape[0]
    assert d == D_MODEL and ctx.shape[1] == CTX_LEN and t_lat % ROW_TILE == 0 and batch < 8
    s_len = t_lat + CTX_LEN
    tiles_per_batch = s_len // ROW_TILE

    cc = jnp.zeros((8, d), F32).at[:batch].set(c).at[batch].set(c_ctx)
    mod = _modulation(cc, w_mod, b_mod)
    tile_row = np.array([b if r < tiles_per_batch - 1 else batch
                         for b in range(batch) for r in range(tiles_per_batch)], np.int32)
    cos_t, sin_t = _rope_tables(t_lat)
    assert w_in.shape[1:] == (d, N_IN)
    w_t = jnp.swapaxes(w_in, 1, 2).reshape(depth * N_IN, d)
    w_branch_b, w_out_b, pool_w_b = w_branch.astype(BF16), w_out.astype(BF16), pool_w.astype(BF16)

    tables = []
    for l in range(depth):
        mod_t = mod[l][tile_row][:, None, :]
        tables.append((mod_t[..., :d], mod_t[..., d:2 * d], mod_t[..., 2 * d:]))

    out = None
    h, xs = _normmod_first(x, ctx, norm_gain[0], tables[0][1], tables[0][0])
    for l in range(depth):
        gt_t = tables[l][2]
        bias = jnp.zeros((1, GATE_LANES), F32).at[0, :4 * MLSTM_HEADS].set(mlstm_gate_bias[l].reshape(-1))
        ya = _inproj(h, w_t, l, 7, lambda j: jnp.where(j == 0, 0, 512 + IN_UNIT * j), "inproj_a")
        yb = _inproj(h, w_t, l, 14, lambda j: jnp.where(j < 8, W_MERGE_ROW + IN_UNIT * j,
                                                       W_LOCAL_ROW + IN_UNIT * (j - 8)), "inproj_b")
        kv, g, g_t, pre, pre_t, suf, suf_t = _inproj_tail(h, w_t, l, bias)

        qb, kb, vt = _qkvprep(ya, kv, cos_t, sin_t, q_norm_gain[l], k_norm_gain[l], batch, tiles_per_batch)
        y_att = _attention(qb, ya, kb, vt, batch, s_len)
        h_f = _mlstm_pass(ya, (g, g_t, pre, pre_t), None, None, batch, tiles_per_batch, reverse=False)
        y_mls = _mlstm_pass(ya, (g, g_t, suf, suf_t), mlstm_norm_gain[l], h_f, batch, tiles_per_batch,
                            reverse=True)
        if l == depth - 1:
            out = _mergeout(y_att, y_mls, yb, conv_w, pool_w_b, pool_scale, w_branch_b, w_out_b, l, xs, gt_t,
                            final_norm_gain, None, batch, tiles_per_batch).reshape(batch, t_lat, d)
        else:
            next_norm = (norm_gain[l + 1], tables[l + 1][1], tables[l + 1][0])
            xs, h = _mergeout(y_att, y_mls, yb, conv_w, pool_w_b, pool_scale, w_branch_b, w_out_b, l, xs, gt_t,
                              None, next_norm, batch, tiles_per_batch)
    return out
```
